```python
import jax, jax.numpy as jnp
from jax import lax
import numpy as np

D_MODEL = 2048
BATCH = 16
SEQ = 256
DEPTH = 1
DEC_BATCH = 8
DEC_SEQ = 1024
PAST_LEN = 512

GRID_W = 64
N_HEADS = 16
N_KV_HEADS = 4
HEAD_DIM = 128
ROPE_THETA = 10000.0
Q_BLOCK = 128
N_RET_HEADS = 8
RET_KEY_DIM = 128
RET_VAL_DIM = 256
RET_CHUNK = 128
D_FF = ((8 * D_MODEL // 3 + 255) // 256) * 256
EPS = 1e-6

ATTN_Q = N_HEADS * HEAD_DIM
ATTN_KV = N_KV_HEADS * HEAD_DIM
RET_QK = N_RET_HEADS * RET_KEY_DIM
RET_V = N_RET_HEADS * RET_VAL_DIM
IN_SPLITS = (ATTN_Q, ATTN_KV, ATTN_KV, RET_QK, RET_QK, RET_V, RET_V, D_MODEL, D_MODEL)
IN_OFFSETS = tuple(sum(IN_SPLITS[:i + 1]) for i in range(len(IN_SPLITS) - 1))
N_IN = sum(IN_SPLITS)

kernel_name = 'hybrid_gqa_retention_prefix_dit_step'


def rms_norm(x, g):
    xf = x.astype(jnp.float32)
    y = xf * lax.rsqrt(jnp.mean(xf * xf, axis=-1, keepdims=True) + EPS)
    return (y * g.astype(jnp.float32)).astype(x.dtype)


def rope_2d(x):
    L = x.shape[1]
    rows = L // GRID_W
    row = jnp.repeat(jnp.arange(rows, dtype=jnp.float32), GRID_W)
    col = jnp.tile(jnp.arange(GRID_W, dtype=jnp.float32), rows)
    half = HEAD_DIM // 2
    quarter = half // 2
    inv_freq = ROPE_THETA ** (-jnp.arange(quarter, dtype=jnp.float32) / quarter)
    xf = x.astype(jnp.float32)

    def rot(xs, pos):
        ang = pos[:, None] * inv_freq[None, :]
        cos = jnp.cos(ang)[None, :, None, :]
        sin = jnp.sin(ang)[None, :, None, :]
        x1, x2 = xs[..., :quarter], xs[..., quarter:]
        return jnp.concatenate([x1 * cos - x2 * sin, x2 * cos + x1 * sin], axis=-1)

    out = jnp.concatenate([rot(xf[..., :half], row), rot(xf[..., half:], col)], axis=-1)
    return out.astype(x.dtype)


def block_attention(q, k, v):
    b, Lq, _, d = q.shape
    grp = N_HEADS // N_KV_HEADS
    n_blk = Lq // Q_BLOCK
    scale = d ** -0.5
    qb = q.reshape(b, n_blk, Q_BLOCK, N_KV_HEADS, grp, d).transpose(1, 0, 2, 3, 4, 5)

    def one_block(q_blk):
        s = jnp.einsum('bqkgd,bskd->bkgqs', q_blk, k).astype(jnp.float32) * scale
        p = jax.nn.softmax(s, axis=-1)
        return jnp.einsum('bkgqs,bskd->bqkgd', p.astype(v.dtype), v)

    o = lax.map(one_block, qb)
    return o.transpose(1, 0, 2, 3, 4, 5).reshape(b, Lq, N_HEADS * d)


def retention_dir(q, k, v, log_g, s0):
    b, h, L, dk = q.shape
    dv = v.shape[-1]
    n_c = L // RET_CHUNK
    qc = q.reshape(b, h, n_c, RET_CHUNK, dk)
    kc = k.reshape(b, h, n_c, RET_CHUNK, dk)
    vc = v.reshape(b, h, n_c, RET_CHUNK, dv)
    idx = jnp.arange(RET_CHUNK, dtype=jnp.float32)
    diff = idx[:, None] - idx[None, :]
    dmat = jnp.where(diff >= 0, jnp.exp(jnp.maximum(diff, 0.0)[None] * log_g[:, None, None]), 0.0)
    scores = jnp.einsum('bhcid,bhcjd->bhcij', qc, kc) * dmat[None, :, None]
    inner = jnp.einsum('bhcij,bhcje->bhcie', scores, vc)
    zeta = jnp.exp((RET_CHUNK - 1 - idx)[None, :] * log_g[:, None])
    u = jnp.einsum('bhcjd,bhcje->cbhde', kc * zeta[None, :, None, :, None], vc)
    g_chunk = jnp.exp(RET_CHUNK * log_g)[None, :, None, None]

    def step(s, u_c):
        return g_chunk * s + u_c, s

    s_fin, s_prev = lax.scan(step, s0, u)
    xi = jnp.exp((idx + 1.0)[None, :] * log_g[:, None])
    cross = jnp.einsum('bhcid,cbhde->bhcie', qc, s_prev) * xi[None, :, None, :, None]
    return (inner + cross).reshape(b, h, L, dv), s_fin


def bidir_retention(q, k, v, log_g_f, log_g_b, s0_f, s0_b):
    o_f, s_f = retention_dir(q, k, v, log_g_f, s0_f)
    o_b, s_b = retention_dir(jnp.flip(q, 2), jnp.flip(k, 2), jnp.flip(v, 2), log_g_b, s0_b)
    return o_f + jnp.flip(o_b, 2), s_f, s_b


def token_mixer(h, w_in, q_norm, k_norm, dec_f, dec_b, ret_norm, w_ba, w_br, w_out,
                ctx_k, ctx_v, s0_f, s0_b, latent):
    b, L, _ = h.shape
    z = h @ w_in
    q_a, k_a, v_a, q_r, k_r, v_r, g_r, gate_a, gate_r = jnp.split(z, IN_OFFSETS, axis=-1)
    q_a = rms_norm(q_a.reshape(b, L, N_HEADS, HEAD_DIM), q_norm)
    k_a = rms_norm(k_a.reshape(b, L, N_KV_HEADS, HEAD_DIM), k_norm)
    v_a = v_a.reshape(b, L, N_KV_HEADS, HEAD_DIM)
    if latent:
        q_use = rope_2d(q_a)
        k_all = jnp.concatenate([ctx_k.astype(k_a.dtype), rope_2d(k_a)], axis=1)
        v_all = jnp.concatenate([ctx_v.astype(v_a.dtype), v_a], axis=1)
    else:
        q_use, k_all, v_all = q_a, k_a, v_a
    o_a = block_attention(q_use, k_all, v_all)
    qr = q_r.reshape(b, L, N_RET_HEADS, RET_KEY_DIM).transpose(0, 2, 1, 3).astype(jnp.float32)
    kr = (k_r.reshape(b, L, N_RET_HEADS, RET_KEY_DIM).transpose(0, 2, 1, 3).astype(jnp.float32)
          * (RET_KEY_DIM ** -0.5))
    vr = v_r.reshape(b, L, N_RET_HEADS, RET_VAL_DIM).transpose(0, 2, 1, 3).astype(jnp.float32)
    if latent:
        s0_f = s0_f.astype(jnp.float32)
        s0_b = s0_b.astype(jnp.float32)
    else:
        s0_f = jnp.zeros((b, N_RET_HEADS, RET_KEY_DIM, RET_VAL_DIM), jnp.float32)
        s0_b = s0_f
    log_g_f = jax.nn.log_sigmoid(dec_f.astype(jnp.float32))
    log_g_b = jax.nn.log_sigmoid(dec_b.astype(jnp.float32))
    o_r, s_f, s_b = bidir_retention(qr, kr, vr, log_g_f, log_g_b, s0_f, s0_b)
    mu = jnp.mean(o_r, axis=-1, keepdims=True)
    var = jnp.mean(jnp.square(o_r - mu), axis=-1, keepdims=True)
    o_r = ((o_r - mu) * lax.rsqrt(var + EPS)).transpose(0, 2, 1, 3).reshape(b, L, RET_V)
    o_r = (o_r * ret_norm.astype(jnp.float32)).astype(h.dtype) * jax.nn.silu(g_r)
    merged = jax.nn.sigmoid(gate_a) * (o_a @ w_ba) + jax.nn.sigmoid(gate_r) * (o_r @ w_br)
    return merged @ w_out, k_a, v_a, s_f.astype(h.dtype), s_b.astype(h.dtype)


def swiglu(h, w_g, w_u, w_d):
    return (jax.nn.silu(h @ w_g) * (h @ w_u)) @ w_d


def layer(x, mod, norm_a, norm_f, w_in, q_norm, k_norm, dec_f, dec_b, ret_norm, w_ba, w_br, w_out,
          w_g, w_u, w_d, ctx_k, ctx_v, s0_f, s0_b, latent):
    sh_a, sc_a, g_a, sh_f, sc_f, g_f = jnp.split(mod, 6, axis=-1)
    h = rms_norm(x, norm_a) * (1.0 + sc_a) + sh_a
    mix, k_a, v_a, s_f, s_b = token_mixer(h, w_in, q_norm, k_norm, dec_f, dec_b, ret_norm, w_ba, w_br,
                                          w_out, ctx_k, ctx_v, s0_f, s0_b, latent)
    x = x + g_a * mix
    h = rms_norm(x, norm_f) * (1.0 + sc_f) + sh_f
    x = x + g_f * swiglu(h, w_g, w_u, w_d)
    return x, k_a, v_a, s_f, s_b


def setup_inputs(seed: int = 0) -> dict:
    key = jax.random.key(seed)
    ks = jax.random.split(key, 32)
    f32 = jnp.float32

    def nrm(k, shape, scale):
        return jax.random.normal(k, shape, f32) * scale

    base_decay = jnp.log(2.0 ** (5.0 + jnp.arange(N_RET_HEADS, dtype=f32)) - 1.0)
    return {
        'x_prompt': nrm(ks[0], (BATCH, SEQ, D_MODEL), 1.0),
        'x_sample': nrm(ks[1], (DEC_BATCH, DEC_SEQ, D_MODEL), 1.0),
        'cache_attn_k': nrm(ks[2], (DEC_BATCH, DEPTH, PAST_LEN, N_KV_HEADS, HEAD_DIM), 1.0),
        'cache_attn_v': nrm(ks[3], (DEC_BATCH, DEPTH, PAST_LEN, N_KV_HEADS, HEAD_DIM), 1.0),
        'state_ret_fwd': nrm(ks[4], (DEC_BATCH, DEPTH, N_RET_HEADS, RET_KEY_DIM, RET_VAL_DIM), 0.5),
        'state_ret_bwd': nrm(ks[5], (DEC_BATCH, DEPTH, N_RET_HEADS, RET_KEY_DIM, RET_VAL_DIM), 0.5),
        'c': nrm(ks[6], (DEC_BATCH, D_MODEL), 1.0),
        'c_ctx': nrm(ks[7], (D_MODEL,), 1.0),
        'norm_attn': 1.0 + nrm(ks[8], (DEPTH, D_MODEL), 0.01),
        'norm_ffn': 1.0 + nrm(ks[9], (DEPTH, D_MODEL), 0.01),
        'w_mod': nrm(ks[10], (DEPTH, D_MODEL, 6 * D_MODEL), 0.5 * D_MODEL ** -0.5),
        'b_mod': nrm(ks[11], (DEPTH, 6 * D_MODEL), 0.01),
        'w_in': nrm(ks[12], (DEPTH, D_MODEL, N_IN), D_MODEL ** -0.5),
        'q_norm': 1.0 + nrm(ks[13], (DEPTH, HEAD_DIM), 0.01),
        'k_norm': 1.0 + nrm(ks[14], (DEPTH, HEAD_DIM), 0.01),
        'ret_decay_fwd': base_decay[None, :] + nrm(ks[15], (DEPTH, N_RET_HEADS), 0.01),
        'ret_decay_bwd': base_decay[None, :] + nrm(ks[16], (DEPTH, N_RET_HEADS), 0.01),
        'ret_norm': 1.0 + nrm(ks[17], (DEPTH, RET_V), 0.01),
        'w_branch_attn': nrm(ks[18], (DEPTH, ATTN_Q, D_MODEL), ATTN_Q ** -0.5),
        'w_branch_ret': nrm(ks[19], (DEPTH, RET_V, D_MODEL), RET_V ** -0.5),
        'w_out': nrm(ks[20], (DEPTH, D_MODEL, D_MODEL), D_MODEL ** -0.5),
        'w_ffn_gate': nrm(ks[21], (DEPTH, D_MODEL, D_FF), D_MODEL ** -0.5),
        'w_ffn_up': nrm(ks[22], (DEPTH, D_MODEL, D_FF), D_MODEL ** -0.5),
        'w_ffn_down': nrm(ks[23], (DEPTH, D_FF, D_MODEL), D_FF ** -0.5),
        'final_norm': 1.0 + nrm(ks[24], (D_MODEL,), 0.01),
    }


def reference(x_prompt, x_sample, cache_attn_k, cache_attn_v, state_ret_fwd, state_ret_bwd, c, c_ctx,
              norm_attn, norm_ffn, w_mod, b_mod, w_in, q_norm, k_norm, ret_decay_fwd, ret_decay_bwd,
              ret_norm, w_branch_attn, w_branch_ret, w_out, w_ffn_gate, w_ffn_up, w_ffn_down, final_norm):
    xp, xs = x_prompt, x_sample
    new_k, new_v, new_sf, new_sb = [], [], [], []
    for l in range(DEPTH):
        mod_ctx = (jax.nn.silu(c_ctx) @ w_mod[l] + b_mod[l])[None, None, :]
        mod_lat = (jax.nn.silu(c) @ w_mod[l] + b_mod[l])[:, None, :]
        xp, k_l, v_l, sf_l, sb_l = layer(
            xp, mod_ctx, norm_attn[l], norm_ffn[l], w_in[l], q_norm[l], k_norm[l], ret_decay_fwd[l],
            ret_decay_bwd[l], ret_norm[l], w_branch_attn[l], w_branch_ret[l], w_out[l], w_ffn_gate[l],
            w_ffn_up[l], w_ffn_down[l], None, None, None, None, False)
        xs, _, _, _, _ = layer(
            xs, mod_lat, norm_attn[l], norm_ffn[l], w_in[l], q_norm[l], k_norm[l], ret_decay_fwd[l],
            ret_decay_bwd[l], ret_norm[l], w_branch_attn[l], w_branch_ret[l], w_out[l], w_ffn_gate[l],
            w_ffn_up[l], w_ffn_down[l], cache_attn_k[:, l], cache_attn_v[:, l], state_ret_fwd[:, l],
            state_ret_bwd[:, l], True)
        new_k.append(k_l)
        new_v.append(v_l)
        new_sf.append(sf_l)
        new_sb.append(sb_l)
    y_prompt = rms_norm(xp, final_norm)
    y_sample = rms_norm(xs, final_norm)
    return (y_prompt, y_sample, jnp.stack(new_k, axis=1), jnp.stack(new_v, axis=1),
            jnp.stack(new_sf, axis=1), jnp.stack(new_sb, axis=1))
```

```python
import functools

import jax
import jax.numpy as jnp
from jax import lax
from jax.experimental import pallas as pl
from jax.experimental.pallas import tpu as pltpu

D_MODEL = 2048
BATCH = 16
SEQ = 256
DEC_BATCH = 8
DEC_SEQ = 1024
PAST_LEN = 512
GRID_W = 64
N_HEADS = 16
N_KV_HEADS = 4
HEAD_DIM = 128
ROPE_THETA = 10000.0
N_RET_HEADS = 8
RET_KEY_DIM = 128
RET_VAL_DIM = 256
D_FF = 5632
EPS = 1e-6

ATTN_Q = N_HEADS * HEAD_DIM
ATTN_KV = N_KV_HEADS * HEAD_DIM
RET_QK = N_RET_HEADS * RET_KEY_DIM
RET_V = N_RET_HEADS * RET_VAL_DIM
GQA_GROUP = N_HEADS // N_KV_HEADS

COL_Q = 0
COL_K = COL_Q + ATTN_Q
COL_V = COL_K + ATTN_KV
COL_QR = COL_V + ATTN_KV
COL_KR = COL_QR + RET_QK
COL_VR = COL_KR + RET_QK
COL_GR = COL_VR + RET_V
COL_GATES = COL_GR + RET_V

T_P = BATCH * SEQ
T_S = DEC_BATCH * DEC_SEQ
T = T_P + T_S
MOD_ROWS = 16

V7X_VMEM_BYTES = 64 * 1024 * 1024
VMEM_CAP = V7X_VMEM_BYTES - 6 * 1024 * 1024

BF16 = jnp.bfloat16
F32 = jnp.float32


def _nbytes(shape, dtype):
    n = 1
    for s in shape:
        if s is not None:
            n *= s
    return n * jnp.dtype(dtype).itemsize


def _params(vmem_estimate, n_grid):
    limit = min(VMEM_CAP, int(vmem_estimate) + 8 * 1024 * 1024)
    return pltpu.CompilerParams(
        dimension_semantics=("arbitrary",) * n_grid, vmem_limit_bytes=limit)


def _mod_row(m, tm):
    n_p = T_P // tm
    return jnp.where(m < n_p, 0, 1 + (m - n_p) // (DEC_SEQ // tm))


def _mod_spec(tm, width, col_block, grid_rank):
    if grid_rank == 1:
        return pl.BlockSpec((None, 1, width), lambda m: (_mod_row(m, tm), 0, col_block(0)))
    return pl.BlockSpec((None, 1, width), lambda n, m: (_mod_row(m, tm), 0, col_block(n)))


def _mod_body(c_ref, w_ref, b_ref, o_ref):
    a = jax.nn.silu(c_ref[...]).astype(BF16)
    w = w_ref[...].astype(BF16)
    o_ref[...] = jnp.dot(a, w, preferred_element_type=F32) + b_ref[...]


def _mod_call(c_all, w_mod, b_mod):
    tn = 1024
    n_out = 6 * D_MODEL
    est = 2 * _nbytes((D_MODEL, tn), F32) + _nbytes((D_MODEL, tn), BF16)
    return pl.pallas_call(
        _mod_body,
        grid=(n_out // tn,),
        in_specs=[
            pl.BlockSpec((MOD_ROWS, D_MODEL), lambda n: (0, 0)),
            pl.BlockSpec((D_MODEL, tn), lambda n: (0, n)),
            pl.BlockSpec((1, tn), lambda n: (0, n)),
        ],
        out_specs=pl.BlockSpec((MOD_ROWS, tn), lambda n: (0, n)),
        out_shape=jax.ShapeDtypeStruct((MOD_ROWS, n_out), F32),
        compiler_params=_params(est, 1),
        name="mod_table",
    )(c_all, w_mod, b_mod)


def _modulated_norm(x, g, sc, sh):
    y = x * lax.rsqrt(jnp.mean(x * x, axis=-1, keepdims=True) + EPS)
    return (y * g) * (1.0 + sc) + sh


def _prenorm2_body(xp_ref, xs_ref, g_ref, sc_ref, sh_ref, o_ref, *, n_p):
    m = pl.program_id(0)

    @pl.when(m < n_p)
    def _():
        o_ref[...] = _modulated_norm(xp_ref[...], g_ref[...], sc_ref[...], sh_ref[...]).astype(BF16)

    @pl.when(m >= n_p)
    def _():
        o_ref[...] = _modulated_norm(xs_ref[...], g_ref[...], sc_ref[...], sh_ref[...]).astype(BF16)


def _prenorm1_body(x_ref, g_ref, sc_ref, sh_ref, o_ref):
    o_ref[...] = _modulated_norm(x_ref[...], g_ref[...], sc_ref[...], sh_ref[...]).astype(BF16)


def _prenorm_call(xs, norm_w, mod3, sh_blk, sc_blk, name):
    tm = 512
    n_p = T_P // tm
    blk = (tm, D_MODEL)
    common = [
        pl.BlockSpec((1, D_MODEL), lambda m: (0, 0)),
        _mod_spec(tm, D_MODEL, lambda n: sc_blk, 1),
        _mod_spec(tm, D_MODEL, lambda n: sh_blk, 1),
    ]
    if len(xs) == 2:
        in_specs = [
            pl.BlockSpec(blk, lambda m: (jnp.minimum(m, n_p - 1), 0)),
            pl.BlockSpec(blk, lambda m: (jnp.maximum(m - n_p, 0), 0)),
        ] + common
        body = functools.partial(_prenorm2_body, n_p=n_p)
    else:
        in_specs = [pl.BlockSpec(blk, lambda m: (m, 0))] + common
        body = _prenorm1_body
    est = 2 * len(xs) * _nbytes(blk, F32) + 2 * _nbytes(blk, BF16) + 2 * _nbytes(blk, F32)
    return pl.pallas_call(
        body,
        grid=(T // tm,),
        in_specs=in_specs,
        out_specs=pl.BlockSpec(blk, lambda m: (m, 0)),
        out_shape=jax.ShapeDtypeStruct((T, D_MODEL), BF16),
        compiler_params=_params(est, 1),
        name=name,
    )(*xs, norm_w, mod3, mod3)


def _matmul_body(*refs, n_lhs, n_rhs, pairs, n_extra, n_out, epilogue):
    lhs = refs[:n_lhs]
    rhs = refs[n_lhs:n_lhs + n_rhs]
    extras = refs[n_lhs + n_rhs:n_lhs + n_rhs + n_extra]
    outs = refs[n_lhs + n_rhs + n_extra:n_lhs + n_rhs + n_extra + n_out]
    wb = refs[n_lhs + n_rhs + n_extra + n_out:]

    @pl.when(pl.program_id(1) == 0)
    def _():
        for w_ref, wb_ref in zip(rhs, wb):
            wb_ref[...] = w_ref[...].astype(BF16)

    zs = [jnp.dot(lhs[i][...], wb[j][...], preferred_element_type=F32) for i, j in pairs]
    epilogue(zs, extras, outs)


def _matmul_call(lhs_list, rhs_list, pairs, n_cols, tm, tn, epilogue, extras, outs, name):
    k_dim = lhs_list[0].shape[1]
    grid = (n_cols // tn, T // tm)
    in_specs = [pl.BlockSpec((tm, k_dim), lambda n, m: (m, 0)) for _ in lhs_list]
    for _, col0 in rhs_list:
        in_specs.append(pl.BlockSpec((k_dim, tn), lambda n, m, c=col0 // tn: (0, c + n)))
    in_specs += [spec for _, spec in extras]
    est = 2 * len(lhs_list) * _nbytes((tm, k_dim), BF16)
    est += len(rhs_list) * (2 * _nbytes((k_dim, tn), F32) + _nbytes((k_dim, tn), BF16))
    for arr, spec in list(extras) + list(outs):
        est += 2 * _nbytes(spec.block_shape, arr.dtype)
    est += 2 * len(pairs) * _nbytes((tm, tn), F32)
    body = functools.partial(
        _matmul_body, n_lhs=len(lhs_list), n_rhs=len(rhs_list), pairs=tuple(pairs),
        n_extra=len(extras), n_out=len(outs), epilogue=epilogue)
    res = pl.pallas_call(
        body,
        grid=grid,
        in_specs=in_specs,
        out_specs=[spec for _, spec in outs],
        out_shape=[s for s, _ in outs],
        scratch_shapes=[pltpu.VMEM((k_dim, tn), BF16) for _ in rhs_list],
        compiler_params=_params(est, 2),
        name=name,
    )(*lhs_list, *[w for w, _ in rhs_list], *[a for a, _ in extras])
    return res


def _tile_spec(tm, tn, col_block0=0):
    return pl.BlockSpec((tm, tn), lambda n, m, c=col_block0: (m, c + n))


def _prompt_tile_spec(tm, tn):
    n_p = T_P // tm
    return pl.BlockSpec((tm, tn), lambda n, m: (jnp.minimum(m, n_p - 1), n))


def _head_rms(x, g):
    return x * lax.rsqrt(jnp.mean(x * x, axis=-1, keepdims=True) + EPS) * g


def _rope(y, c, s_lo, s_hi):
    return y * c + pltpu.roll(y, HEAD_DIM - 32, axis=1) * s_lo + pltpu.roll(y, 32, axis=1) * s_hi


def _epi_q(zs, extras, outs, *, tm, tn):
    g_ref, c_ref, slo_ref, shi_ref = extras
    (q_ref,) = outs
    (z,) = zs
    n_p = T_P // tm
    m = pl.program_id(1)

    def heads(rope):
        for h in range(tn // HEAD_DIM):
            sl = slice(h * HEAD_DIM, (h + 1) * HEAD_DIM)
            y = _head_rms(z[:, sl], g_ref[...])
            if rope:
                y = _rope(y, c_ref[...], slo_ref[...], shi_ref[...])
            q_ref[:, sl] = y.astype(BF16)

    pl.when(m < n_p)(lambda: heads(False))
    pl.when(m >= n_p)(lambda: heads(True))


def _epi_k(zs, extras, outs, *, tm, tn):
    g_ref, c_ref, slo_ref, shi_ref = extras
    k_att_ref, k_new_ref = outs
    (z,) = zs
    n_p = T_P // tm
    m = pl.program_id(1)

    def heads(rope):
        for h in range(tn // HEAD_DIM):
            sl = slice(h * HEAD_DIM, (h + 1) * HEAD_DIM)
            y = _head_rms(z[:, sl], g_ref[...])
            if rope:
                k_att_ref[:, sl] = _rope(y, c_ref[...], slo_ref[...], shi_ref[...]).astype(BF16)
            else:
                k_new_ref[:, sl] = y
                k_att_ref[:, sl] = y.astype(BF16)

    pl.when(m < n_p)(lambda: heads(False))
    pl.when(m >= n_p)(lambda: heads(True))


def _epi_v(zs, extras, outs, *, tm):
    v_att_ref, v_new_ref = outs
    (z,) = zs
    v_att_ref[...] = z.astype(BF16)

    @pl.when(pl.program_id(1) < T_P // tm)
    def _():
        v_new_ref[...] = z


def _epi_cast(zs, extras, outs):
    outs[0][...] = zs[0].astype(outs[0].dtype)


def _epi_scale(zs, extras, outs, *, scale):
    outs[0][...] = zs[0] * scale


def _epi_silu(zs, extras, outs):
    outs[0][...] = jax.nn.silu(zs[0])


def _epi_sigmoid(zs, extras, outs):
    outs[0][...] = jax.nn.sigmoid(zs[0])


def _epi_merge(zs, extras, outs):
    ga_ref, gr_ref = extras
    outs[0][...] = (ga_ref[...] * zs[0] + gr_ref[...] * zs[1]).astype(BF16)


def _epi_residual2(zs, extras, outs, *, tm):
    xp_ref, xs_ref, gate_ref = extras
    (o_ref,) = outs
    y = gate_ref[...] * zs[0]
    m = pl.program_id(1)
    n_p = T_P // tm

    @pl.when(m < n_p)
    def _():
        o_ref[...] = xp_ref[...] + y

    @pl.when(m >= n_p)
    def _():
        o_ref[...] = xs_ref[...] + y


def _epi_swiglu(zs, extras, outs):
    outs[0][...] = (jax.nn.silu(zs[0]) * zs[1]).astype(BF16)


def _epi_residual(zs, extras, outs):
    x_ref, gate_ref = extras
    outs[0][...] = x_ref[...] + gate_ref[...] * zs[0]


_NT = (((1,), (1,)), ((), ()))


def _attn_body(*refs, has_ctx):
    if has_ctx:
        q_ref, k_ref, v_ref, ck_ref, cv_ref, _, o_ref = refs
        ck = ck_ref[...].astype(BF16)
        cv = cv_ref[...].astype(BF16)
    else:
        q_ref, k_ref, v_ref, o_ref = refs
    k = k_ref[...]
    v = v_ref[...]
    scale = HEAD_DIM ** -0.5
    for h in range(GQA_GROUP):
        sl = slice(h * HEAD_DIM, (h + 1) * HEAD_DIM)
        q = q_ref[:, sl]
        s_new = lax.dot_general(q, k, _NT, preferred_element_type=F32) * scale
        mx = jnp.max(s_new, axis=-1, keepdims=True)
        if has_ctx:
            s_ctx = lax.dot_general(q, ck, _NT, preferred_element_type=F32) * scale
            mx = jnp.maximum(mx, jnp.max(s_ctx, axis=-1, keepdims=True))
            p_ctx = jnp.exp(s_ctx - mx)
            p_new = jnp.exp(s_new - mx)
            inv = 1.0 / (jnp.sum(p_ctx, axis=-1, keepdims=True) + jnp.sum(p_new, axis=-1, keepdims=True))
            o = jnp.dot((p_ctx * inv).astype(BF16), cv, preferred_element_type=F32)
            o = o + jnp.dot((p_new * inv).astype(BF16), v, preferred_element_type=F32)
        else:
            p_new = jnp.exp(s_new - mx)
            inv = 1.0 / jnp.sum(p_new, axis=-1, keepdims=True)
            o = jnp.dot((p_new * inv).astype(BF16), v, preferred_element_type=F32)
        o_ref[:, sl] = o.astype(BF16)


def _attn_call(q_att, k_att, v_att, ctx_k, ctx_v, prompt_out, *, latent):
    gw = GQA_GROUP * HEAD_DIM
    if latent:
        seq, nb, tq = DEC_SEQ, DEC_BATCH, 512
        row0 = T_P
    else:
        seq, nb, tq = SEQ, BATCH, SEQ
        row0 = 0
    nqt = seq // tq
    q_spec = pl.BlockSpec((tq, gw), lambda b, g, t: (row0 // tq + b * nqt + t, g))
    kv_spec = pl.BlockSpec((seq, HEAD_DIM), lambda b, g, t: (row0 // seq + b, g))
    in_specs = [q_spec, kv_spec, kv_spec]
    args = [q_att, k_att, v_att]
    n_keys = seq
    if latent:
        c_spec = pl.BlockSpec((None, PAST_LEN, HEAD_DIM), lambda b, g, t: (b, 0, g))
        in_specs += [c_spec, c_spec, pl.BlockSpec(memory_space=pl.ANY)]
        args += [ctx_k, ctx_v, prompt_out]
        n_keys += PAST_LEN
    est = 4 * _nbytes((tq, gw), BF16) + 4 * _nbytes((seq, HEAD_DIM), BF16)
    est += 4 * _nbytes((PAST_LEN, HEAD_DIM), F32)
    est += 3 * GQA_GROUP * _nbytes((tq, n_keys), F32)
    return pl.pallas_call(
        functools.partial(_attn_body, has_ctx=latent),
        grid=(nb, N_KV_HEADS, nqt),
        in_specs=in_specs,
        out_specs=q_spec,
        out_shape=jax.ShapeDtypeStruct((T, ATTN_Q), BF16),
        input_output_aliases={len(args) - 1: 0} if latent else {},
        compiler_params=_params(est, 3),
        name="attn_latent" if latent else "attn_prompt",
    )(*args)


_TN = (((0,), (0,)), ((), ()))


def _ret_body(*refs, seq, tq, latent):
    if latent:
        (lgf_ref, lgb_ref, q_ref, k_ref, v_ref, sg_ref, rn_ref, s0f_ref, s0b_ref, _,
         o_ref, d_ref) = refs
    else:
        (lgf_ref, lgb_ref, q_ref, k_ref, v_ref, sg_ref, rn_ref,
         o_ref, sf_ref, sb_ref, d_ref) = refs
    h = pl.program_id(0)
    lgf = lgf_ref[h]
    lgb = lgb_ref[h]

    @pl.when(pl.program_id(1) == 0)
    def _():
        for t in range(seq // tq):
            i = lax.broadcasted_iota(jnp.int32, (tq, seq), 0) + t * tq
            j = lax.broadcasted_iota(jnp.int32, (tq, seq), 1)
            diff = (i - j).astype(F32)
            arg = jnp.where(diff >= 0, diff * lgf, -diff * lgb)
            d_ref[t * tq:(t + 1) * tq, :] = jnp.exp(arg) * jnp.where(diff == 0, 2.0, 1.0)

    k_f32 = k_ref[...]
    kb = k_f32.astype(BF16)
    v = v_ref[...]
    if latent:
        s0f = s0f_ref[...].astype(BF16)
        s0b = s0b_ref[...].astype(BF16)
    for t in range(seq // tq):
        rows = slice(t * tq, (t + 1) * tq)
        q = q_ref[rows, :]
        raw = lax.dot_general(q, kb, _NT, preferred_element_type=F32)
        p = (raw * d_ref[rows, :]).astype(BF16)
        o = jnp.dot(p, v, preferred_element_type=F32)
        if latent:
            pos = (lax.broadcasted_iota(jnp.int32, (tq, RET_VAL_DIM), 0) + t * tq).astype(F32)
            xi_f = jnp.exp((pos + 1.0) * lgf)
            xi_b = jnp.exp((seq - pos) * lgb)
            o = o + jnp.dot(q, s0f, preferred_element_type=F32) * xi_f
            o = o + jnp.dot(q, s0b, preferred_element_type=F32) * xi_b
        mu = jnp.mean(o, axis=-1, keepdims=True)
        oc = o - mu
        var = jnp.mean(oc * oc, axis=-1, keepdims=True)
        y = (oc * lax.rsqrt(var + EPS)) * rn_ref[...]
        o_ref[rows, :] = (y * sg_ref[rows, :]).astype(BF16)
    if not latent:
        j = lax.broadcasted_iota(jnp.int32, (seq, RET_KEY_DIM), 0).astype(F32)
        kz_f = (k_f32 * jnp.exp((seq - 1.0 - j) * lgf)).astype(BF16)
        kz_b = (k_f32 * jnp.exp(j * lgb)).astype(BF16)
        sf_ref[...] = lax.dot_general(kz_f, v, _TN, preferred_element_type=F32)
        sb_ref[...] = lax.dot_general(kz_b, v, _TN, preferred_element_type=F32)


def _ret_call(lgf, lgb, q_ret, k_ret, v_ret, sg, ret_norm, s0f, s0b, prompt_out, *, latent):
    if latent:
        seq, nb, row0 = DEC_SEQ, DEC_BATCH, T_P
    else:
        seq, nb, row0 = SEQ, BATCH, 0
    tq = 256
    rb = row0 // seq
    smem = pl.BlockSpec(memory_space=pltpu.SMEM)
    qk_spec = pl.BlockSpec((seq, RET_KEY_DIM), lambda h, b: (rb + b, h))
    v_spec = pl.BlockSpec((seq, RET_VAL_DIM), lambda h, b: (rb + b, h))
    st_spec = pl.BlockSpec((None, None, None, RET_KEY_DIM, RET_VAL_DIM), lambda h, b: (b, 0, h, 0, 0))
    in_specs = [smem, smem, qk_spec, qk_spec, v_spec, v_spec,
                pl.BlockSpec((1, RET_VAL_DIM), lambda h, b: (0, h))]
    args = [lgf, lgb, q_ret, k_ret, v_ret, sg, ret_norm]
    o_shape = jax.ShapeDtypeStruct((T, RET_V), BF16)
    if latent:
        in_specs += [st_spec, st_spec, pl.BlockSpec(memory_space=pl.ANY)]
        args += [s0f, s0b, prompt_out]
        out_specs = v_spec
        out_shape = o_shape
        aliases = {len(args) - 1: 0}
    else:
        aliases = {}
        st_shape = jax.ShapeDtypeStruct((BATCH, 1, N_RET_HEADS, RET_KEY_DIM, RET_VAL_DIM), F32)
        out_specs = [v_spec, st_spec, st_spec]
        out_shape = [o_shape, st_shape, st_shape]
    est = _nbytes((seq, seq), F32) + 4 * _nbytes((tq, seq), F32)
    est += 2 * (_nbytes((seq, RET_KEY_DIM), BF16) + _nbytes((seq, RET_KEY_DIM), F32))
    est += 2 * (2 * _nbytes((seq, RET_VAL_DIM), BF16) + _nbytes((seq, RET_VAL_DIM), F32))
    est += 8 * _nbytes((RET_KEY_DIM, RET_VAL_DIM), F32)
    return pl.pallas_call(
        functools.partial(_ret_body, seq=seq, tq=tq, latent=latent),
        grid=(N_RET_HEADS, nb),
        in_specs=in_specs,
        out_specs=out_specs,
        out_shape=out_shape,
        scratch_shapes=[pltpu.VMEM((seq, seq), F32)],
        input_output_aliases=aliases,
        compiler_params=_params(est, 2),
        name="ret_latent" if latent else "ret_prompt",
    )(*args)


def _final_body(x_ref, g_ref, yp_ref, ys_ref, *, n_p):
    x = x_ref[...]
    y = x * lax.rsqrt(jnp.mean(x * x, axis=-1, keepdims=True) + EPS) * g_ref[...]
    m = pl.program_id(0)

    @pl.when(m < n_p)
    def _():
        yp_ref[...] = y

    @pl.when(m >= n_p)
    def _():
        ys_ref[...] = y


def _final_call(x, g):
    tm = 512
    n_p = T_P // tm
    blk = (tm, D_MODEL)
    est = 6 * _nbytes(blk, F32) + 2 * _nbytes(blk, F32)
    return pl.pallas_call(
        functools.partial(_final_body, n_p=n_p),
        grid=(T // tm,),
        in_specs=[pl.BlockSpec(blk, lambda m: (m, 0)), pl.BlockSpec((1, D_MODEL), lambda m: (0, 0))],
        out_specs=[
            pl.BlockSpec(blk, lambda m: (jnp.minimum(m, n_p - 1), 0)),
            pl.BlockSpec(blk, lambda m: (jnp.maximum(m - n_p, 0), 0)),
        ],
        out_shape=[jax.ShapeDtypeStruct((T_P, D_MODEL), F32), jax.ShapeDtypeStruct((T_S, D_MODEL), F32)],
        compiler_params=_params(est, 1),
        name="final_norm",
    )(x, g)


def _rope_tables():
    rows = DEC_SEQ // GRID_W
    row = jnp.repeat(jnp.arange(rows, dtype=F32), GRID_W)
    col = jnp.tile(jnp.arange(GRID_W, dtype=F32), rows)
    quarter = HEAD_DIM // 4
    inv_freq = ROPE_THETA ** (-jnp.arange(quarter, dtype=F32) / quarter)
    ang_r = row[:, None] * inv_freq[None, :]
    ang_c = col[:, None] * inv_freq[None, :]
    cr, sr, cc, sc = jnp.cos(ang_r), jnp.sin(ang_r), jnp.cos(ang_c), jnp.sin(ang_c)
    zero = jnp.zeros_like(sr)
    cos = jnp.concatenate([cr, cr, cc, cc], axis=-1)
    s_lo = jnp.concatenate([-sr, zero, -sc, zero], axis=-1)
    s_hi = jnp.concatenate([zero, sr, zero, sc], axis=-1)
    return cos, s_lo, s_hi


def kernel(x_prompt, x_sample, cache_attn_k, cache_attn_v, state_ret_fwd, state_ret_bwd, c, c_ctx,
           norm_attn, norm_ffn, w_mod, b_mod, w_in, q_norm, k_norm, ret_decay_fwd, ret_decay_bwd,
           ret_norm, w_branch_attn, w_branch_ret, w_out, w_ffn_gate, w_ffn_up, w_ffn_down, final_norm):
    xp = x_prompt.reshape(T_P, D_MODEL)
    xs = x_sample.reshape(T_S, D_MODEL)
    ctx_k = cache_attn_k.reshape(DEC_BATCH, PAST_LEN, ATTN_KV)
    ctx_v = cache_attn_v.reshape(DEC_BATCH, PAST_LEN, ATTN_KV)
    lgf = jax.nn.log_sigmoid(ret_decay_fwd[0].astype(F32))
    lgb = jax.nn.log_sigmoid(ret_decay_bwd[0].astype(F32))
    cos, s_lo, s_hi = _rope_tables()

    c_all = jnp.concatenate(
        [c_ctx[None, :], c, jnp.zeros((MOD_ROWS - 1 - DEC_BATCH, D_MODEL), F32)], axis=0)
    mod = _mod_call(c_all, w_mod[0], b_mod)
    mod3 = mod.reshape(MOD_ROWS, 1, 6 * D_MODEL)
    h = _prenorm_call([xp, xs], norm_attn, mod3, 0, 1, "prenorm_attn")

    w_in0 = w_in[0]
    tm = 1024
    tbl = pl.BlockSpec((DEC_SEQ, HEAD_DIM), lambda n, m: (0, 0))
    hd = pl.BlockSpec((1, HEAD_DIM), lambda n, m: (0, 0))
    rope_extras = [(cos, tbl), (s_lo, tbl), (s_hi, tbl)]

    def proj(col0, n_cols, tn, epilogue, extras, outs, name):
        return _matmul_call([h], [(w_in0, col0)], [(0, 0)], n_cols, tm, tn, epilogue, extras, outs, name)

    def full(n_cols, dtype):
        return jax.ShapeDtypeStruct((T, n_cols), dtype)

    (q_att,) = proj(COL_Q, ATTN_Q, 1024, functools.partial(_epi_q, tm=tm, tn=1024),
                    [(q_norm, hd)] + rope_extras,
                    [(full(ATTN_Q, BF16), _tile_spec(tm, 1024))], "proj_q")
    prompt_kv = jax.ShapeDtypeStruct((T_P, ATTN_KV), F32)
    k_att, k_new = proj(COL_K, ATTN_KV, ATTN_KV, functools.partial(_epi_k, tm=tm, tn=ATTN_KV),
                        [(k_norm, hd)] + rope_extras,
                        [(full(ATTN_KV, BF16), _tile_spec(tm, ATTN_KV)),
                         (prompt_kv, _prompt_tile_spec(tm, ATTN_KV))], "proj_k")
    v_att, v_new = proj(COL_V, ATTN_KV, ATTN_KV, functools.partial(_epi_v, tm=tm), [],
                        [(full(ATTN_KV, BF16), _tile_spec(tm, ATTN_KV)),
                         (prompt_kv, _prompt_tile_spec(tm, ATTN_KV))], "proj_v")
    (q_ret,) = proj(COL_QR, RET_QK, 1024, _epi_cast, [],
                    [(full(RET_QK, BF16), _tile_spec(tm, 1024))], "proj_qr")
    (k_ret,) = proj(COL_KR, RET_QK, 1024, functools.partial(_epi_scale, scale=RET_KEY_DIM ** -0.5), [],
                    [(full(RET_QK, F32), _tile_spec(tm, 1024))], "proj_kr")
    (v_ret,) = proj(COL_VR, RET_V, 1024, _epi_cast, [],
                    [(full(RET_V, BF16), _tile_spec(tm, 1024))], "proj_vr")
    (sg,) = proj(COL_GR, RET_V, 1024, _epi_silu, [],
                 [(full(RET_V, F32), _tile_spec(tm, 1024))], "proj_gr")
    (gates,) = proj(COL_GATES, 2 * D_MODEL, 1024, _epi_sigmoid, [],
                    [(full(2 * D_MODEL, F32), _tile_spec(tm, 1024))], "proj_gates")

    o_a = _attn_call(q_att, k_att, v_att, None, None, None, latent=False)
    o_a = _attn_call(q_att, k_att, v_att, ctx_k, ctx_v, o_a, latent=True)
    o_r, new_sf, new_sb = _ret_call(lgf, lgb, q_ret, k_ret, v_ret, sg, ret_norm,
                                    None, None, None, latent=False)
    o_r = _ret_call(lgf, lgb, q_ret, k_ret, v_ret, sg, ret_norm,
                    state_ret_fwd, state_ret_bwd, o_r, latent=True)

    tm2, tn2 = 512, 512
    (merged,) = _matmul_call(
        [o_a, o_r], [(w_branch_attn[0], 0), (w_branch_ret[0], 0)], [(0, 0), (1, 1)],
        D_MODEL, tm2, tn2, _epi_merge,
        [(gates, _tile_spec(tm2, tn2)), (gates, _tile_spec(tm2, tn2, D_MODEL // tn2))],
        [(full(D_MODEL, BF16), _tile_spec(tm2, tn2))], "merge")

    tn3 = 1024
    n_p2 = T_P // tm2
    (x1,) = _matmul_call(
        [merged], [(w_out[0], 0)], [(0, 0)], D_MODEL, tm2, tn3,
        functools.partial(_epi_residual2, tm=tm2),
        [(xp, pl.BlockSpec((tm2, tn3), lambda n, m: (jnp.minimum(m, n_p2 - 1), n))),
         (xs, pl.BlockSpec((tm2, tn3), lambda n, m: (jnp.maximum(m - n_p2, 0), n))),
         (mod3, _mod_spec(tm2, tn3, lambda n: 2 * (D_MODEL // tn3) + n, 2))],
        [(full(D_MODEL, F32), _tile_spec(tm2, tn3))], "out_proj")

    h2 = _prenorm_call([x1], norm_ffn, mod3, 3, 4, "prenorm_ffn")

    (act,) = _matmul_call(
        [h2], [(w_ffn_gate[0], 0), (w_ffn_up[0], 0)], [(0, 0), (0, 1)],
        D_FF, 1024, 512, _epi_swiglu, [],
        [(full(D_FF, BF16), _tile_spec(1024, 512))], "ffn_up")

    tn4 = 512
    (y_pre,) = _matmul_call(
        [act], [(w_ffn_down[0], 0)], [(0, 0)], D_MODEL, tm2, tn4, _epi_residual,
        [(x1, _tile_spec(tm2, tn4)),
         (mod3, _mod_spec(tm2, tn4, lambda n: 5 * (D_MODEL // tn4) + n, 2))],
        [(full(D_MODEL, F32), _tile_spec(tm2, tn4))], "ffn_down")

    y_p, y_s = _final_call(y_pre, final_norm[None, :])

    return (y_p.reshape(BATCH, SEQ, D_MODEL),
            y_s.reshape(DEC_BATCH, DEC_SEQ, D_MODEL),
            k_new.reshape(BATCH, 1, SEQ, N_KV_HEADS, HEAD_DIM),
            v_new.reshape(BATCH, 1, SEQ, N_KV_HEADS, HEAD_DIM),
            new_sf, new_sb)
```

```python
import functools

import jax
import jax.numpy as jnp
from jax import lax
from jax.experimental import pallas as pl
from jax.experimental.pallas import tpu as pltpu

D_MODEL = 2048
BATCH = 16
SEQ = 256
DEC_BATCH = 8
DEC_SEQ = 1024
PAST_LEN = 512
GRID_W = 64
N_HEADS = 16
N_KV_HEADS = 4
HEAD_DIM = 128
ROPE_THETA = 10000.0
N_RET_HEADS = 8
RET_KEY_DIM = 128
RET_VAL_DIM = 256
D_FF = 5632
EPS = 1e-6

ATTN_Q = N_HEADS * HEAD_DIM
ATTN_KV = N_KV_HEADS * HEAD_DIM
RET_QK = N_RET_HEADS * RET_KEY_DIM
RET_V = N_RET_HEADS * RET_VAL_DIM
GQA_GROUP = N_HEADS // N_KV_HEADS

COL_Q = 0
COL_K = COL_Q + ATTN_Q
COL_V = COL_K + ATTN_KV
COL_QR = COL_V + ATTN_KV
COL_KR = COL_QR + RET_QK
COL_VR = COL_KR + RET_QK
COL_GR = COL_VR + RET_V
COL_GATES = COL_GR + RET_V

T_P = BATCH * SEQ
T_S = DEC_BATCH * DEC_SEQ
T = T_P + T_S
MOD_ROWS = 16

V7X_VMEM_BYTES = 64 * 1024 * 1024
VMEM_CAP = V7X_VMEM_BYTES - 6 * 1024 * 1024

BF16 = jnp.bfloat16
F32 = jnp.float32


def _nbytes(shape, dtype):
    n = 1
    for s in shape:
        if s is not None:
            n *= s
    return n * jnp.dtype(dtype).itemsize


def _params(vmem_estimate, n_grid):
    limit = min(VMEM_CAP, int(vmem_estimate) + 8 * 1024 * 1024)
    return pltpu.CompilerParams(
        dimension_semantics=("arbitrary",) * n_grid, vmem_limit_bytes=limit)


def _mod_row(m, tm):
    n_p = T_P // tm
    return jnp.where(m < n_p, 0, 1 + (m - n_p) // (DEC_SEQ // tm))


def _mod_spec(tm, width, col_block, grid_rank):
    if grid_rank == 1:
        return pl.BlockSpec((None, 1, width), lambda m: (_mod_row(m, tm), 0, col_block(0)))
    return pl.BlockSpec((None, 1, width), lambda n, m: (_mod_row(m, tm), 0, col_block(n)))


def _mod_body(c_ref, w_ref, b_ref, o_ref):
    a = jax.nn.silu(c_ref[...]).astype(BF16)
    w = w_ref[...].astype(BF16)
    o_ref[...] = jnp.dot(a, w, preferred_element_type=F32) + b_ref[...]


def _mod_call(c_all, w_mod, b_mod):
    tn = 1024
    n_out = 6 * D_MODEL
    est = 2 * _nbytes((D_MODEL, tn), F32) + _nbytes((D_MODEL, tn), BF16)
    return pl.pallas_call(
        _mod_body,
        grid=(n_out // tn,),
        in_specs=[
            pl.BlockSpec((MOD_ROWS, D_MODEL), lambda n: (0, 0)),
            pl.BlockSpec((D_MODEL, tn), lambda n: (0, n)),
            pl.BlockSpec((1, tn), lambda n: (0, n)),
        ],
        out_specs=pl.BlockSpec((MOD_ROWS, tn), lambda n: (0, n)),
        out_shape=jax.ShapeDtypeStruct((MOD_ROWS, n_out), F32),
        compiler_params=_params(est, 1),
        name="mod_table",
    )(c_all, w_mod, b_mod)


def _modulated_norm(x, g, sc, sh):
    y = x * lax.rsqrt(jnp.mean(x * x, axis=-1, keepdims=True) + EPS)
    return (y * g) * (1.0 + sc) + sh


def _prenorm2_body(xp_ref, xs_ref, g_ref, sc_ref, sh_ref, o_ref, *, n_p):
    m = pl.program_id(0)

    @pl.when(m < n_p)
    def _():
        o_ref[...] = _modulated_norm(xp_ref[...], g_ref[...], sc_ref[...], sh_ref[...]).astype(BF16)

    @pl.when(m >= n_p)
    def _():
        o_ref[...] = _modulated_norm(xs_ref[...], g_ref[...], sc_ref[...], sh_ref[...]).astype(BF16)


def _prenorm1_body(x_ref, g_ref, sc_ref, sh_ref, o_ref):
    o_ref[...] = _modulated_norm(x_ref[...], g_ref[...], sc_ref[...], sh_ref[...]).astype(BF16)


def _prenorm_call(xs, norm_w, mod3, sh_blk, sc_blk, name):
    tm = 512
    n_p = T_P // tm
    blk = (tm, D_MODEL)
    common = [
        pl.BlockSpec((1, D_MODEL), lambda m: (0, 0)),
        _mod_spec(tm, D_MODEL, lambda n: sc_blk, 1),
        _mod_spec(tm, D_MODEL, lambda n: sh_blk, 1),
    ]
    if len(xs) == 2:
        in_specs = [
            pl.BlockSpec(blk, lambda m: (jnp.minimum(m, n_p - 1), 0)),
            pl.BlockSpec(blk, lambda m: (jnp.maximum(m - n_p, 0), 0)),
        ] + common
        body = functools.partial(_prenorm2_body, n_p=n_p)
    else:
        in_specs = [pl.BlockSpec(blk, lambda m: (m, 0))] + common
        body = _prenorm1_body
    est = 2 * len(xs) * _nbytes(blk, F32) + 2 * _nbytes(blk, BF16) + 2 * _nbytes(blk, F32)
    return pl.pallas_call(
        body,
        grid=(T // tm,),
        in_specs=in_specs,
        out_specs=pl.BlockSpec(blk, lambda m: (m, 0)),
        out_shape=jax.ShapeDtypeStruct((T, D_MODEL), BF16),
        compiler_params=_params(est, 1),
        name=name,
    )(*xs, norm_w, mod3, mod3)


def _matmul_body(*refs, n_lhs, n_rhs, pairs, n_extra, n_out, epilogue, tm, row_chunk):
    lhs = refs[:n_lhs]
    rhs = refs[n_lhs:n_lhs + n_rhs]
    extras = refs[n_lhs + n_rhs:n_lhs + n_rhs + n_extra]
    outs = refs[n_lhs + n_rhs + n_extra:n_lhs + n_rhs + n_extra + n_out]
    wb = refs[n_lhs + n_rhs + n_extra + n_out:]

    @pl.when(pl.program_id(1) == 0)
    def _():
        for w_ref, wb_ref in zip(rhs, wb):
            wb_ref[...] = w_ref[...].astype(BF16)

    for r in range(tm // row_chunk):
        rows = slice(r * row_chunk, (r + 1) * row_chunk)
        zs = [jnp.dot(lhs[i][rows, :], wb[j][...], preferred_element_type=F32) for i, j in pairs]
        epilogue(zs, extras, outs, rows)


def _matmul_call(lhs_list, rhs_list, pairs, n_cols, tm, tn, epilogue, extras, outs, name,
                 row_chunk=256):
    k_dim = lhs_list[0].shape[1]
    grid = (n_cols // tn, T // tm)
    in_specs = [pl.BlockSpec((tm, k_dim), lambda n, m: (m, 0)) for _ in lhs_list]
    for _, col0 in rhs_list:
        in_specs.append(pl.BlockSpec((k_dim, tn), lambda n, m, c=col0 // tn: (0, c + n)))
    in_specs += [spec for _, spec in extras]
    est = 2 * len(lhs_list) * _nbytes((tm, k_dim), BF16)
    est += len(rhs_list) * (2 * _nbytes((k_dim, tn), F32) + _nbytes((k_dim, tn), BF16))
    for arr, spec in list(extras) + list(outs):
        est += 2 * _nbytes(spec.block_shape, arr.dtype)
    est += 2 * len(pairs) * _nbytes((tm, tn), F32)
    body = functools.partial(
        _matmul_body, n_lhs=len(lhs_list), n_rhs=len(rhs_list), pairs=tuple(pairs),
        n_extra=len(extras), n_out=len(outs), epilogue=epilogue, tm=tm, row_chunk=row_chunk)
    res = pl.pallas_call(
        body,
        grid=grid,
        in_specs=in_specs,
        out_specs=[spec for _, spec in outs],
        out_shape=[s for s, _ in outs],
        scratch_shapes=[pltpu.VMEM((k_dim, tn), BF16) for _ in rhs_list],
        compiler_params=_params(est, 2),
        name=name,
    )(*lhs_list, *[w for w, _ in rhs_list], *[a for a, _ in extras])
    return res


def _tile_spec(tm, tn, col_block0=0):
    return pl.BlockSpec((tm, tn), lambda n, m, c=col_block0: (m, c + n))


def _prompt_tile_spec(tm, tn):
    n_p = T_P // tm
    return pl.BlockSpec((tm, tn), lambda n, m: (jnp.minimum(m, n_p), n))


def _head_rms(x, g):
    return x * lax.rsqrt(jnp.mean(x * x, axis=-1, keepdims=True) + EPS) * g


def _rope(y, c, s_lo, s_hi):
    return y * c + pltpu.roll(y, HEAD_DIM - 32, axis=1) * s_lo + pltpu.roll(y, 32, axis=1) * s_hi


def _epi_q(zs, extras, outs, rows, *, tn):
    g_ref, c_ref, slo_ref, shi_ref = extras
    (q_ref,) = outs
    (z,) = zs
    for h in range(tn // HEAD_DIM):
        sl = slice(h * HEAD_DIM, (h + 1) * HEAD_DIM)
        y = _head_rms(z[:, sl], g_ref[...])
        q_ref[rows, sl] = _rope(y, c_ref[rows, :], slo_ref[rows, :], shi_ref[rows, :]).astype(BF16)


def _epi_k(zs, extras, outs, rows, *, tn):
    g_ref, c_ref, slo_ref, shi_ref = extras
    k_att_ref, k_new_ref = outs
    (z,) = zs
    for h in range(tn // HEAD_DIM):
        sl = slice(h * HEAD_DIM, (h + 1) * HEAD_DIM)
        y = _head_rms(z[:, sl], g_ref[...])
        k_new_ref[rows, sl] = y
        k_att_ref[rows, sl] = _rope(y, c_ref[rows, :], slo_ref[rows, :], shi_ref[rows, :]).astype(BF16)


def _epi_v(zs, extras, outs, rows):
    v_att_ref, v_new_ref = outs
    v_att_ref[rows, :] = zs[0].astype(BF16)
    v_new_ref[rows, :] = zs[0]


def _epi_cast(zs, extras, outs, rows):
    outs[0][rows, :] = zs[0].astype(outs[0].dtype)


def _epi_scale(zs, extras, outs, rows, *, scale):
    outs[0][rows, :] = zs[0] * scale


def _epi_silu(zs, extras, outs, rows):
    outs[0][rows, :] = jax.nn.silu(zs[0])


def _epi_sigmoid(zs, extras, outs, rows):
    outs[0][rows, :] = jax.nn.sigmoid(zs[0])


def _epi_merge(zs, extras, outs, rows):
    ga_ref, gr_ref = extras
    outs[0][rows, :] = (ga_ref[rows, :] * zs[0] + gr_ref[rows, :] * zs[1]).astype(BF16)


def _epi_residual2(zs, extras, outs, rows, *, tm):
    xp_ref, xs_ref, gate_ref = extras
    x = jnp.where(pl.program_id(1) < T_P // tm, xp_ref[rows, :], xs_ref[rows, :])
    outs[0][rows, :] = x + gate_ref[...] * zs[0]


def _epi_swiglu(zs, extras, outs, rows):
    outs[0][rows, :] = (jax.nn.silu(zs[0]) * zs[1]).astype(BF16)


def _epi_residual(zs, extras, outs, rows):
    x_ref, gate_ref = extras
    outs[0][rows, :] = x_ref[rows, :] + gate_ref[...] * zs[0]


_NT = (((1,), (1,)), ((), ()))


_SOFTMAX_EXP2_SCALE = HEAD_DIM ** -0.5 * 1.4426950408889634


def _attn_body(*refs, has_ctx, seq, tq, n_sub):
    if has_ctx:
        q_ref, k_ref, v_ref, ck_ref, cv_ref, _, o_ref = refs
        ck = ck_ref[...].astype(BF16)
        cv = cv_ref[...].astype(BF16)
    else:
        q_ref, k_ref, v_ref, o_ref = refs
    for i in range(n_sub):
        k = k_ref[i * seq:(i + 1) * seq, :]
        v = v_ref[i * seq:(i + 1) * seq, :]
        rows = slice(i * tq, (i + 1) * tq)
        for h in range(GQA_GROUP):
            sl = slice(h * HEAD_DIM, (h + 1) * HEAD_DIM)
            q = q_ref[rows, sl]
            s_new = lax.dot_general(q, k, _NT, preferred_element_type=F32)
            mx = jnp.max(s_new, axis=-1, keepdims=True)
            if has_ctx:
                s_ctx = lax.dot_general(q, ck, _NT, preferred_element_type=F32)
                mx = jnp.maximum(mx, jnp.max(s_ctx, axis=-1, keepdims=True))
                p_ctx = jnp.exp2((s_ctx - mx) * _SOFTMAX_EXP2_SCALE)
                p_new = jnp.exp2((s_new - mx) * _SOFTMAX_EXP2_SCALE)
                den = jnp.sum(p_ctx, axis=-1, keepdims=True) + jnp.sum(p_new, axis=-1, keepdims=True)
                o = jnp.dot(p_ctx.astype(BF16), cv, preferred_element_type=F32)
                o = o + jnp.dot(p_new.astype(BF16), v, preferred_element_type=F32)
            else:
                p_new = jnp.exp2((s_new - mx) * _SOFTMAX_EXP2_SCALE)
                den = jnp.sum(p_new, axis=-1, keepdims=True)
                o = jnp.dot(p_new.astype(BF16), v, preferred_element_type=F32)
            o_ref[rows, sl] = (o * (1.0 / den)).astype(BF16)


def _attn_call(q_att, k_att, v_att, ctx_k, ctx_v, prompt_out, *, latent):
    gw = GQA_GROUP * HEAD_DIM
    if latent:
        seq, nb, tq, n_sub = DEC_SEQ, DEC_BATCH, 512, 1
        row0 = T_P
    else:
        seq, nb, tq, n_sub = SEQ, BATCH, SEQ, 4
        row0 = 0
    nqt = seq // tq
    q_rows, kv_rows = n_sub * tq, n_sub * seq
    q_spec = pl.BlockSpec((q_rows, gw), lambda b, g, t: (row0 // q_rows + b * nqt + t, g))
    kv_spec = pl.BlockSpec((kv_rows, HEAD_DIM), lambda b, g, t: (row0 // kv_rows + b, g))
    in_specs = [q_spec, kv_spec, kv_spec]
    args = [q_att, k_att, v_att]
    n_keys = seq
    if latent:
        c_spec = pl.BlockSpec((None, PAST_LEN, HEAD_DIM), lambda b, g, t: (b, 0, g))
        in_specs += [c_spec, c_spec, pl.BlockSpec(memory_space=pl.ANY)]
        args += [ctx_k, ctx_v, prompt_out]
        n_keys += PAST_LEN
    est = 4 * _nbytes((q_rows, gw), BF16) + 4 * _nbytes((kv_rows, HEAD_DIM), BF16)
    est += 4 * _nbytes((PAST_LEN, HEAD_DIM), F32)
    est += 3 * GQA_GROUP * n_sub * _nbytes((tq, n_keys), F32)
    return pl.pallas_call(
        functools.partial(_attn_body, has_ctx=latent, seq=seq, tq=tq, n_sub=n_sub),
        grid=(nb // n_sub, N_KV_HEADS, nqt),
        in_specs=in_specs,
        out_specs=q_spec,
        out_shape=jax.ShapeDtypeStruct((T, ATTN_Q), BF16),
        input_output_aliases={len(args) - 1: 0} if latent else {},
        compiler_params=_params(est, 3),
        name="attn_latent" if latent else "attn_prompt",
    )(*args)


_TN = (((0,), (0,)), ((), ()))


def _ret_body(*refs, seq, tq, latent, n_sub):
    if latent:
        (lgf_ref, lgb_ref, q_ref, k_ref, v_ref, sg_ref, rn_ref, s0f_ref, s0b_ref, _,
         o_ref, d_ref) = refs
    else:
        (lgf_ref, lgb_ref, q_ref, k_ref, v_ref, sg_ref, rn_ref,
         o_ref, sf_ref, sb_ref, d_ref) = refs
    h = pl.program_id(0)
    lgf = lgf_ref[h]
    lgb = lgb_ref[h]

    @pl.when(pl.program_id(1) == 0)
    def _():
        for t in range(seq // tq):
            i = lax.broadcasted_iota(jnp.int32, (tq, seq), 0) + t * tq
            j = lax.broadcasted_iota(jnp.int32, (tq, seq), 1)
            diff = (i - j).astype(F32)
            arg = jnp.where(diff >= 0, diff * lgf, -diff * lgb)
            d_ref[t * tq:(t + 1) * tq, :] = jnp.exp(arg) * jnp.where(diff == 0, 2.0, 1.0)

    if latent:
        s0f = s0f_ref[...].astype(BF16)
        s0b = s0b_ref[...].astype(BF16)
    for i in range(n_sub):
        k_f32 = k_ref[i * seq:(i + 1) * seq, :]
        kb = k_f32.astype(BF16)
        v = v_ref[i * seq:(i + 1) * seq, :]
        for t in range(seq // tq):
            rows = slice(i * seq + t * tq, i * seq + (t + 1) * tq)
            q = q_ref[rows, :]
            raw = lax.dot_general(q, kb, _NT, preferred_element_type=F32)
            p = (raw * d_ref[t * tq:(t + 1) * tq, :]).astype(BF16)
            o = jnp.dot(p, v, preferred_element_type=F32)
            if latent:
                pos = (lax.broadcasted_iota(jnp.int32, (tq, RET_VAL_DIM), 0) + t * tq).astype(F32)
                xi_f = jnp.exp((pos + 1.0) * lgf)
                xi_b = jnp.exp((seq - pos) * lgb)
                o = o + jnp.dot(q, s0f, preferred_element_type=F32) * xi_f
                o = o + jnp.dot(q, s0b, preferred_element_type=F32) * xi_b
            mu = jnp.mean(o, axis=-1, keepdims=True)
            oc = o - mu
            var = jnp.mean(oc * oc, axis=-1, keepdims=True)
            y = (oc * lax.rsqrt(var + EPS)) * rn_ref[...]
            o_ref[rows, :] = (y * sg_ref[rows, :]).astype(BF16)
        if not latent:
            j = lax.broadcasted_iota(jnp.int32, (seq, RET_KEY_DIM), 0).astype(F32)
            kz_f = (k_f32 * jnp.exp((seq - 1.0 - j) * lgf)).astype(BF16)
            kz_b = (k_f32 * jnp.exp(j * lgb)).astype(BF16)
            sf_ref[i] = lax.dot_general(kz_f, v, _TN, preferred_element_type=F32)
            sb_ref[i] = lax.dot_general(kz_b, v, _TN, preferred_element_type=F32)


def _ret_call(lgf, lgb, q_ret, k_ret, v_ret, sg, ret_norm, s0f, s0b, prompt_out, *, latent):
    if latent:
        seq, nb, row0, n_sub = DEC_SEQ, DEC_BATCH, T_P, 1
    else:
        seq, nb, row0, n_sub = SEQ, BATCH, 0, 4
    tq = 256
    rows = n_sub * seq
    rb = row0 // rows
    smem = pl.BlockSpec(memory_space=pltpu.SMEM)
    qk_spec = pl.BlockSpec((rows, RET_KEY_DIM), lambda h, b: (rb + b, h))
    v_spec = pl.BlockSpec((rows, RET_VAL_DIM), lambda h, b: (rb + b, h))
    in_specs = [smem, smem, qk_spec, qk_spec, v_spec, v_spec,
                pl.BlockSpec((1, RET_VAL_DIM), lambda h, b: (0, h))]
    args = [lgf, lgb, q_ret, k_ret, v_ret, sg, ret_norm]
    o_shape = jax.ShapeDtypeStruct((T, RET_V), BF16)
    if latent:
        st_spec = pl.BlockSpec((None, None, None, RET_KEY_DIM, RET_VAL_DIM),
                               lambda h, b: (b, 0, h, 0, 0))
        in_specs += [st_spec, st_spec, pl.BlockSpec(memory_space=pl.ANY)]
        args += [s0f, s0b, prompt_out]
        out_specs = v_spec
        out_shape = o_shape
        aliases = {len(args) - 1: 0}
    else:
        aliases = {}
        st_spec = pl.BlockSpec((n_sub, None, None, RET_KEY_DIM, RET_VAL_DIM),
                               lambda h, b: (b, 0, h, 0, 0))
        st_shape = jax.ShapeDtypeStruct((BATCH, 1, N_RET_HEADS, RET_KEY_DIM, RET_VAL_DIM), F32)
        out_specs = [v_spec, st_spec, st_spec]
        out_shape = [o_shape, st_shape, st_shape]
    est = _nbytes((seq, seq), F32) + 4 * n_sub * _nbytes((tq, seq), F32)
    est += 2 * (_nbytes((rows, RET_KEY_DIM), BF16) + _nbytes((rows, RET_KEY_DIM), F32))
    est += 2 * (2 * _nbytes((rows, RET_VAL_DIM), BF16) + _nbytes((rows, RET_VAL_DIM), F32))
    est += 8 * n_sub * _nbytes((RET_KEY_DIM, RET_VAL_DIM), F32)
    return pl.pallas_call(
        functools.partial(_ret_body, seq=seq, tq=tq, latent=latent, n_sub=n_sub),
        grid=(N_RET_HEADS, nb // n_sub),
        in_specs=in_specs,
        out_specs=out_specs,
        out_shape=out_shape,
        scratch_shapes=[pltpu.VMEM((seq, seq), F32)],
        input_output_aliases=aliases,
        compiler_params=_params(est, 2),
        name="ret_latent" if latent else "ret_prompt",
    )(*args)


def _final_body(x_ref, g_ref, yp_ref, ys_ref, *, n_p):
    x = x_ref[...]
    y = x * lax.rsqrt(jnp.mean(x * x, axis=-1, keepdims=True) + EPS) * g_ref[...]
    m = pl.program_id(0)

    @pl.when(m < n_p)
    def _():
        yp_ref[...] = y

    @pl.when(m >= n_p)
    def _():
        ys_ref[...] = y


def _final_call(x, g):
    tm = 512
    n_p = T_P // tm
    blk = (tm, D_MODEL)
    est = 6 * _nbytes(blk, F32) + 2 * _nbytes(blk, F32)
    return pl.pallas_call(
        functools.partial(_final_body, n_p=n_p),
        grid=(T // tm,),
        in_specs=[pl.BlockSpec(blk, lambda m: (m, 0)), pl.BlockSpec((1, D_MODEL), lambda m: (0, 0))],
        out_specs=[
            pl.BlockSpec(blk, lambda m: (jnp.minimum(m, n_p - 1), 0)),
            pl.BlockSpec(blk, lambda m: (jnp.maximum(m - n_p, 0), 0)),
        ],
        out_shape=[jax.ShapeDtypeStruct((T_P, D_MODEL), F32), jax.ShapeDtypeStruct((T_S, D_MODEL), F32)],
        compiler_params=_params(est, 1),
        name="final_norm",
    )(x, g)


def _rope_tables():
    rows = DEC_SEQ // GRID_W
    row = jnp.repeat(jnp.arange(rows, dtype=F32), GRID_W)
    col = jnp.tile(jnp.arange(GRID_W, dtype=F32), rows)
    quarter = HEAD_DIM // 4
    inv_freq = ROPE_THETA ** (-jnp.arange(quarter, dtype=F32) / quarter)
    ang_r = row[:, None] * inv_freq[None, :]
    ang_c = col[:, None] * inv_freq[None, :]
    cr, sr, cc, sc = jnp.cos(ang_r), jnp.sin(ang_r), jnp.cos(ang_c), jnp.sin(ang_c)
    zero = jnp.zeros_like(sr)
    cos = jnp.concatenate([cr, cr, cc, cc], axis=-1)
    s_lo = jnp.concatenate([-sr, zero, -sc, zero], axis=-1)
    s_hi = jnp.concatenate([zero, sr, zero, sc], axis=-1)
    ident = jnp.zeros((DEC_SEQ, HEAD_DIM), F32)
    return (jnp.concatenate([ident + 1.0, cos], axis=0),
            jnp.concatenate([ident, s_lo], axis=0),
            jnp.concatenate([ident, s_hi], axis=0))


def kernel(x_prompt, x_sample, cache_attn_k, cache_attn_v, state_ret_fwd, state_ret_bwd, c, c_ctx,
           norm_attn, norm_ffn, w_mod, b_mod, w_in, q_norm, k_norm, ret_decay_fwd, ret_decay_bwd,
           ret_norm, w_branch_attn, w_branch_ret, w_out, w_ffn_gate, w_ffn_up, w_ffn_down, final_norm):
    xp = x_prompt.reshape(T_P, D_MODEL)
    xs = x_sample.reshape(T_S, D_MODEL)
    ctx_k = cache_attn_k.reshape(DEC_BATCH, PAST_LEN, ATTN_KV)
    ctx_v = cache_attn_v.reshape(DEC_BATCH, PAST_LEN, ATTN_KV)
    lgf = jax.nn.log_sigmoid(ret_decay_fwd[0].astype(F32))
    lgb = jax.nn.log_sigmoid(ret_decay_bwd[0].astype(F32))
    cos, s_lo, s_hi = _rope_tables()

    c_all = jnp.concatenate(
        [c_ctx[None, :], c, jnp.zeros((MOD_ROWS - 1 - DEC_BATCH, D_MODEL), F32)], axis=0)
    mod = _mod_call(c_all, w_mod[0], b_mod)
    mod3 = mod.reshape(MOD_ROWS, 1, 6 * D_MODEL)
    h = _prenorm_call([xp, xs], norm_attn, mod3, 0, 1, "prenorm_attn")

    w_in0 = w_in[0]
    tm = 1024
    tbl = pl.BlockSpec((DEC_SEQ, HEAD_DIM), lambda n, m: (jnp.where(m < T_P // tm, 0, 1), 0))
    hd = pl.BlockSpec((1, HEAD_DIM), lambda n, m: (0, 0))
    rope_extras = [(cos, tbl), (s_lo, tbl), (s_hi, tbl)]

    def proj(col0, n_cols, tn, epilogue, extras, outs, name):
        return _matmul_call([h], [(w_in0, col0)], [(0, 0)], n_cols, tm, tn, epilogue, extras, outs, name)

    def full(n_cols, dtype):
        return jax.ShapeDtypeStruct((T, n_cols), dtype)

    (q_att,) = proj(COL_Q, ATTN_Q, 1024, functools.partial(_epi_q, tn=1024),
                    [(q_norm, hd)] + rope_extras,
                    [(full(ATTN_Q, BF16), _tile_spec(tm, 1024))], "proj_q")
    prompt_kv = jax.ShapeDtypeStruct((T_P + tm, ATTN_KV), F32)
    k_att, k_new = proj(COL_K, ATTN_KV, ATTN_KV, functools.partial(_epi_k, tn=ATTN_KV),
                        [(k_norm, hd)] + rope_extras,
                        [(full(ATTN_KV, BF16), _tile_spec(tm, ATTN_KV)),
                         (prompt_kv, _prompt_tile_spec(tm, ATTN_KV))], "proj_k")
    v_att, v_new = proj(COL_V, ATTN_KV, ATTN_KV, _epi_v, [],
                        [(full(ATTN_KV, BF16), _tile_spec(tm, ATTN_KV)),
                         (prompt_kv, _prompt_tile_spec(tm, ATTN_KV))], "proj_v")
    (q_ret,) = proj(COL_QR, RET_QK, 1024, _epi_cast, [],
                    [(full(RET_QK, BF16), _tile_spec(tm, 1024))], "proj_qr")
    (k_ret,) = proj(COL_KR, RET_QK, 1024, functools.partial(_epi_scale, scale=RET_KEY_DIM ** -0.5), [],
                    [(full(RET_QK, F32), _tile_spec(tm, 1024))], "proj_kr")
    (v_ret,) = proj(COL_VR, RET_V, 1024, _epi_cast, [],
                    [(full(RET_V, BF16), _tile_spec(tm, 1024))], "proj_vr")
    (sg,) = proj(COL_GR, RET_V, 1024, _epi_silu, [],
                 [(full(RET_V, F32), _tile_spec(tm, 1024))], "proj_gr")
    (gates,) = proj(COL_GATES, 2 * D_MODEL, 1024, _epi_sigmoid, [],
                    [(full(2 * D_MODEL, F32), _tile_spec(tm, 1024))], "proj_gates")

    o_a = _attn_call(q_att, k_att, v_att, None, None, None, latent=False)
    o_a = _attn_call(q_att, k_att, v_att, ctx_k, ctx_v, o_a, latent=True)
    o_r, new_sf, new_sb = _ret_call(lgf, lgb, q_ret, k_ret, v_ret, sg, ret_norm,
                                    None, None, None, latent=False)
    o_r = _ret_call(lgf, lgb, q_ret, k_ret, v_ret, sg, ret_norm,
                    state_ret_fwd, state_ret_bwd, o_r, latent=True)

    tm2, tn2 = 512, 512
    (merged,) = _matmul_call(
        [o_a, o_r], [(w_branch_attn[0], 0), (w_branch_ret[0], 0)], [(0, 0), (1, 1)],
        D_MODEL, tm2, tn2, _epi_merge,
        [(gates, _tile_spec(tm2, tn2)), (gates, _tile_spec(tm2, tn2, D_MODEL // tn2))],
        [(full(D_MODEL, BF16), _tile_spec(tm2, tn2))], "merge")

    tn3 = 1024
    n_p2 = T_P // tm2
    (x1,) = _matmul_call(
        [merged], [(w_out[0], 0)], [(0, 0)], D_MODEL, tm2, tn3,
        functools.partial(_epi_residual2, tm=tm2),
        [(xp, pl.BlockSpec((tm2, tn3), lambda n, m: (jnp.minimum(m, n_p2 - 1), n))),
         (xs, pl.BlockSpec((tm2, tn3), lambda n, m: (jnp.maximum(m - n_p2, 0), n))),
         (mod3, _mod_spec(tm2, tn3, lambda n: 2 * (D_MODEL // tn3) + n, 2))],
        [(full(D_MODEL, F32), _tile_spec(tm2, tn3))], "out_proj")

    h2 = _prenorm_call([x1], norm_ffn, mod3, 3, 4, "prenorm_ffn")

    (act,) = _matmul_call(
        [h2], [(w_ffn_gate[0], 0), (w_ffn_up[0], 0)], [(0, 0), (0, 1)],
        D_FF, 1024, 512, _epi_swiglu, [],
        [(full(D_FF, BF16), _tile_spec(1024, 512))], "ffn_up")

    tn4 = 512
    (y_pre,) = _matmul_call(
        [act], [(w_ffn_down[0], 0)], [(0, 0)], D_MODEL, tm2, tn4, _epi_residual,
        [(x1, _tile_spec(tm2, tn4)),
         (mod3, _mod_spec(tm2, tn4, lambda n: 5 * (D_MODEL // tn4) + n, 2))],
        [(full(D_MODEL, F32), _tile_spec(tm2, tn4))], "ffn_down")

    y_p, y_s = _final_call(y_pre, final_norm[None, :])

    return (y_p.reshape(BATCH, SEQ, D_MODEL),
            y_s.reshape(DEC_BATCH, DEC_SEQ, D_MODEL),
            k_new[:T_P].reshape(BATCH, 1, SEQ, N_KV_HEADS, HEAD_DIM),
            v_new[:T_P].reshape(BATCH, 1, SEQ, N_KV_HEADS, HEAD_DIM),
            new_sf, new_sb)
```

```python
import functools

import jax
import jax.numpy as jnp
from jax import lax
from jax.experimental import pallas as pl
from jax.experimental.pallas import tpu as pltpu

D_MODEL = 2048
BATCH = 16
SEQ = 256
DEC_BATCH = 8
DEC_SEQ = 1024
PAST_LEN = 512
GRID_W = 64
N_HEADS = 16
N_KV_HEADS = 4
HEAD_DIM = 128
ROPE_THETA = 10000.0
N_RET_HEADS = 8
RET_KEY_DIM = 128
RET_VAL_DIM = 256
D_FF = 5632
EPS = 1e-6

ATTN_Q = N_HEADS * HEAD_DIM
ATTN_KV = N_KV_HEADS * HEAD_DIM
RET_QK = N_RET_HEADS * RET_KEY_DIM
RET_V = N_RET_HEADS * RET_VAL_DIM
GQA_GROUP = N_HEADS // N_KV_HEADS

COL_Q = 0
COL_K = COL_Q + ATTN_Q
COL_V = COL_K + ATTN_KV
COL_QR = COL_V + ATTN_KV
COL_KR = COL_QR + RET_QK
COL_VR = COL_KR + RET_QK
COL_GR = COL_VR + RET_V
COL_GATES = COL_GR + RET_V

T_P = BATCH * SEQ
T_S = DEC_BATCH * DEC_SEQ
T = T_P + T_S
MOD_ROWS = 16

V7X_VMEM_BYTES = 64 * 1024 * 1024
VMEM_CAP = V7X_VMEM_BYTES - 6 * 1024 * 1024

BF16 = jnp.bfloat16
F32 = jnp.float32


def _nbytes(shape, dtype):
    n = 1
    for s in shape:
        if s is not None:
            n *= s
    return n * jnp.dtype(dtype).itemsize


def _params(vmem_estimate, n_grid):
    limit = min(VMEM_CAP, int(vmem_estimate) + 8 * 1024 * 1024)
    return pltpu.CompilerParams(
        dimension_semantics=("arbitrary",) * n_grid, vmem_limit_bytes=limit)


def _mod_row(m, tm):
    n_p = T_P // tm
    return jnp.where(m < n_p, 0, 1 + (m - n_p) // (DEC_SEQ // tm))


def _mod_spec(tm, width, col_block, grid_rank):
    if grid_rank == 1:
        return pl.BlockSpec((None, 1, width), lambda m: (_mod_row(m, tm), 0, col_block(0)))
    return pl.BlockSpec((None, 1, width), lambda n, m: (_mod_row(m, tm), 0, col_block(n)))


def _mod_body(c_ref, w_ref, b_ref, o_ref):
    a = jax.nn.silu(c_ref[...]).astype(BF16)
    w = w_ref[...].astype(BF16)
    o_ref[...] = jnp.dot(a, w, preferred_element_type=F32) + b_ref[...]


def _mod_call(c_all, w_mod, b_mod):
    tn = 1024
    n_out = 6 * D_MODEL
    est = 2 * _nbytes((D_MODEL, tn), F32) + _nbytes((D_MODEL, tn), BF16)
    return pl.pallas_call(
        _mod_body,
        grid=(n_out // tn,),
        in_specs=[
            pl.BlockSpec((MOD_ROWS, D_MODEL), lambda n: (0, 0)),
            pl.BlockSpec((D_MODEL, tn), lambda n: (0, n)),
            pl.BlockSpec((1, tn), lambda n: (0, n)),
        ],
        out_specs=pl.BlockSpec((MOD_ROWS, tn), lambda n: (0, n)),
        out_shape=jax.ShapeDtypeStruct((MOD_ROWS, n_out), F32),
        compiler_params=_params(est, 1),
        name="mod_table",
    )(c_all, w_mod, b_mod)


def _modulated_norm(x, g, sc, sh):
    y = x * lax.rsqrt(jnp.mean(x * x, axis=-1, keepdims=True) + EPS)
    return (y * g) * (1.0 + sc) + sh


def _prenorm2_body(xp_ref, xs_ref, g_ref, sc_ref, sh_ref, o_ref, *, n_p):
    m = pl.program_id(0)

    @pl.when(m < n_p)
    def _():
        o_ref[...] = _modulated_norm(xp_ref[...], g_ref[...], sc_ref[...], sh_ref[...]).astype(BF16)

    @pl.when(m >= n_p)
    def _():
        o_ref[...] = _modulated_norm(xs_ref[...], g_ref[...], sc_ref[...], sh_ref[...]).astype(BF16)


def _prenorm1_body(x_ref, g_ref, sc_ref, sh_ref, o_ref):
    o_ref[...] = _modulated_norm(x_ref[...], g_ref[...], sc_ref[...], sh_ref[...]).astype(BF16)


def _prenorm_call(xs, norm_w, mod3, sh_blk, sc_blk, name):
    tm = 512
    n_p = T_P // tm
    blk = (tm, D_MODEL)
    common = [
        pl.BlockSpec((1, D_MODEL), lambda m: (0, 0)),
        _mod_spec(tm, D_MODEL, lambda n: sc_blk, 1),
        _mod_spec(tm, D_MODEL, lambda n: sh_blk, 1),
    ]
    if len(xs) == 2:
        in_specs = [
            pl.BlockSpec(blk, lambda m: (jnp.minimum(m, n_p - 1), 0)),
            pl.BlockSpec(blk, lambda m: (jnp.maximum(m - n_p, 0), 0)),
        ] + common
        body = functools.partial(_prenorm2_body, n_p=n_p)
    else:
        in_specs = [pl.BlockSpec(blk, lambda m: (m, 0))] + common
        body = _prenorm1_body
    est = 2 * len(xs) * _nbytes(blk, F32) + 2 * _nbytes(blk, BF16) + 2 * _nbytes(blk, F32)
    return pl.pallas_call(
        body,
        grid=(T // tm,),
        in_specs=in_specs,
        out_specs=pl.BlockSpec(blk, lambda m: (m, 0)),
        out_shape=jax.ShapeDtypeStruct((T, D_MODEL), BF16),
        compiler_params=_params(est, 1),
        name=name,
    )(*xs, norm_w, mod3, mod3)


def _matmul_body(*refs, lhs_split, n_rhs, col0s, pairs, n_extra, n_out, epilogue,
                 tm, tn, kc, row_chunk):
    n_lhs_refs = sum(2 if s else 1 for s in lhs_split)
    lhs_refs = refs[:n_lhs_refs]
    pos = n_lhs_refs
    rhs = refs[pos:pos + n_rhs]
    pos += n_rhs
    extras = refs[pos:pos + n_extra]
    pos += n_extra
    outs = refs[pos:pos + n_out]
    pos += n_out
    wb = refs[pos:pos + n_rhs]
    stage = refs[pos + n_rhs:pos + 2 * n_rhs]
    sems = refs[pos + 2 * n_rhs:pos + 3 * n_rhs]

    n, m = pl.program_id(0), pl.program_id(1)
    n_tiles = pl.num_programs(0)
    k_dim = wb[0].shape[1]
    n_chunks = k_dim // kc
    cur = n % 2

    def chunk_copy(j, tile, chunk, slot):
        col = pl.multiple_of(col0s[j] + tile * tn, tn)
        row = pl.multiple_of(chunk * kc, kc)
        return pltpu.make_async_copy(
            rhs[j].at[pl.ds(row, kc), pl.ds(col, tn)], stage[j].at[slot], sems[j].at[slot])

    @pl.when(jnp.logical_and(n == 0, m == 0))
    def _():
        for j in range(n_rhs):
            chunk_copy(j, 0, 0, 0).start()
            for c in range(n_chunks):
                if c + 1 < n_chunks:
                    chunk_copy(j, 0, c + 1, (c + 1) % 2).start()
                chunk_copy(j, 0, c, c % 2).wait()
                wb[j][0, c * kc:(c + 1) * kc, :] = stage[j][c % 2].astype(BF16)

    nxt = jnp.minimum(n + 1, n_tiles - 1)
    chunk = jnp.minimum(m, n_chunks - 1)
    for j in range(n_rhs):
        chunk_copy(j, nxt, chunk, 0).start()

    is_prompt = m < T_P // tm

    def lhs_rows(i, rows):
        first = sum(2 if s else 1 for s in lhs_split[:i])
        if lhs_split[i]:
            return jnp.where(is_prompt, lhs_refs[first][rows, :], lhs_refs[first + 1][rows, :])
        return lhs_refs[first][rows, :]

    n_rc = tm // row_chunk
    for r in range(n_rc):
        rows = slice(r * row_chunk, (r + 1) * row_chunk)
        zs = [jnp.dot(lhs_rows(i, rows), wb[j][cur], preferred_element_type=F32) for i, j in pairs]
        epilogue(zs, extras, outs, rows)
        if r == n_rc - 1:
            for j in range(n_rhs):
                chunk_copy(j, nxt, chunk, 0).wait()
                wb[j][1 - cur, pl.ds(pl.multiple_of(chunk * kc, kc), kc), :] = stage[j][0].astype(BF16)


def _matmul_call(lhs_list, rhs_list, pairs, n_cols, tm, tn, epilogue, extras, outs, name,
                 row_chunk=256, kc=256):
    lhs_split = tuple(isinstance(a, tuple) for a in lhs_list)
    k_dim = (lhs_list[0][0] if lhs_split[0] else lhs_list[0]).shape[1]
    n_p = T_P // tm
    grid = (n_cols // tn, T // tm)
    assert k_dim % kc == 0 and k_dim // kc <= grid[1], (k_dim, kc, grid)
    in_specs, lhs_args = [], []
    for a in lhs_list:
        if isinstance(a, tuple):
            in_specs.append(pl.BlockSpec((tm, k_dim), lambda n, m: (jnp.minimum(m, n_p - 1), 0)))
            in_specs.append(pl.BlockSpec((tm, k_dim), lambda n, m: (jnp.maximum(m - n_p, 0), 0)))
            lhs_args += list(a)
        else:
            in_specs.append(pl.BlockSpec((tm, k_dim), lambda n, m: (m, 0)))
            lhs_args.append(a)
    in_specs += [pl.BlockSpec(memory_space=pl.ANY) for _ in rhs_list]
    in_specs += [spec for _, spec in extras]
    est = 2 * len(lhs_args) * _nbytes((tm, k_dim), BF16)
    est += len(rhs_list) * 2 * (_nbytes((k_dim, tn), BF16) + _nbytes((kc, tn), F32))
    for arr, spec in list(extras) + list(outs):
        est += 2 * _nbytes(spec.block_shape, arr.dtype)
    est += 3 * len(pairs) * _nbytes((row_chunk, tn), F32)
    body = functools.partial(
        _matmul_body, lhs_split=lhs_split, n_rhs=len(rhs_list),
        col0s=tuple(c for _, c in rhs_list), pairs=tuple(pairs),
        n_extra=len(extras), n_out=len(outs), epilogue=epilogue,
        tm=tm, tn=tn, kc=kc, row_chunk=row_chunk)
    scratch = [pltpu.VMEM((2, k_dim, tn), BF16) for _ in rhs_list]
    scratch += [pltpu.VMEM((2, kc, tn), F32) for _ in rhs_list]
    scratch += [pltpu.SemaphoreType.DMA((2,)) for _ in rhs_list]
    res = pl.pallas_call(
        body,
        grid=grid,
        in_specs=in_specs,
        out_specs=[spec for _, spec in outs],
        out_shape=[s for s, _ in outs],
        scratch_shapes=scratch,
        compiler_params=_params(est, 2),
        name=name,
    )(*lhs_args, *[w for w, _ in rhs_list], *[a for a, _ in extras])
    return res


def _tile_spec(tm, tn, col_block0=0):
    return pl.BlockSpec((tm, tn), lambda n, m, c=col_block0: (m, c + n))


def _prompt_tile_spec(tm, tn):
    n_p = T_P // tm
    return pl.BlockSpec((tm, tn), lambda n, m: (jnp.minimum(m, n_p), n))


def _head_rms(x, g):
    return x * lax.rsqrt(jnp.mean(x * x, axis=-1, keepdims=True) + EPS) * g


def _rope(y, c, s_lo, s_hi):
    return y * c + pltpu.roll(y, HEAD_DIM - 32, axis=1) * s_lo + pltpu.roll(y, 32, axis=1) * s_hi


def _epi_q(zs, extras, outs, rows, *, tn):
    g_ref, c_ref, slo_ref, shi_ref = extras
    (q_ref,) = outs
    (z,) = zs
    for h in range(tn // HEAD_DIM):
        sl = slice(h * HEAD_DIM, (h + 1) * HEAD_DIM)
        y = _head_rms(z[:, sl], g_ref[...])
        q_ref[rows, sl] = _rope(y, c_ref[rows, :], slo_ref[rows, :], shi_ref[rows, :]).astype(BF16)


def _epi_k(zs, extras, outs, rows, *, tn):
    g_ref, c_ref, slo_ref, shi_ref = extras
    k_att_ref, k_new_ref = outs
    (z,) = zs
    for h in range(tn // HEAD_DIM):
        sl = slice(h * HEAD_DIM, (h + 1) * HEAD_DIM)
        y = _head_rms(z[:, sl], g_ref[...])
        k_new_ref[rows, sl] = y
        k_att_ref[rows, sl] = _rope(y, c_ref[rows, :], slo_ref[rows, :], shi_ref[rows, :]).astype(BF16)


def _epi_v(zs, extras, outs, rows):
    v_att_ref, v_new_ref = outs
    v_att_ref[rows, :] = zs[0].astype(BF16)
    v_new_ref[rows, :] = zs[0]


def _epi_cast(zs, extras, outs, rows):
    outs[0][rows, :] = zs[0].astype(outs[0].dtype)


def _epi_scale(zs, extras, outs, rows, *, scale):
    outs[0][rows, :] = zs[0] * scale


def _epi_silu(zs, extras, outs, rows):
    outs[0][rows, :] = jax.nn.silu(zs[0])


def _epi_sigmoid(zs, extras, outs, rows):
    outs[0][rows, :] = jax.nn.sigmoid(zs[0])


def _epi_merge(zs, extras, outs, rows):
    ga_ref, gr_ref = extras
    outs[0][rows, :] = (ga_ref[rows, :] * zs[0] + gr_ref[rows, :] * zs[1]).astype(BF16)


def _epi_residual2(zs, extras, outs, rows, *, tm):
    xp_ref, xs_ref, gate_ref = extras
    x = jnp.where(pl.program_id(1) < T_P // tm, xp_ref[rows, :], xs_ref[rows, :])
    outs[0][rows, :] = x + gate_ref[...] * zs[0]


def _epi_swiglu(zs, extras, outs, rows):
    outs[0][rows, :] = (jax.nn.silu(zs[0]) * zs[1]).astype(BF16)


def _epi_residual(zs, extras, outs, rows):
    x_ref, gate_ref = extras
    outs[0][rows, :] = x_ref[rows, :] + gate_ref[...] * zs[0]


_NT = (((1,), (1,)), ((), ()))


_SOFTMAX_EXP2_SCALE = HEAD_DIM ** -0.5 * 1.4426950408889634


def _attn_body(*refs, has_ctx, seq, tq, n_sub):
    if has_ctx:
        q_ref, k_ref, v_ref, ck_ref, cv_ref, o_ref = refs
        ck = ck_ref[...].astype(BF16)
        cv = cv_ref[...].astype(BF16)
    else:
        q_ref, k_ref, v_ref, o_ref = refs
    for i in range(n_sub):
        k = k_ref[i * seq:(i + 1) * seq, :]
        v = v_ref[i * seq:(i + 1) * seq, :]
        rows = slice(i * tq, (i + 1) * tq)
        for h in range(GQA_GROUP):
            sl = slice(h * HEAD_DIM, (h + 1) * HEAD_DIM)
            q = q_ref[rows, sl]
            s_new = lax.dot_general(q, k, _NT, preferred_element_type=F32)
            mx = jnp.max(s_new, axis=-1, keepdims=True)
            if has_ctx:
                s_ctx = lax.dot_general(q, ck, _NT, preferred_element_type=F32)
                mx = jnp.maximum(mx, jnp.max(s_ctx, axis=-1, keepdims=True))
                p_ctx = jnp.exp2((s_ctx - mx) * _SOFTMAX_EXP2_SCALE)
                p_new = jnp.exp2((s_new - mx) * _SOFTMAX_EXP2_SCALE)
                den = jnp.sum(p_ctx, axis=-1, keepdims=True) + jnp.sum(p_new, axis=-1, keepdims=True)
                o = jnp.dot(p_ctx.astype(BF16), cv, preferred_element_type=F32)
                o = o + jnp.dot(p_new.astype(BF16), v, preferred_element_type=F32)
            else:
                p_new = jnp.exp2((s_new - mx) * _SOFTMAX_EXP2_SCALE)
                den = jnp.sum(p_new, axis=-1, keepdims=True)
                o = jnp.dot(p_new.astype(BF16), v, preferred_element_type=F32)
            o_ref[rows, sl] = (o * (1.0 / den)).astype(BF16)


def _attn_call(q_att, k_att, v_att, ctx_k, ctx_v, *, latent):
    gw = GQA_GROUP * HEAD_DIM
    if latent:
        seq, nb, tq, n_sub = DEC_SEQ, DEC_BATCH, 512, 1
        row0 = T_P
    else:
        seq, nb, tq, n_sub = SEQ, BATCH, SEQ, 4
        row0 = 0
    nqt = seq // tq
    q_rows, kv_rows = n_sub * tq, n_sub * seq
    q_spec = pl.BlockSpec((q_rows, gw), lambda b, g, t: (row0 // q_rows + b * nqt + t, g))
    o_spec = pl.BlockSpec((q_rows, gw), lambda b, g, t: (b * nqt + t, g))
    kv_spec = pl.BlockSpec((kv_rows, HEAD_DIM), lambda b, g, t: (row0 // kv_rows + b, g))
    in_specs = [q_spec, kv_spec, kv_spec]
    args = [q_att, k_att, v_att]
    n_keys = seq
    if latent:
        c_spec = pl.BlockSpec((None, PAST_LEN, HEAD_DIM), lambda b, g, t: (b, 0, g))
        in_specs += [c_spec, c_spec]
        args += [ctx_k, ctx_v]
        n_keys += PAST_LEN
    est = 4 * _nbytes((q_rows, gw), BF16) + 4 * _nbytes((kv_rows, HEAD_DIM), BF16)
    est += 4 * _nbytes((PAST_LEN, HEAD_DIM), F32)
    est += 3 * GQA_GROUP * n_sub * _nbytes((tq, n_keys), F32)
    return pl.pallas_call(
        functools.partial(_attn_body, has_ctx=latent, seq=seq, tq=tq, n_sub=n_sub),
        grid=(nb // n_sub, N_KV_HEADS, nqt),
        in_specs=in_specs,
        out_specs=o_spec,
        out_shape=jax.ShapeDtypeStruct((nb * seq, ATTN_Q), BF16),
        compiler_params=_params(est, 3),
        name="attn_latent" if latent else "attn_prompt",
    )(*args)


_TN = (((0,), (0,)), ((), ()))


def _ret_body(*refs, seq, tq, latent, n_sub):
    if latent:
        (lgf_ref, lgb_ref, q_ref, k_ref, v_ref, sg_ref, rn_ref, s0f_ref, s0b_ref,
         o_ref, d_ref) = refs
    else:
        (lgf_ref, lgb_ref, q_ref, k_ref, v_ref, sg_ref, rn_ref,
         o_ref, sf_ref, sb_ref, d_ref) = refs
    h = pl.program_id(0)
    lgf = lgf_ref[h]
    lgb = lgb_ref[h]

    @pl.when(pl.program_id(1) == 0)
    def _():
        for t in range(seq // tq):
            i = lax.broadcasted_iota(jnp.int32, (tq, seq), 0) + t * tq
            j = lax.broadcasted_iota(jnp.int32, (tq, seq), 1)
            diff = (i - j).astype(F32)
            arg = jnp.where(diff >= 0, diff * lgf, -diff * lgb)
            d_ref[t * tq:(t + 1) * tq, :] = jnp.exp(arg) * jnp.where(diff == 0, 2.0, 1.0)

    if latent:
        s0f = s0f_ref[...].astype(BF16)
        s0b = s0b_ref[...].astype(BF16)
    for i in range(n_sub):
        k_f32 = k_ref[i * seq:(i + 1) * seq, :]
        kb = k_f32.astype(BF16)
        v = v_ref[i * seq:(i + 1) * seq, :]
        for t in range(seq // tq):
            rows = slice(i * seq + t * tq, i * seq + (t + 1) * tq)
            q = q_ref[rows, :]
            raw = lax.dot_general(q, kb, _NT, preferred_element_type=F32)
            p = (raw * d_ref[t * tq:(t + 1) * tq, :]).astype(BF16)
            o = jnp.dot(p, v, preferred_element_type=F32)
            if latent:
                pos = (lax.broadcasted_iota(jnp.int32, (tq, RET_VAL_DIM), 0) + t * tq).astype(F32)
                xi_f = jnp.exp((pos + 1.0) * lgf)
                xi_b = jnp.exp((seq - pos) * lgb)
                o = o + jnp.dot(q, s0f, preferred_element_type=F32) * xi_f
                o = o + jnp.dot(q, s0b, preferred_element_type=F32) * xi_b
            mu = jnp.mean(o, axis=-1, keepdims=True)
            oc = o - mu
            var = jnp.mean(oc * oc, axis=-1, keepdims=True)
            y = (oc * lax.rsqrt(var + EPS)) * rn_ref[...]
            o_ref[rows, :] = (y * sg_ref[rows, :]).astype(BF16)
        if not latent:
            j = lax.broadcasted_iota(jnp.int32, (seq, RET_KEY_DIM), 0).astype(F32)
            kz_f = (k_f32 * jnp.exp((seq - 1.0 - j) * lgf)).astype(BF16)
            kz_b = (k_f32 * jnp.exp(j * lgb)).astype(BF16)
            sf_ref[i] = lax.dot_general(kz_f, v, _TN, preferred_element_type=F32)
            sb_ref[i] = lax.dot_general(kz_b, v, _TN, preferred_element_type=F32)


def _ret_call(lgf, lgb, q_ret, k_ret, v_ret, sg, ret_norm, s0f, s0b, *, latent):
    if latent:
        seq, nb, row0, n_sub = DEC_SEQ, DEC_BATCH, T_P, 1
    else:
        seq, nb, row0, n_sub = SEQ, BATCH, 0, 4
    tq = 256
    rows = n_sub * seq
    rb = row0 // rows
    smem = pl.BlockSpec(memory_space=pltpu.SMEM)
    qk_spec = pl.BlockSpec((rows, RET_KEY_DIM), lambda h, b: (rb + b, h))
    v_spec = pl.BlockSpec((rows, RET_VAL_DIM), lambda h, b: (rb + b, h))
    o_spec = pl.BlockSpec((rows, RET_VAL_DIM), lambda h, b: (b, h))
    in_specs = [smem, smem, qk_spec, qk_spec, v_spec, v_spec,
                pl.BlockSpec((1, RET_VAL_DIM), lambda h, b: (0, h))]
    args = [lgf, lgb, q_ret, k_ret, v_ret, sg, ret_norm]
    o_shape = jax.ShapeDtypeStruct((nb * seq, RET_V), BF16)
    if latent:
        st_spec = pl.BlockSpec((None, None, None, RET_KEY_DIM, RET_VAL_DIM),
                               lambda h, b: (b, 0, h, 0, 0))
        in_specs += [st_spec, st_spec]
        args += [s0f, s0b]
        out_specs = o_spec
        out_shape = o_shape
    else:
        st_spec = pl.BlockSpec((n_sub, None, None, RET_KEY_DIM, RET_VAL_DIM),
                               lambda h, b: (b, 0, h, 0, 0))
        st_shape = jax.ShapeDtypeStruct((BATCH, 1, N_RET_HEADS, RET_KEY_DIM, RET_VAL_DIM), F32)
        out_specs = [o_spec, st_spec, st_spec]
        out_shape = [o_shape, st_shape, st_shape]
    est = _nbytes((seq, seq), F32) + 4 * n_sub * _nbytes((tq, seq), F32)
    est += 2 * (_nbytes((rows, RET_KEY_DIM), BF16) + _nbytes((rows, RET_KEY_DIM), F32))
    est += 2 * (2 * _nbytes((rows, RET_VAL_DIM), BF16) + _nbytes((rows, RET_VAL_DIM), F32))
    est += 8 * n_sub * _nbytes((RET_KEY_DIM, RET_VAL_DIM), F32)
    return pl.pallas_call(
        functools.partial(_ret_body, seq=seq, tq=tq, latent=latent, n_sub=n_sub),
        grid=(N_RET_HEADS, nb // n_sub),
        in_specs=in_specs,
        out_specs=out_specs,
        out_shape=out_shape,
        scratch_shapes=[pltpu.VMEM((seq, seq), F32)],
        compiler_params=_params(est, 2),
        name="ret_latent" if latent else "ret_prompt",
    )(*args)


def _final_body(x_ref, g_ref, yp_ref, ys_ref, *, n_p):
    x = x_ref[...]
    y = x * lax.rsqrt(jnp.mean(x * x, axis=-1, keepdims=True) + EPS) * g_ref[...]
    m = pl.program_id(0)

    @pl.when(m < n_p)
    def _():
        yp_ref[...] = y

    @pl.when(m >= n_p)
    def _():
        ys_ref[...] = y


def _final_call(x, g):
    tm = 512
    n_p = T_P // tm
    blk = (tm, D_MODEL)
    est = 6 * _nbytes(blk, F32) + 2 * _nbytes(blk, F32)
    return pl.pallas_call(
        functools.partial(_final_body, n_p=n_p),
        grid=(T // tm,),
        in_specs=[pl.BlockSpec(blk, lambda m: (m, 0)), pl.BlockSpec((1, D_MODEL), lambda m: (0, 0))],
        out_specs=[
            pl.BlockSpec(blk, lambda m: (jnp.minimum(m, n_p - 1), 0)),
            pl.BlockSpec(blk, lambda m: (jnp.maximum(m - n_p, 0), 0)),
        ],
        out_shape=[jax.ShapeDtypeStruct((T_P, D_MODEL), F32), jax.ShapeDtypeStruct((T_S, D_MODEL), F32)],
        compiler_params=_params(est, 1),
        name="final_norm",
    )(x, g)


def _rope_tables():
    rows = DEC_SEQ // GRID_W
    row = jnp.repeat(jnp.arange(rows, dtype=F32), GRID_W)
    col = jnp.tile(jnp.arange(GRID_W, dtype=F32), rows)
    quarter = HEAD_DIM // 4
    inv_freq = ROPE_THETA ** (-jnp.arange(quarter, dtype=F32) / quarter)
    ang_r = row[:, None] * inv_freq[None, :]
    ang_c = col[:, None] * inv_freq[None, :]
    cr, sr, cc, sc = jnp.cos(ang_r), jnp.sin(ang_r), jnp.cos(ang_c), jnp.sin(ang_c)
    zero = jnp.zeros_like(sr)
    cos = jnp.concatenate([cr, cr, cc, cc], axis=-1)
    s_lo = jnp.concatenate([-sr, zero, -sc, zero], axis=-1)
    s_hi = jnp.concatenate([zero, sr, zero, sc], axis=-1)
    ident = jnp.zeros((DEC_SEQ, HEAD_DIM), F32)
    return (jnp.concatenate([ident + 1.0, cos], axis=0),
            jnp.concatenate([ident, s_lo], axis=0),
            jnp.concatenate([ident, s_hi], axis=0))


def kernel(x_prompt, x_sample, cache_attn_k, cache_attn_v, state_ret_fwd, state_ret_bwd, c, c_ctx,
           norm_attn, norm_ffn, w_mod, b_mod, w_in, q_norm, k_norm, ret_decay_fwd, ret_decay_bwd,
           ret_norm, w_branch_attn, w_branch_ret, w_out, w_ffn_gate, w_ffn_up, w_ffn_down, final_norm):
    xp = x_prompt.reshape(T_P, D_MODEL)
    xs = x_sample.reshape(T_S, D_MODEL)
    ctx_k = cache_attn_k.reshape(DEC_BATCH, PAST_LEN, ATTN_KV)
    ctx_v = cache_attn_v.reshape(DEC_BATCH, PAST_LEN, ATTN_KV)
    lgf = jax.nn.log_sigmoid(ret_decay_fwd[0].astype(F32))
    lgb = jax.nn.log_sigmoid(ret_decay_bwd[0].astype(F32))
    cos, s_lo, s_hi = _rope_tables()

    c_all = jnp.concatenate(
        [c_ctx[None, :], c, jnp.zeros((MOD_ROWS - 1 - DEC_BATCH, D_MODEL), F32)], axis=0)
    mod = _mod_call(c_all, w_mod[0], b_mod)
    mod3 = mod.reshape(MOD_ROWS, 1, 6 * D_MODEL)
    h = _prenorm_call([xp, xs], norm_attn, mod3, 0, 1, "prenorm_attn")

    w_in0 = w_in[0]
    tm = 1024
    tbl = pl.BlockSpec((DEC_SEQ, HEAD_DIM), lambda n, m: (jnp.where(m < T_P // tm, 0, 1), 0))
    hd = pl.BlockSpec((1, HEAD_DIM), lambda n, m: (0, 0))
    rope_extras = [(cos, tbl), (s_lo, tbl), (s_hi, tbl)]

    def proj(col0, n_cols, tn, epilogue, extras, outs, name):
        return _matmul_call([h], [(w_in0, col0)], [(0, 0)], n_cols, tm, tn, epilogue, extras, outs, name)

    def full(n_cols, dtype):
        return jax.ShapeDtypeStruct((T, n_cols), dtype)

    (q_att,) = proj(COL_Q, ATTN_Q, 1024, functools.partial(_epi_q, tn=1024),
                    [(q_norm, hd)] + rope_extras,
                    [(full(ATTN_Q, BF16), _tile_spec(tm, 1024))], "proj_q")
    prompt_kv = jax.ShapeDtypeStruct((T_P + tm, ATTN_KV), F32)
    k_att, k_new = proj(COL_K, ATTN_KV, ATTN_KV, functools.partial(_epi_k, tn=ATTN_KV),
                        [(k_norm, hd)] + rope_extras,
                        [(full(ATTN_KV, BF16), _tile_spec(tm, ATTN_KV)),
                         (prompt_kv, _prompt_tile_spec(tm, ATTN_KV))], "proj_k")
    v_att, v_new = proj(COL_V, ATTN_KV, ATTN_KV, _epi_v, [],
                        [(full(ATTN_KV, BF16), _tile_spec(tm, ATTN_KV)),
                         (prompt_kv, _prompt_tile_spec(tm, ATTN_KV))], "proj_v")
    (q_ret,) = proj(COL_QR, RET_QK, 1024, _epi_cast, [],
                    [(full(RET_QK, BF16), _tile_spec(tm, 1024))], "proj_qr")
    (k_ret,) = proj(COL_KR, RET_QK, 1024, functools.partial(_epi_scale, scale=RET_KEY_DIM ** -0.5), [],
                    [(full(RET_QK, F32), _tile_spec(tm, 1024))], "proj_kr")
    (v_ret,) = proj(COL_VR, RET_V, 1024, _epi_cast, [],
                    [(full(RET_V, BF16), _tile_spec(tm, 1024))], "proj_vr")
    (sg,) = proj(COL_GR, RET_V, 1024, _epi_silu, [],
                 [(full(RET_V, F32), _tile_spec(tm, 1024))], "proj_gr")
    (gates,) = proj(COL_GATES, 2 * D_MODEL, 1024, _epi_sigmoid, [],
                    [(full(2 * D_MODEL, F32), _tile_spec(tm, 1024))], "proj_gates")

    o_a_p = _attn_call(q_att, k_att, v_att, None, None, latent=False)
    o_a_s = _attn_call(q_att, k_att, v_att, ctx_k, ctx_v, latent=True)
    o_r_p, new_sf, new_sb = _ret_call(lgf, lgb, q_ret, k_ret, v_ret, sg, ret_norm,
                                      None, None, latent=False)
    o_r_s = _ret_call(lgf, lgb, q_ret, k_ret, v_ret, sg, ret_norm,
                      state_ret_fwd, state_ret_bwd, latent=True)

    tm2, tn2 = 512, 1024
    (merged,) = _matmul_call(
        [(o_a_p, o_a_s), (o_r_p, o_r_s)], [(w_branch_attn[0], 0), (w_branch_ret[0], 0)],
        [(0, 0), (1, 1)], D_MODEL, tm2, tn2, _epi_merge,
        [(gates, _tile_spec(tm2, tn2)), (gates, _tile_spec(tm2, tn2, D_MODEL // tn2))],
        [(full(D_MODEL, BF16), _tile_spec(tm2, tn2))], "merge")

    tm3, tn3 = 1024, 1024
    n_p3 = T_P // tm3
    (x1,) = _matmul_call(
        [merged], [(w_out[0], 0)], [(0, 0)], D_MODEL, tm3, tn3,
        functools.partial(_epi_residual2, tm=tm3),
        [(xp, pl.BlockSpec((tm3, tn3), lambda n, m: (jnp.minimum(m, n_p3 - 1), n))),
         (xs, pl.BlockSpec((tm3, tn3), lambda n, m: (jnp.maximum(m - n_p3, 0), n))),
         (mod3, _mod_spec(tm3, tn3, lambda n: 2 * (D_MODEL // tn3) + n, 2))],
        [(full(D_MODEL, F32), _tile_spec(tm3, tn3))], "out_proj")

    h2 = _prenorm_call([x1], norm_ffn, mod3, 3, 4, "prenorm_ffn")

    (act,) = _matmul_call(
        [h2], [(w_ffn_gate[0], 0), (w_ffn_up[0], 0)], [(0, 0), (0, 1)],
        D_FF, 2048, 512, _epi_swiglu, [],
        [(full(D_FF, BF16), _tile_spec(2048, 512))], "ffn_up", kc=512)

    tm4, tn4 = 512, 1024
    (y_pre,) = _matmul_call(
        [act], [(w_ffn_down[0], 0)], [(0, 0)], D_MODEL, tm4, tn4, _epi_residual,
        [(x1, _tile_spec(tm4, tn4)),
         (mod3, _mod_spec(tm4, tn4, lambda n: 5 * (D_MODEL // tn4) + n, 2))],
        [(full(D_MODEL, F32), _tile_spec(tm4, tn4))], "ffn_down", kc=512)

    y_p, y_s = _final_call(y_pre, final_norm[None, :])

    return (y_p.reshape(BATCH, SEQ, D_MODEL),
            y_s.reshape(DEC_BATCH, DEC_SEQ, D_MODEL),
            k_new[:T_P].reshape(BATCH, 1, SEQ, N_KV_HEADS, HEAD_DIM),
            v_new[:T_P].reshape(BATCH, 1, SEQ, N_KV_HEADS, HEAD_DIM),
            new_sf, new_sb)
```

```python
import functools

import jax
import jax.numpy as jnp
from jax import lax
from jax.experimental import pallas as pl
from jax.experimental.pallas import tpu as pltpu

D_MODEL = 2048
BATCH = 16
SEQ = 256
DEC_BATCH = 8
DEC_SEQ = 1024
PAST_LEN = 512
GRID_W = 64
N_HEADS = 16
N_KV_HEADS = 4
HEAD_DIM = 128
ROPE_THETA = 10000.0
N_RET_HEADS = 8
RET_KEY_DIM = 128
RET_VAL_DIM = 256
D_FF = 5632
EPS = 1e-6

ATTN_Q = N_HEADS * HEAD_DIM
ATTN_KV = N_KV_HEADS * HEAD_DIM
RET_QK = N_RET_HEADS * RET_KEY_DIM
RET_V = N_RET_HEADS * RET_VAL_DIM
GQA_GROUP = N_HEADS // N_KV_HEADS

COL_Q = 0
COL_K = COL_Q + ATTN_Q
COL_V = COL_K + ATTN_KV
COL_QR = COL_V + ATTN_KV
COL_KR = COL_QR + RET_QK
COL_VR = COL_KR + RET_QK
COL_GR = COL_VR + RET_V
COL_GATES = COL_GR + RET_V

T_P = BATCH * SEQ
T_S = DEC_BATCH * DEC_SEQ
T = T_P + T_S
MOD_ROWS = 16

V7X_VMEM_BYTES = 64 * 1024 * 1024
VMEM_CAP = V7X_VMEM_BYTES - 6 * 1024 * 1024

BF16 = jnp.bfloat16
F32 = jnp.float32


def _nbytes(shape, dtype):
    n = 1
    for s in shape:
        if s is not None:
            n *= s
    return n * jnp.dtype(dtype).itemsize


def _params(vmem_estimate, n_grid):
    limit = min(VMEM_CAP, int(vmem_estimate) + 8 * 1024 * 1024)
    return pltpu.CompilerParams(
        dimension_semantics=("arbitrary",) * n_grid, vmem_limit_bytes=limit)


def _mod_row(m, tm):
    n_p = T_P // tm
    return jnp.where(m < n_p, 0, 1 + (m - n_p) // (DEC_SEQ // tm))


def _mod_spec(tm, width, col_block, grid_rank):
    if grid_rank == 1:
        return pl.BlockSpec((None, 1, width), lambda m: (_mod_row(m, tm), 0, col_block(0)))
    return pl.BlockSpec((None, 1, width), lambda n, m: (_mod_row(m, tm), 0, col_block(n)))


def _mod_body(c_ref, w_ref, b_ref, o_ref):
    a = jax.nn.silu(c_ref[...]).astype(BF16)
    w = w_ref[...].astype(BF16)
    o_ref[...] = jnp.dot(a, w, preferred_element_type=F32) + b_ref[...]


def _mod_call(c_all, w_mod, b_mod):
    tn = 1024
    n_out = 6 * D_MODEL
    est = 2 * _nbytes((D_MODEL, tn), F32) + _nbytes((D_MODEL, tn), BF16)
    return pl.pallas_call(
        _mod_body,
        grid=(n_out // tn,),
        in_specs=[
            pl.BlockSpec((MOD_ROWS, D_MODEL), lambda n: (0, 0)),
            pl.BlockSpec((D_MODEL, tn), lambda n: (0, n)),
            pl.BlockSpec((1, tn), lambda n: (0, n)),
        ],
        out_specs=pl.BlockSpec((MOD_ROWS, tn), lambda n: (0, n)),
        out_shape=jax.ShapeDtypeStruct((MOD_ROWS, n_out), F32),
        compiler_params=_params(est, 1),
        name="mod_table",
    )(c_all, w_mod, b_mod)


def _modulated_norm(x, g, sc, sh):
    y = x * lax.rsqrt(jnp.mean(x * x, axis=-1, keepdims=True) + EPS)
    return (y * g) * (1.0 + sc) + sh


def _prenorm2_body(xp_ref, xs_ref, g_ref, sc_ref, sh_ref, o_ref, *, n_p):
    m = pl.program_id(0)

    @pl.when(m < n_p)
    def _():
        o_ref[...] = _modulated_norm(xp_ref[...], g_ref[...], sc_ref[...], sh_ref[...]).astype(BF16)

    @pl.when(m >= n_p)
    def _():
        o_ref[...] = _modulated_norm(xs_ref[...], g_ref[...], sc_ref[...], sh_ref[...]).astype(BF16)


def _prenorm1_body(x_ref, g_ref, sc_ref, sh_ref, o_ref):
    o_ref[...] = _modulated_norm(x_ref[...], g_ref[...], sc_ref[...], sh_ref[...]).astype(BF16)


def _prenorm_call(xs, norm_w, mod3, sh_blk, sc_blk, name):
    tm = 512
    n_p = T_P // tm
    blk = (tm, D_MODEL)
    common = [
        pl.BlockSpec((1, D_MODEL), lambda m: (0, 0)),
        _mod_spec(tm, D_MODEL, lambda n: sc_blk, 1),
        _mod_spec(tm, D_MODEL, lambda n: sh_blk, 1),
    ]
    if len(xs) == 2:
        in_specs = [
            pl.BlockSpec(blk, lambda m: (jnp.minimum(m, n_p - 1), 0)),
            pl.BlockSpec(blk, lambda m: (jnp.maximum(m - n_p, 0), 0)),
        ] + common
        body = functools.partial(_prenorm2_body, n_p=n_p)
    else:
        in_specs = [pl.BlockSpec(blk, lambda m: (m, 0))] + common
        body = _prenorm1_body
    est = 2 * len(xs) * _nbytes(blk, F32) + 2 * _nbytes(blk, BF16) + 2 * _nbytes(blk, F32)
    return pl.pallas_call(
        body,
        grid=(T // tm,),
        in_specs=in_specs,
        out_specs=pl.BlockSpec(blk, lambda m: (m, 0)),
        out_shape=jax.ShapeDtypeStruct((T, D_MODEL), BF16),
        compiler_params=_params(est, 1),
        name=name,
    )(*xs, norm_w, mod3, mod3)


def _matmul_body(*refs, lhs_split, n_rhs, col0s, pairs, n_extra, n_out, epilogue,
                 tm, tn, kc, row_chunk):
    n_lhs_refs = sum(2 if s else 1 for s in lhs_split)
    lhs_refs = refs[:n_lhs_refs]
    pos = n_lhs_refs
    rhs = refs[pos:pos + n_rhs]
    pos += n_rhs
    extras = refs[pos:pos + n_extra]
    pos += n_extra
    outs = refs[pos:pos + n_out]
    pos += n_out
    wb = refs[pos:pos + n_rhs]
    stage = refs[pos + n_rhs:pos + 2 * n_rhs]
    sems = refs[pos + 2 * n_rhs:pos + 3 * n_rhs]

    n, m = pl.program_id(0), pl.program_id(1)
    n_tiles = pl.num_programs(0)
    k_dim = wb[0].shape[1]
    n_chunks = k_dim // kc
    cur = n % 2

    def chunk_copy(j, tile, chunk, slot):
        col = pl.multiple_of(col0s[j] + tile * tn, tn)
        row = pl.multiple_of(chunk * kc, kc)
        return pltpu.make_async_copy(
            rhs[j].at[pl.ds(row, kc), pl.ds(col, tn)], stage[j].at[slot], sems[j].at[slot])

    nxt = jnp.minimum(n + 1, n_tiles - 1)
    chunk = jnp.minimum(m, n_chunks - 1)
    first = jnp.logical_and(n == 0, m == 0)

    def start_prefetch():
        for j in range(n_rhs):
            chunk_copy(j, nxt, chunk, 0).start()

    @pl.when(first)
    def _():
        for j in range(n_rhs):
            chunk_copy(j, 0, 0, 0).start()
            for c in range(n_chunks):
                if c + 1 < n_chunks:
                    chunk_copy(j, 0, c + 1, (c + 1) % 2).start()
                chunk_copy(j, 0, c, c % 2).wait()
                wb[j][0, c * kc:(c + 1) * kc, :] = stage[j][c % 2].astype(BF16)
        start_prefetch()

    pl.when(jnp.logical_not(first))(start_prefetch)

    is_prompt = m < T_P // tm

    def lhs_rows(i, rows):
        first = sum(2 if s else 1 for s in lhs_split[:i])
        if lhs_split[i]:
            return jnp.where(is_prompt, lhs_refs[first][rows, :], lhs_refs[first + 1][rows, :])
        return lhs_refs[first][rows, :]

    n_rc = tm // row_chunk
    for r in range(n_rc):
        rows = slice(r * row_chunk, (r + 1) * row_chunk)
        zs = [jnp.dot(lhs_rows(i, rows), wb[j][cur], preferred_element_type=F32) for i, j in pairs]
        epilogue(zs, extras, outs, rows)
        if r == n_rc - 1:
            for j in range(n_rhs):
                chunk_copy(j, nxt, chunk, 0).wait()
                wb[j][1 - cur, pl.ds(pl.multiple_of(chunk * kc, kc), kc), :] = stage[j][0].astype(BF16)


def _matmul_call(lhs_list, rhs_list, pairs, n_cols, tm, tn, epilogue, extras, outs, name,
                 row_chunk=256, kc=256):
    lhs_split = tuple(isinstance(a, tuple) for a in lhs_list)
    k_dim = (lhs_list[0][0] if lhs_split[0] else lhs_list[0]).shape[1]
    n_p = T_P // tm
    grid = (n_cols // tn, T // tm)
    assert k_dim % kc == 0 and k_dim // kc <= grid[1], (k_dim, kc, grid)
    in_specs, lhs_args = [], []
    for a in lhs_list:
        if isinstance(a, tuple):
            in_specs.append(pl.BlockSpec((tm, k_dim), lambda n, m: (jnp.minimum(m, n_p - 1), 0)))
            in_specs.append(pl.BlockSpec((tm, k_dim), lambda n, m: (jnp.maximum(m - n_p, 0), 0)))
            lhs_args += list(a)
        else:
            in_specs.append(pl.BlockSpec((tm, k_dim), lambda n, m: (m, 0)))
            lhs_args.append(a)
    in_specs += [pl.BlockSpec(memory_space=pl.ANY) for _ in rhs_list]
    in_specs += [spec for _, spec in extras]
    est = 2 * len(lhs_args) * _nbytes((tm, k_dim), BF16)
    est += len(rhs_list) * 2 * (_nbytes((k_dim, tn), BF16) + _nbytes((kc, tn), F32))
    for arr, spec in list(extras) + list(outs):
        est += 2 * _nbytes(spec.block_shape, arr.dtype)
    est += 3 * len(pairs) * _nbytes((row_chunk, tn), F32)
    body = functools.partial(
        _matmul_body, lhs_split=lhs_split, n_rhs=len(rhs_list),
        col0s=tuple(c for _, c in rhs_list), pairs=tuple(pairs),
        n_extra=len(extras), n_out=len(outs), epilogue=epilogue,
        tm=tm, tn=tn, kc=kc, row_chunk=row_chunk)
    scratch = [pltpu.VMEM((2, k_dim, tn), BF16) for _ in rhs_list]
    scratch += [pltpu.VMEM((2, kc, tn), F32) for _ in rhs_list]
    scratch += [pltpu.SemaphoreType.DMA((2,)) for _ in rhs_list]
    res = pl.pallas_call(
        body,
        grid=grid,
        in_specs=in_specs,
        out_specs=[spec for _, spec in outs],
        out_shape=[s for s, _ in outs],
        scratch_shapes=scratch,
        compiler_params=_params(est, 2),
        name=name,
    )(*lhs_args, *[w for w, _ in rhs_list], *[a for a, _ in extras])
    return res


def _tile_spec(tm, tn, col_block0=0):
    return pl.BlockSpec((tm, tn), lambda n, m, c=col_block0: (m, c + n))


def _prompt_tile_spec(tm, tn):
    n_p = T_P // tm
    return pl.BlockSpec((tm, tn), lambda n, m: (jnp.minimum(m, n_p), n))


def _head_rms(x, g):
    return x * lax.rsqrt(jnp.mean(x * x, axis=-1, keepdims=True) + EPS) * g


def _rope(y, c, s_lo, s_hi):
    return y * c + pltpu.roll(y, HEAD_DIM - 32, axis=1) * s_lo + pltpu.roll(y, 32, axis=1) * s_hi


def _epi_q(zs, extras, outs, rows, *, tn):
    g_ref, c_ref, slo_ref, shi_ref = extras
    (q_ref,) = outs
    (z,) = zs
    for h in range(tn // HEAD_DIM):
        sl = slice(h * HEAD_DIM, (h + 1) * HEAD_DIM)
        y = _head_rms(z[:, sl], g_ref[...])
        q_ref[rows, sl] = _rope(y, c_ref[rows, :], slo_ref[rows, :], shi_ref[rows, :]).astype(BF16)


def _epi_k(zs, extras, outs, rows, *, tn):
    g_ref, c_ref, slo_ref, shi_ref = extras
    k_att_ref, k_new_ref = outs
    (z,) = zs
    for h in range(tn // HEAD_DIM):
        sl = slice(h * HEAD_DIM, (h + 1) * HEAD_DIM)
        y = _head_rms(z[:, sl], g_ref[...])
        k_new_ref[rows, sl] = y
        k_att_ref[rows, sl] = _rope(y, c_ref[rows, :], slo_ref[rows, :], shi_ref[rows, :]).astype(BF16)


def _epi_v(zs, extras, outs, rows):
    v_att_ref, v_new_ref = outs
    v_att_ref[rows, :] = zs[0].astype(BF16)
    v_new_ref[rows, :] = zs[0]


def _epi_cast(zs, extras, outs, rows):
    outs[0][rows, :] = zs[0].astype(outs[0].dtype)


def _epi_scale(zs, extras, outs, rows, *, scale):
    outs[0][rows, :] = zs[0] * scale


def _epi_silu(zs, extras, outs, rows):
    outs[0][rows, :] = jax.nn.silu(zs[0])


def _epi_sigmoid(zs, extras, outs, rows):
    outs[0][rows, :] = jax.nn.sigmoid(zs[0])


def _epi_merge(zs, extras, outs, rows):
    ga_ref, gr_ref = extras
    outs[0][rows, :] = (ga_ref[rows, :] * zs[0] + gr_ref[rows, :] * zs[1]).astype(BF16)


def _epi_residual2(zs, extras, outs, rows, *, tm):
    xp_ref, xs_ref, gate_ref = extras
    x = jnp.where(pl.program_id(1) < T_P // tm, xp_ref[rows, :], xs_ref[rows, :])
    outs[0][rows, :] = x + gate_ref[...] * zs[0]


def _epi_swiglu(zs, extras, outs, rows):
    outs[0][rows, :] = (jax.nn.silu(zs[0]) * zs[1]).astype(BF16)


def _epi_residual(zs, extras, outs, rows):
    x_ref, gate_ref = extras
    outs[0][rows, :] = x_ref[rows, :] + gate_ref[...] * zs[0]


_NT = (((1,), (1,)), ((), ()))


_SOFTMAX_EXP2_SCALE = HEAD_DIM ** -0.5 * 1.4426950408889634


def _attn_body(*refs, has_ctx, seq, tq, n_sub):
    if has_ctx:
        q_ref, k_ref, v_ref, ck_ref, cv_ref, o_ref = refs
        ck = ck_ref[...].astype(BF16)
        cv = cv_ref[...].astype(BF16)
    else:
        q_ref, k_ref, v_ref, o_ref = refs
    for i in range(n_sub):
        k = k_ref[i * seq:(i + 1) * seq, :]
        v = v_ref[i * seq:(i + 1) * seq, :]
        rows = slice(i * tq, (i + 1) * tq)
        for h in range(GQA_GROUP):
            sl = slice(h * HEAD_DIM, (h + 1) * HEAD_DIM)
            q = q_ref[rows, sl]
            s_new = lax.dot_general(q, k, _NT, preferred_element_type=F32)
            mx = jnp.max(s_new, axis=-1, keepdims=True)
            if has_ctx:
                s_ctx = lax.dot_general(q, ck, _NT, preferred_element_type=F32)
                mx = jnp.maximum(mx, jnp.max(s_ctx, axis=-1, keepdims=True))
                p_ctx = jnp.exp2((s_ctx - mx) * _SOFTMAX_EXP2_SCALE)
                p_new = jnp.exp2((s_new - mx) * _SOFTMAX_EXP2_SCALE)
                den = jnp.sum(p_ctx, axis=-1, keepdims=True) + jnp.sum(p_new, axis=-1, keepdims=True)
                o = jnp.dot(p_ctx.astype(BF16), cv, preferred_element_type=F32)
                o = o + jnp.dot(p_new.astype(BF16), v, preferred_element_type=F32)
            else:
                p_new = jnp.exp2((s_new - mx) * _SOFTMAX_EXP2_SCALE)
                den = jnp.sum(p_new, axis=-1, keepdims=True)
                o = jnp.dot(p_new.astype(BF16), v, preferred_element_type=F32)
            o_ref[rows, sl] = (o * (1.0 / den)).astype(BF16)


def _attn_call(q_att, k_att, v_att, ctx_k, ctx_v, *, latent):
    gw = GQA_GROUP * HEAD_DIM
    if latent:
        seq, nb, tq, n_sub = DEC_SEQ, DEC_BATCH, 512, 1
        row0 = T_P
    else:
        seq, nb, tq, n_sub = SEQ, BATCH, SEQ, 4
        row0 = 0
    nqt = seq // tq
    q_rows, kv_rows = n_sub * tq, n_sub * seq
    q_spec = pl.BlockSpec((q_rows, gw), lambda b, g, t: (row0 // q_rows + b * nqt + t, g))
    o_spec = pl.BlockSpec((q_rows, gw), lambda b, g, t: (b * nqt + t, g))
    kv_spec = pl.BlockSpec((kv_rows, HEAD_DIM), lambda b, g, t: (row0 // kv_rows + b, g))
    in_specs = [q_spec, kv_spec, kv_spec]
    args = [q_att, k_att, v_att]
    n_keys = seq
    if latent:
        c_spec = pl.BlockSpec((None, PAST_LEN, HEAD_DIM), lambda b, g, t: (b, 0, g))
        in_specs += [c_spec, c_spec]
        args += [ctx_k, ctx_v]
        n_keys += PAST_LEN
    est = 4 * _nbytes((q_rows, gw), BF16) + 4 * _nbytes((kv_rows, HEAD_DIM), BF16)
    est += 4 * _nbytes((PAST_LEN, HEAD_DIM), F32)
    est += 3 * GQA_GROUP * n_sub * _nbytes((tq, n_keys), F32)
    return pl.pallas_call(
        functools.partial(_attn_body, has_ctx=latent, seq=seq, tq=tq, n_sub=n_sub),
        grid=(nb // n_sub, N_KV_HEADS, nqt),
        in_specs=in_specs,
        out_specs=o_spec,
        out_shape=jax.ShapeDtypeStruct((nb * seq, ATTN_Q), BF16),
        compiler_params=_params(est, 3),
        name="attn_latent" if latent else "attn_prompt",
    )(*args)


_TN = (((0,), (0,)), ((), ()))


def _ret_body(*refs, seq, tq, latent, n_sub):
    if latent:
        (lgf_ref, lgb_ref, q_ref, k_ref, v_ref, sg_ref, rn_ref, s0f_ref, s0b_ref,
         o_ref, d_ref) = refs
    else:
        (lgf_ref, lgb_ref, q_ref, k_ref, v_ref, sg_ref, rn_ref,
         o_ref, sf_ref, sb_ref, d_ref) = refs
    h = pl.program_id(0)
    lgf = lgf_ref[h]
    lgb = lgb_ref[h]

    @pl.when(pl.program_id(1) == 0)
    def _():
        for t in range(seq // tq):
            i = lax.broadcasted_iota(jnp.int32, (tq, seq), 0) + t * tq
            j = lax.broadcasted_iota(jnp.int32, (tq, seq), 1)
            diff = (i - j).astype(F32)
            arg = jnp.where(diff >= 0, diff * lgf, -diff * lgb)
            d_ref[t * tq:(t + 1) * tq, :] = jnp.exp(arg) * jnp.where(diff == 0, 2.0, 1.0)

    if latent:
        s0f = s0f_ref[...].astype(BF16)
        s0b = s0b_ref[...].astype(BF16)
    for i in range(n_sub):
        k_f32 = k_ref[i * seq:(i + 1) * seq, :]
        kb = k_f32.astype(BF16)
        v = v_ref[i * seq:(i + 1) * seq, :]
        for t in range(seq // tq):
            rows = slice(i * seq + t * tq, i * seq + (t + 1) * tq)
            q = q_ref[rows, :]
            raw = lax.dot_general(q, kb, _NT, preferred_element_type=F32)
            p = (raw * d_ref[t * tq:(t + 1) * tq, :]).astype(BF16)
            o = jnp.dot(p, v, preferred_element_type=F32)
            if latent:
                pos = (lax.broadcasted_iota(jnp.int32, (tq, RET_VAL_DIM), 0) + t * tq).astype(F32)
                xi_f = jnp.exp((pos + 1.0) * lgf)
                xi_b = jnp.exp((seq - pos) * lgb)
                o = o + jnp.dot(q, s0f, preferred_element_type=F32) * xi_f
                o = o + jnp.dot(q, s0b, preferred_element_type=F32) * xi_b
            mu = jnp.mean(o, axis=-1, keepdims=True)
            oc = o - mu
            var = jnp.mean(oc * oc, axis=-1, keepdims=True)
            y = (oc * lax.rsqrt(var + EPS)) * rn_ref[...]
            o_ref[rows, :] = (y * sg_ref[rows, :]).astype(BF16)
        if not latent:
            j = lax.broadcasted_iota(jnp.int32, (seq, RET_KEY_DIM), 0).astype(F32)
            kz_f = (k_f32 * jnp.exp((seq - 1.0 - j) * lgf)).astype(BF16)
            kz_b = (k_f32 * jnp.exp(j * lgb)).astype(BF16)
            sf_ref[i] = lax.dot_general(kz_f, v, _TN, preferred_element_type=F32)
            sb_ref[i] = lax.dot_general(kz_b, v, _TN, preferred_element_type=F32)


def _ret_call(lgf, lgb, q_ret, k_ret, v_ret, sg, ret_norm, s0f, s0b, *, latent):
    if latent:
        seq, nb, row0, n_sub = DEC_SEQ, DEC_BATCH, T_P, 1
    else:
        seq, nb, row0, n_sub = SEQ, BATCH, 0, 4
    tq = 256
    rows = n_sub * seq
    rb = row0 // rows
    smem = pl.BlockSpec(memory_space=pltpu.SMEM)
    qk_spec = pl.BlockSpec((rows, RET_KEY_DIM), lambda h, b: (rb + b, h))
    v_spec = pl.BlockSpec((rows, RET_VAL_DIM), lambda h, b: (rb + b, h))
    o_spec = pl.BlockSpec((rows, RET_VAL_DIM), lambda h, b: (b, h))
    in_specs = [smem, smem, qk_spec, qk_spec, v_spec, v_spec,
                pl.BlockSpec((1, RET_VAL_DIM), lambda h, b: (0, h))]
    args = [lgf, lgb, q_ret, k_ret, v_ret, sg, ret_norm]
    o_shape = jax.ShapeDtypeStruct((nb * seq, RET_V), BF16)
    if latent:
        st_spec = pl.BlockSpec((None, None, None, RET_KEY_DIM, RET_VAL_DIM),
                               lambda h, b: (b, 0, h, 0, 0))
        in_specs += [st_spec, st_spec]
        args += [s0f, s0b]
        out_specs = o_spec
        out_shape = o_shape
    else:
        st_spec = pl.BlockSpec((n_sub, None, None, RET_KEY_DIM, RET_VAL_DIM),
                               lambda h, b: (b, 0, h, 0, 0))
        st_shape = jax.ShapeDtypeStruct((BATCH, 1, N_RET_HEADS, RET_KEY_DIM, RET_VAL_DIM), F32)
        out_specs = [o_spec, st_spec, st_spec]
        out_shape = [o_shape, st_shape, st_shape]
    est = _nbytes((seq, seq), F32) + 4 * n_sub * _nbytes((tq, seq), F32)
    est += 2 * (_nbytes((rows, RET_KEY_DIM), BF16) + _nbytes((rows, RET_KEY_DIM), F32))
    est += 2 * (2 * _nbytes((rows, RET_VAL_DIM), BF16) + _nbytes((rows, RET_VAL_DIM), F32))
    est += 8 * n_sub * _nbytes((RET_KEY_DIM, RET_VAL_DIM), F32)
    return pl.pallas_call(
        functools.partial(_ret_body, seq=seq, tq=tq, latent=latent, n_sub=n_sub),
        grid=(N_RET_HEADS, nb // n_sub),
        in_specs=in_specs,
        out_specs=out_specs,
        out_shape=out_shape,
        scratch_shapes=[pltpu.VMEM((seq, seq), F32)],
        compiler_params=_params(est, 2),
        name="ret_latent" if latent else "ret_prompt",
    )(*args)


def _final_body(x_ref, g_ref, yp_ref, ys_ref, *, n_p):
    x = x_ref[...]
    y = x * lax.rsqrt(jnp.mean(x * x, axis=-1, keepdims=True) + EPS) * g_ref[...]
    m = pl.program_id(0)

    @pl.when(m < n_p)
    def _():
        yp_ref[...] = y

    @pl.when(m >= n_p)
    def _():
        ys_ref[...] = y


def _final_call(x, g):
    tm = 512
    n_p = T_P // tm
    blk = (tm, D_MODEL)
    est = 6 * _nbytes(blk, F32) + 2 * _nbytes(blk, F32)
    return pl.pallas_call(
        functools.partial(_final_body, n_p=n_p),
        grid=(T // tm,),
        in_specs=[pl.BlockSpec(blk, lambda m: (m, 0)), pl.BlockSpec((1, D_MODEL), lambda m: (0, 0))],
        out_specs=[
            pl.BlockSpec(blk, lambda m: (jnp.minimum(m, n_p - 1), 0)),
            pl.BlockSpec(blk, lambda m: (jnp.maximum(m - n_p, 0), 0)),
        ],
        out_shape=[jax.ShapeDtypeStruct((T_P, D_MODEL), F32), jax.ShapeDtypeStruct((T_S, D_MODEL), F32)],
        compiler_params=_params(est, 1),
        name="final_norm",
    )(x, g)


def _rope_tables():
    rows = DEC_SEQ // GRID_W
    row = jnp.repeat(jnp.arange(rows, dtype=F32), GRID_W)
    col = jnp.tile(jnp.arange(GRID_W, dtype=F32), rows)
    quarter = HEAD_DIM // 4
    inv_freq = ROPE_THETA ** (-jnp.arange(quarter, dtype=F32) / quarter)
    ang_r = row[:, None] * inv_freq[None, :]
    ang_c = col[:, None] * inv_freq[None, :]
    cr, sr, cc, sc = jnp.cos(ang_r), jnp.sin(ang_r), jnp.cos(ang_c), jnp.sin(ang_c)
    zero = jnp.zeros_like(sr)
    cos = jnp.concatenate([cr, cr, cc, cc], axis=-1)
    s_lo = jnp.concatenate([-sr, zero, -sc, zero], axis=-1)
    s_hi = jnp.concatenate([zero, sr, zero, sc], axis=-1)
    ident = jnp.zeros((DEC_SEQ, HEAD_DIM), F32)
    return (jnp.concatenate([ident + 1.0, cos], axis=0),
            jnp.concatenate([ident, s_lo], axis=0),
            jnp.concatenate([ident, s_hi], axis=0))


def kernel(x_prompt, x_sample, cache_attn_k, cache_attn_v, state_ret_fwd, state_ret_bwd, c, c_ctx,
           norm_attn, norm_ffn, w_mod, b_mod, w_in, q_norm, k_norm, ret_decay_fwd, ret_decay_bwd,
           ret_norm, w_branch_attn, w_branch_ret, w_out, w_ffn_gate, w_ffn_up, w_ffn_down, final_norm):
    xp = x_prompt.reshape(T_P, D_MODEL)
    xs = x_sample.reshape(T_S, D_MODEL)
    ctx_k = cache_attn_k.reshape(DEC_BATCH, PAST_LEN, ATTN_KV)
    ctx_v = cache_attn_v.reshape(DEC_BATCH, PAST_LEN, ATTN_KV)
    lgf = jax.nn.log_sigmoid(ret_decay_fwd[0].astype(F32))
    lgb = jax.nn.log_sigmoid(ret_decay_bwd[0].astype(F32))
    cos, s_lo, s_hi = _rope_tables()

    c_all = jnp.concatenate(
        [c_ctx[None, :], c, jnp.zeros((MOD_ROWS - 1 - DEC_BATCH, D_MODEL), F32)], axis=0)
    mod = _mod_call(c_all, w_mod[0], b_mod)
    mod3 = mod.reshape(MOD_ROWS, 1, 6 * D_MODEL)
    h = _prenorm_call([xp, xs], norm_attn, mod3, 0, 1, "prenorm_attn")

    w_in0 = w_in[0]
    tm = 1024
    tbl = pl.BlockSpec((DEC_SEQ, HEAD_DIM), lambda n, m: (jnp.where(m < T_P // tm, 0, 1), 0))
    hd = pl.BlockSpec((1, HEAD_DIM), lambda n, m: (0, 0))
    rope_extras = [(cos, tbl), (s_lo, tbl), (s_hi, tbl)]

    def proj(col0, n_cols, tn, epilogue, extras, outs, name):
        return _matmul_call([h], [(w_in0, col0)], [(0, 0)], n_cols, tm, tn, epilogue, extras, outs, name)

    def full(n_cols, dtype):
        return jax.ShapeDtypeStruct((T, n_cols), dtype)

    (q_att,) = proj(COL_Q, ATTN_Q, 1024, functools.partial(_epi_q, tn=1024),
                    [(q_norm, hd)] + rope_extras,
                    [(full(ATTN_Q, BF16), _tile_spec(tm, 1024))], "proj_q")
    prompt_kv = jax.ShapeDtypeStruct((T_P + tm, ATTN_KV), F32)
    k_att, k_new = proj(COL_K, ATTN_KV, ATTN_KV, functools.partial(_epi_k, tn=ATTN_KV),
                        [(k_norm, hd)] + rope_extras,
                        [(full(ATTN_KV, BF16), _tile_spec(tm, ATTN_KV)),
                         (prompt_kv, _prompt_tile_spec(tm, ATTN_KV))], "proj_k")
    v_att, v_new = proj(COL_V, ATTN_KV, ATTN_KV, _epi_v, [],
                        [(full(ATTN_KV, BF16), _tile_spec(tm, ATTN_KV)),
                         (prompt_kv, _prompt_tile_spec(tm, ATTN_KV))], "proj_v")
    (q_ret,) = proj(COL_QR, RET_QK, 1024, _epi_cast, [],
                    [(full(RET_QK, BF16), _tile_spec(tm, 1024))], "proj_qr")
    (k_ret,) = proj(COL_KR, RET_QK, 1024, functools.partial(_epi_scale, scale=RET_KEY_DIM ** -0.5), [],
                    [(full(RET_QK, F32), _tile_spec(tm, 1024))], "proj_kr")
    (v_ret,) = proj(COL_VR, RET_V, 1024, _epi_cast, [],
                    [(full(RET_V, BF16), _tile_spec(tm, 1024))], "proj_vr")
    (sg,) = proj(COL_GR, RET_V, 1024, _epi_silu, [],
                 [(full(RET_V, F32), _tile_spec(tm, 1024))], "proj_gr")
    (gates,) = proj(COL_GATES, 2 * D_MODEL, 1024, _epi_sigmoid, [],
                    [(full(2 * D_MODEL, F32), _tile_spec(tm, 1024))], "proj_gates")

    o_a_p = _attn_call(q_att, k_att, v_att, None, None, latent=False)
    o_a_s = _attn_call(q_att, k_att, v_att, ctx_k, ctx_v, latent=True)
    o_r_p, new_sf, new_sb = _ret_call(lgf, lgb, q_ret, k_ret, v_ret, sg, ret_norm,
                                      None, None, latent=False)
    o_r_s = _ret_call(lgf, lgb, q_ret, k_ret, v_ret, sg, ret_norm,
                      state_ret_fwd, state_ret_bwd, latent=True)

    tm2, tn2 = 512, 1024
    (merged,) = _matmul_call(
        [(o_a_p, o_a_s), (o_r_p, o_r_s)], [(w_branch_attn[0], 0), (w_branch_ret[0], 0)],
        [(0, 0), (1, 1)], D_MODEL, tm2, tn2, _epi_merge,
        [(gates, _tile_spec(tm2, tn2)), (gates, _tile_spec(tm2, tn2, D_MODEL // tn2))],
        [(full(D_MODEL, BF16), _tile_spec(tm2, tn2))], "merge")

    tm3, tn3 = 1024, 1024
    n_p3 = T_P // tm3
    (x1,) = _matmul_call(
        [merged], [(w_out[0], 0)], [(0, 0)], D_MODEL, tm3, tn3,
        functools.partial(_epi_residual2, tm=tm3),
        [(xp, pl.BlockSpec((tm3, tn3), lambda n, m: (jnp.minimum(m, n_p3 - 1), n))),
         (xs, pl.BlockSpec((tm3, tn3), lambda n, m: (jnp.maximum(m - n_p3, 0), n))),
         (mod3, _mod_spec(tm3, tn3, lambda n: 2 * (D_MODEL // tn3) + n, 2))],
        [(full(D_MODEL, F32), _tile_spec(tm3, tn3))], "out_proj")

    h2 = _prenorm_call([x1], norm_ffn, mod3, 3, 4, "prenorm_ffn")

    (act,) = _matmul_call(
        [h2], [(w_ffn_gate[0], 0), (w_ffn_up[0], 0)], [(0, 0), (0, 1)],
        D_FF, 2048, 512, _epi_swiglu, [],
        [(full(D_FF, BF16), _tile_spec(2048, 512))], "ffn_up", kc=512)

    tm4, tn4 = 512, 1024
    (y_pre,) = _matmul_call(
        [act], [(w_ffn_down[0], 0)], [(0, 0)], D_MODEL, tm4, tn4, _epi_residual,
        [(x1, _tile_spec(tm4, tn4)),
         (mod3, _mod_spec(tm4, tn4, lambda n: 5 * (D_MODEL // tn4) + n, 2))],
        [(full(D_MODEL, F32), _tile_spec(tm4, tn4))], "ffn_down", kc=512)

    y_p, y_s = _final_call(y_pre, final_norm[None, :])

    return (y_p.reshape(BATCH, SEQ, D_MODEL),
            y_s.reshape(DEC_BATCH, DEC_SEQ, D_MODEL),
            k_new[:T_P].reshape(BATCH, 1, SEQ, N_KV_HEADS, HEAD_DIM),
            v_new[:T_P].reshape(BATCH, 1, SEQ, N_KV_HEADS, HEAD_DIM),
            new_sf, new_sb)
```

```python
import functools

import jax
import jax.numpy as jnp
from jax import lax
from jax.experimental import pallas as pl
from jax.experimental.pallas import tpu as pltpu

D_MODEL = 2048
BATCH = 16
SEQ = 256
DEC_BATCH = 8
DEC_SEQ = 1024
PAST_LEN = 512
GRID_W = 64
N_HEADS = 16
N_KV_HEADS = 4
HEAD_DIM = 128
ROPE_THETA = 10000.0
N_RET_HEADS = 8
RET_KEY_DIM = 128
RET_VAL_DIM = 256
D_FF = 5632
EPS = 1e-6

ATTN_Q = N_HEADS * HEAD_DIM
ATTN_KV = N_KV_HEADS * HEAD_DIM
RET_QK = N_RET_HEADS * RET_KEY_DIM
RET_V = N_RET_HEADS * RET_VAL_DIM
GQA_GROUP = N_HEADS // N_KV_HEADS

COL_Q = 0
COL_K = COL_Q + ATTN_Q
COL_V = COL_K + ATTN_KV
COL_QR = COL_V + ATTN_KV
COL_KR = COL_QR + RET_QK
COL_VR = COL_KR + RET_QK
COL_GR = COL_VR + RET_V
COL_GATES = COL_GR + RET_V

T_P = BATCH * SEQ
T_S = DEC_BATCH * DEC_SEQ
T = T_P + T_S
MOD_ROWS = 16

V7X_VMEM_BYTES = 64 * 1024 * 1024
VMEM_CAP = V7X_VMEM_BYTES - 6 * 1024 * 1024

BF16 = jnp.bfloat16
F32 = jnp.float32


def _nbytes(shape, dtype):
    n = 1
    for s in shape:
        if s is not None:
            n *= s
    return n * jnp.dtype(dtype).itemsize


def _params(vmem_estimate, n_grid):
    limit = min(VMEM_CAP, int(vmem_estimate) + 8 * 1024 * 1024)
    return pltpu.CompilerParams(
        dimension_semantics=("arbitrary",) * n_grid, vmem_limit_bytes=limit)


def _mod_row(m, tm):
    n_p = T_P // tm
    return jnp.where(m < n_p, 0, 1 + (m - n_p) // (DEC_SEQ // tm))


def _mod_spec(tm, width, col_block, grid_rank):
    if grid_rank == 1:
        return pl.BlockSpec((None, 1, width), lambda m: (_mod_row(m, tm), 0, col_block(0)))
    return pl.BlockSpec((None, 1, width), lambda n, m: (_mod_row(m, tm), 0, col_block(n)))


def _mod_body(c_ref, w_ref, b_ref, o_ref):
    a = jax.nn.silu(c_ref[...]).astype(BF16)
    w = w_ref[...].astype(BF16)
    o_ref[...] = jnp.dot(a, w, preferred_element_type=F32) + b_ref[...]


def _mod_call(c_all, w_mod, b_mod):
    tn = 1024
    n_out = 6 * D_MODEL
    est = 2 * _nbytes((D_MODEL, tn), F32) + _nbytes((D_MODEL, tn), BF16)
    return pl.pallas_call(
        _mod_body,
        grid=(n_out // tn,),
        in_specs=[
            pl.BlockSpec((MOD_ROWS, D_MODEL), lambda n: (0, 0)),
            pl.BlockSpec((D_MODEL, tn), lambda n: (0, n)),
            pl.BlockSpec((1, tn), lambda n: (0, n)),
        ],
        out_specs=pl.BlockSpec((MOD_ROWS, tn), lambda n: (0, n)),
        out_shape=jax.ShapeDtypeStruct((MOD_ROWS, n_out), F32),
        compiler_params=_params(est, 1),
        name="mod_table",
    )(c_all, w_mod, b_mod)


def _modulated_norm(x, g, sc, sh):
    y = x * lax.rsqrt(jnp.mean(x * x, axis=-1, keepdims=True) + EPS)
    return (y * g) * (1.0 + sc) + sh


def _prenorm2_body(xp_ref, xs_ref, g_ref, sc_ref, sh_ref, o_ref, *, n_p):
    m = pl.program_id(0)

    @pl.when(m < n_p)
    def _():
        o_ref[...] = _modulated_norm(xp_ref[...], g_ref[...], sc_ref[...], sh_ref[...]).astype(BF16)

    @pl.when(m >= n_p)
    def _():
        o_ref[...] = _modulated_norm(xs_ref[...], g_ref[...], sc_ref[...], sh_ref[...]).astype(BF16)


def _prenorm1_body(x_ref, g_ref, sc_ref, sh_ref, o_ref):
    o_ref[...] = _modulated_norm(x_ref[...], g_ref[...], sc_ref[...], sh_ref[...]).astype(BF16)


def _prenorm_call(xs, norm_w, mod3, sh_blk, sc_blk, name):
    tm = 512 if len(xs) == 2 else 1024
    n_p = T_P // tm
    blk = (tm, D_MODEL)
    common = [
        pl.BlockSpec((1, D_MODEL), lambda m: (0, 0)),
        _mod_spec(tm, D_MODEL, lambda n: sc_blk, 1),
        _mod_spec(tm, D_MODEL, lambda n: sh_blk, 1),
    ]
    if len(xs) == 2:
        in_specs = [
            pl.BlockSpec(blk, lambda m: (jnp.minimum(m, n_p - 1), 0)),
            pl.BlockSpec(blk, lambda m: (jnp.maximum(m - n_p, 0), 0)),
        ] + common
        body = functools.partial(_prenorm2_body, n_p=n_p)
    else:
        in_specs = [pl.BlockSpec(blk, lambda m: (m, 0))] + common
        body = _prenorm1_body
    est = 2 * len(xs) * _nbytes(blk, F32) + 2 * _nbytes(blk, BF16) + 2 * _nbytes(blk, F32)
    return pl.pallas_call(
        body,
        grid=(T // tm,),
        in_specs=in_specs,
        out_specs=pl.BlockSpec(blk, lambda m: (m, 0)),
        out_shape=jax.ShapeDtypeStruct((T, D_MODEL), BF16),
        compiler_params=_params(est, 1),
        name=name,
    )(*xs, norm_w, mod3, mod3)


def _matmul_body(*refs, lhs_split, n_rhs, col0s, pairs, n_extra, n_out, epilogue,
                 tm, tn, kc, row_chunk, stream):
    n_lhs_refs = sum(2 if s else 1 for s in lhs_split)
    lhs_refs = refs[:n_lhs_refs]
    pos = n_lhs_refs
    rhs = refs[pos:pos + n_rhs]
    pos += n_rhs
    extras = refs[pos:pos + n_extra]
    pos += n_extra
    outs = refs[pos:pos + n_out]
    pos += n_out
    wb = refs[pos:pos + n_rhs]
    n, m = pl.program_id(0), pl.program_id(1)
    is_prompt = m < T_P // tm

    def lhs_rows(i, rows):
        first = sum(2 if s else 1 for s in lhs_split[:i])
        if lhs_split[i]:
            return jnp.where(is_prompt, lhs_refs[first][rows, :], lhs_refs[first + 1][rows, :])
        return lhs_refs[first][rows, :]

    def compute(weight):
        for r in range(tm // row_chunk):
            rows = slice(r * row_chunk, (r + 1) * row_chunk)
            zs = [jnp.dot(lhs_rows(i, rows), weight(j), preferred_element_type=F32) for i, j in pairs]
            epilogue(zs, extras, outs, rows)

    if not stream:
        @pl.when(m == 0)
        def _():
            for w_ref, wb_ref in zip(rhs, wb):
                wb_ref[...] = w_ref[...].astype(BF16)

        compute(lambda j: wb[j][...])
        return

    stage = refs[pos + n_rhs:pos + 2 * n_rhs]
    sems = refs[pos + 2 * n_rhs:pos + 3 * n_rhs]
    n_tiles = pl.num_programs(0)
    n_chunks = wb[0].shape[1] // kc

    def chunk_copy(j, tile, chunk, slot):
        col = pl.multiple_of(col0s[j] + tile * tn, tn)
        row = pl.multiple_of(chunk * kc, kc)
        return pltpu.make_async_copy(
            rhs[j].at[pl.ds(row, kc), pl.ds(col, tn)], stage[j].at[slot], sems[j].at[slot])

    nxt = jnp.minimum(n + 1, n_tiles - 1)
    chunk = jnp.minimum(m, n_chunks - 1)
    first = jnp.logical_and(n == 0, m == 0)

    def start_prefetch():
        for j in range(n_rhs):
            chunk_copy(j, nxt, chunk, 0).start()

    @pl.when(first)
    def _():
        for j in range(n_rhs):
            chunk_copy(j, 0, 0, 0).start()
            for c in range(n_chunks):
                if c + 1 < n_chunks:
                    chunk_copy(j, 0, c + 1, (c + 1) % 2).start()
                chunk_copy(j, 0, c, c % 2).wait()
                wb[j][0, c * kc:(c + 1) * kc, :] = stage[j][c % 2].astype(BF16)
        start_prefetch()

    pl.when(jnp.logical_not(first))(start_prefetch)

    for cur in range(2):
        @pl.when(n % 2 == cur)
        def _(cur=cur):
            compute(lambda j: wb[j][cur])
            for j in range(n_rhs):
                chunk_copy(j, nxt, chunk, 0).wait()
                wb[j][1 - cur, pl.ds(pl.multiple_of(chunk * kc, kc), kc), :] = stage[j][0].astype(BF16)


def _matmul_call(lhs_list, rhs_list, pairs, n_cols, tm, tn, epilogue, extras, outs, name,
                 row_chunk=256, stream=False, kc=256):
    lhs_split = tuple(isinstance(a, tuple) for a in lhs_list)
    k_dim = (lhs_list[0][0] if lhs_split[0] else lhs_list[0]).shape[1]
    n_p = T_P // tm
    grid = (n_cols // tn, T // tm)
    in_specs, lhs_args = [], []
    for a in lhs_list:
        if isinstance(a, tuple):
            in_specs.append(pl.BlockSpec((tm, k_dim), lambda n, m: (jnp.minimum(m, n_p - 1), 0)))
            in_specs.append(pl.BlockSpec((tm, k_dim), lambda n, m: (jnp.maximum(m - n_p, 0), 0)))
            lhs_args += list(a)
        else:
            in_specs.append(pl.BlockSpec((tm, k_dim), lambda n, m: (m, 0)))
            lhs_args.append(a)
    est = 2 * len(lhs_args) * _nbytes((tm, k_dim), BF16)
    if stream:
        assert k_dim % kc == 0 and k_dim // kc <= grid[1], (k_dim, kc, grid)
        in_specs += [pl.BlockSpec(memory_space=pl.ANY) for _ in rhs_list]
        est += len(rhs_list) * 2 * (_nbytes((k_dim, tn), BF16) + _nbytes((kc, tn), F32))
        scratch = [pltpu.VMEM((2, k_dim, tn), BF16) for _ in rhs_list]
        scratch += [pltpu.VMEM((2, kc, tn), F32) for _ in rhs_list]
        scratch += [pltpu.SemaphoreType.DMA((2,)) for _ in rhs_list]
    else:
        for _, col0 in rhs_list:
            in_specs.append(pl.BlockSpec((k_dim, tn), lambda n, m, c=col0 // tn: (0, c + n)))
        est += len(rhs_list) * (2 * _nbytes((k_dim, tn), F32) + _nbytes((k_dim, tn), BF16))
        scratch = [pltpu.VMEM((k_dim, tn), BF16) for _ in rhs_list]
    in_specs += [spec for _, spec in extras]
    for arr, spec in list(extras) + list(outs):
        est += 2 * _nbytes(spec.block_shape, arr.dtype)
    est += 3 * len(pairs) * _nbytes((row_chunk, tn), F32)
    body = functools.partial(
        _matmul_body, lhs_split=lhs_split, n_rhs=len(rhs_list),
        col0s=tuple(c for _, c in rhs_list), pairs=tuple(pairs),
        n_extra=len(extras), n_out=len(outs), epilogue=epilogue,
        tm=tm, tn=tn, kc=kc, row_chunk=row_chunk, stream=stream)
    res = pl.pallas_call(
        body,
        grid=grid,
        in_specs=in_specs,
        out_specs=[spec for _, spec in outs],
        out_shape=[s for s, _ in outs],
        scratch_shapes=scratch,
        compiler_params=_params(est, 2),
        name=name,
    )(*lhs_args, *[w for w, _ in rhs_list], *[a for a, _ in extras])
    return res


def _tile_spec(tm, tn, col_block0=0):
    return pl.BlockSpec((tm, tn), lambda n, m, c=col_block0: (m, c + n))


def _prompt_tile_spec(tm, tn):
    n_p = T_P // tm
    return pl.BlockSpec((tm, tn), lambda n, m: (jnp.minimum(m, n_p), n))


def _head_rms(x, g):
    return x * lax.rsqrt(jnp.mean(x * x, axis=-1, keepdims=True) + EPS) * g


def _rope(y, c, s_lo, s_hi):
    return y * c + pltpu.roll(y, HEAD_DIM - 32, axis=1) * s_lo + pltpu.roll(y, 32, axis=1) * s_hi


def _epi_q(zs, extras, outs, rows, *, tn):
    g_ref, c_ref, slo_ref, shi_ref = extras
    (q_ref,) = outs
    (z,) = zs
    for h in range(tn // HEAD_DIM):
        sl = slice(h * HEAD_DIM, (h + 1) * HEAD_DIM)
        y = _head_rms(z[:, sl], g_ref[...])
        q_ref[rows, sl] = _rope(y, c_ref[rows, :], slo_ref[rows, :], shi_ref[rows, :]).astype(BF16)


def _epi_k(zs, extras, outs, rows, *, tn):
    g_ref, c_ref, slo_ref, shi_ref = extras
    k_att_ref, k_new_ref = outs
    (z,) = zs
    for h in range(tn // HEAD_DIM):
        sl = slice(h * HEAD_DIM, (h + 1) * HEAD_DIM)
        y = _head_rms(z[:, sl], g_ref[...])
        k_new_ref[rows, sl] = y
        k_att_ref[rows, sl] = _rope(y, c_ref[rows, :], slo_ref[rows, :], shi_ref[rows, :]).astype(BF16)


def _epi_v(zs, extras, outs, rows):
    v_att_ref, v_new_ref = outs
    v_att_ref[rows, :] = zs[0].astype(BF16)
    v_new_ref[rows, :] = zs[0]


def _epi_cast(zs, extras, outs, rows):
    outs[0][rows, :] = zs[0].astype(outs[0].dtype)


def _epi_scale(zs, extras, outs, rows, *, scale):
    outs[0][rows, :] = zs[0] * scale


def _epi_silu(zs, extras, outs, rows):
    outs[0][rows, :] = jax.nn.silu(zs[0])


def _epi_sigmoid(zs, extras, outs, rows):
    outs[0][rows, :] = jax.nn.sigmoid(zs[0])


def _epi_merge(zs, extras, outs, rows):
    ga_ref, gr_ref = extras
    outs[0][rows, :] = (ga_ref[rows, :] * zs[0] + gr_ref[rows, :] * zs[1]).astype(BF16)


def _epi_residual2(zs, extras, outs, rows, *, tm):
    xp_ref, xs_ref, gate_ref = extras
    x = jnp.where(pl.program_id(1) < T_P // tm, xp_ref[rows, :], xs_ref[rows, :])
    outs[0][rows, :] = x + gate_ref[...] * zs[0]


def _epi_swiglu(zs, extras, outs, rows):
    outs[0][rows, :] = (jax.nn.silu(zs[0]) * zs[1]).astype(BF16)


def _epi_residual(zs, extras, outs, rows):
    x_ref, gate_ref = extras
    outs[0][rows, :] = x_ref[rows, :] + gate_ref[...] * zs[0]


_NT = (((1,), (1,)), ((), ()))


_SOFTMAX_EXP2_SCALE = HEAD_DIM ** -0.5 * 1.4426950408889634


def _attn_body(*refs, has_ctx, seq, tq, n_sub):
    if has_ctx:
        q_ref, k_ref, v_ref, ck_ref, cv_ref, o_ref = refs
        ck = ck_ref[...].astype(BF16)
        cv = cv_ref[...].astype(BF16)
    else:
        q_ref, k_ref, v_ref, o_ref = refs
    chain_rows = tq
    for i, c in [(i, c) for i in range(n_sub) for c in range(tq // chain_rows)]:
        k = k_ref[i * seq:(i + 1) * seq, :]
        v = v_ref[i * seq:(i + 1) * seq, :]
        rows = slice(i * tq + c * chain_rows, i * tq + (c + 1) * chain_rows)
        for h in range(GQA_GROUP):
            sl = slice(h * HEAD_DIM, (h + 1) * HEAD_DIM)
            q = q_ref[rows, sl]
            s_new = lax.dot_general(q, k, _NT, preferred_element_type=F32)
            mx = jnp.max(s_new, axis=-1, keepdims=True)
            if has_ctx:
                s_ctx = lax.dot_general(q, ck, _NT, preferred_element_type=F32)
                mx = jnp.maximum(mx, jnp.max(s_ctx, axis=-1, keepdims=True))
                p_ctx = jnp.exp2((s_ctx - mx) * _SOFTMAX_EXP2_SCALE)
                p_new = jnp.exp2((s_new - mx) * _SOFTMAX_EXP2_SCALE)
                den = jnp.sum(p_ctx, axis=-1, keepdims=True) + jnp.sum(p_new, axis=-1, keepdims=True)
                o = jnp.dot(p_ctx.astype(BF16), cv, preferred_element_type=F32)
                o = o + jnp.dot(p_new.astype(BF16), v, preferred_element_type=F32)
            else:
                p_new = jnp.exp2((s_new - mx) * _SOFTMAX_EXP2_SCALE)
                den = jnp.sum(p_new, axis=-1, keepdims=True)
                o = jnp.dot(p_new.astype(BF16), v, preferred_element_type=F32)
            o_ref[rows, sl] = (o * (1.0 / den)).astype(BF16)


def _attn_call(q_att, k_att, v_att, ctx_k, ctx_v, *, latent):
    gw = GQA_GROUP * HEAD_DIM
    if latent:
        seq, nb, tq, n_sub = DEC_SEQ, DEC_BATCH, 512, 1
        row0 = T_P
    else:
        seq, nb, tq, n_sub = SEQ, BATCH, SEQ, 4
        row0 = 0
    nqt = seq // tq
    q_rows, kv_rows = n_sub * tq, n_sub * seq
    q_spec = pl.BlockSpec((q_rows, gw), lambda b, g, t: (row0 // q_rows + b * nqt + t, g))
    o_spec = pl.BlockSpec((q_rows, gw), lambda b, g, t: (b * nqt + t, g))
    kv_spec = pl.BlockSpec((kv_rows, HEAD_DIM), lambda b, g, t: (row0 // kv_rows + b, g))
    in_specs = [q_spec, kv_spec, kv_spec]
    args = [q_att, k_att, v_att]
    n_keys = seq
    if latent:
        c_spec = pl.BlockSpec((None, PAST_LEN, HEAD_DIM), lambda b, g, t: (b, 0, g))
        in_specs += [c_spec, c_spec]
        args += [ctx_k, ctx_v]
        n_keys += PAST_LEN
    est = 4 * _nbytes((q_rows, gw), BF16) + 4 * _nbytes((kv_rows, HEAD_DIM), BF16)
    est += 4 * _nbytes((PAST_LEN, HEAD_DIM), F32)
    est += 3 * GQA_GROUP * n_sub * _nbytes((tq, n_keys), F32)
    return pl.pallas_call(
        functools.partial(_attn_body, has_ctx=latent, seq=seq, tq=tq, n_sub=n_sub),
        grid=(nb // n_sub, N_KV_HEADS, nqt),
        in_specs=in_specs,
        out_specs=o_spec,
        out_shape=jax.ShapeDtypeStruct((nb * seq, ATTN_Q), BF16),
        compiler_params=_params(est, 3),
        name="attn_latent" if latent else "attn_prompt",
    )(*args)


_TN = (((0,), (0,)), ((), ()))


def _ret_body(*refs, seq, tq, latent, n_sub):
    if latent:
        (lgf_ref, lgb_ref, q_ref, k_ref, v_ref, sg_ref, rn_ref, s0f_ref, s0b_ref,
         o_ref, d_ref) = refs
    else:
        (lgf_ref, lgb_ref, q_ref, k_ref, v_ref, sg_ref, rn_ref,
         o_ref, sf_ref, sb_ref, d_ref) = refs
    h = pl.program_id(0)
    lgf = lgf_ref[h]
    lgb = lgb_ref[h]

    @pl.when(pl.program_id(1) == 0)
    def _():
        for t in range(seq // tq):
            i = lax.broadcasted_iota(jnp.int32, (tq, seq), 0) + t * tq
            j = lax.broadcasted_iota(jnp.int32, (tq, seq), 1)
            diff = (i - j).astype(F32)
            arg = jnp.where(diff >= 0, diff * lgf, -diff * lgb)
            d_ref[t * tq:(t + 1) * tq, :] = jnp.exp(arg) * jnp.where(diff == 0, 2.0, 1.0)

    if latent:
        s0f = s0f_ref[...].astype(BF16)
        s0b = s0b_ref[...].astype(BF16)
    for i in range(n_sub):
        k_f32 = k_ref[i * seq:(i + 1) * seq, :]
        kb = k_f32.astype(BF16)
        v = v_ref[i * seq:(i + 1) * seq, :]
        for t in range(seq // tq):
            rows = slice(i * seq + t * tq, i * seq + (t + 1) * tq)
            q = q_ref[rows, :]
            raw = lax.dot_general(q, kb, _NT, preferred_element_type=F32)
            p = (raw * d_ref[t * tq:(t + 1) * tq, :]).astype(BF16)
            o = jnp.dot(p, v, preferred_element_type=F32)
            if latent:
                pos = (lax.broadcasted_iota(jnp.int32, (tq, RET_VAL_DIM), 0) + t * tq).astype(F32)
                xi_f = jnp.exp((pos + 1.0) * lgf)
                xi_b = jnp.exp((seq - pos) * lgb)
                o = o + jnp.dot(q, s0f, preferred_element_type=F32) * xi_f
                o = o + jnp.dot(q, s0b, preferred_element_type=F32) * xi_b
            mu = jnp.mean(o, axis=-1, keepdims=True)
            oc = o - mu
            var = jnp.mean(oc * oc, axis=-1, keepdims=True)
            y = (oc * lax.rsqrt(var + EPS)) * rn_ref[...]
            o_ref[rows, :] = (y * sg_ref[rows, :]).astype(BF16)
        if not latent:
            j = lax.broadcasted_iota(jnp.int32, (seq, RET_KEY_DIM), 0).astype(F32)
            kz_f = (k_f32 * jnp.exp((seq - 1.0 - j) * lgf)).astype(BF16)
            kz_b = (k_f32 * jnp.exp(j * lgb)).astype(BF16)
            sf_ref[i] = lax.dot_general(kz_f, v, _TN, preferred_element_type=F32)
            sb_ref[i] = lax.dot_general(kz_b, v, _TN, preferred_element_type=F32)


def _ret_call(lgf, lgb, q_ret, k_ret, v_ret, sg, ret_norm, s0f, s0b, *, latent):
    if latent:
        seq, nb, row0, n_sub = DEC_SEQ, DEC_BATCH, T_P, 1
    else:
        seq, nb, row0, n_sub = SEQ, BATCH, 0, 4
    tq = 256
    rows = n_sub * seq
    rb = row0 // rows
    smem = pl.BlockSpec(memory_space=pltpu.SMEM)
    qk_spec = pl.BlockSpec((rows, RET_KEY_DIM), lambda h, b: (rb + b, h))
    v_spec = pl.BlockSpec((rows, RET_VAL_DIM), lambda h, b: (rb + b, h))
    o_spec = pl.BlockSpec((rows, RET_VAL_DIM), lambda h, b: (b, h))
    in_specs = [smem, smem, qk_spec, qk_spec, v_spec, v_spec,
                pl.BlockSpec((1, RET_VAL_DIM), lambda h, b: (0, h))]
    args = [lgf, lgb, q_ret, k_ret, v_ret, sg, ret_norm]
    o_shape = jax.ShapeDtypeStruct((nb * seq, RET_V), BF16)
    if latent:
        st_spec = pl.BlockSpec((None, None, None, RET_KEY_DIM, RET_VAL_DIM),
                               lambda h, b: (b, 0, h, 0, 0))
        in_specs += [st_spec, st_spec]
        args += [s0f, s0b]
        out_specs = o_spec
        out_shape = o_shape
    else:
        st_spec = pl.BlockSpec((n_sub, None, None, RET_KEY_DIM, RET_VAL_DIM),
                               lambda h, b: (b, 0, h, 0, 0))
        st_shape = jax.ShapeDtypeStruct((BATCH, 1, N_RET_HEADS, RET_KEY_DIM, RET_VAL_DIM), F32)
        out_specs = [o_spec, st_spec, st_spec]
        out_shape = [o_shape, st_shape, st_shape]
    est = _nbytes((seq, seq), F32) + 4 * n_sub * _nbytes((tq, seq), F32)
    est += 2 * (_nbytes((rows, RET_KEY_DIM), BF16) + _nbytes((rows, RET_KEY_DIM), F32))
    est += 2 * (2 * _nbytes((rows, RET_VAL_DIM), BF16) + _nbytes((rows, RET_VAL_DIM), F32))
    est += 8 * n_sub * _nbytes((RET_KEY_DIM, RET_VAL_DIM), F32)
    return pl.pallas_call(
        functools.partial(_ret_body, seq=seq, tq=tq, latent=latent, n_sub=n_sub),
        grid=(N_RET_HEADS, nb // n_sub),
        in_specs=in_specs,
        out_specs=out_specs,
        out_shape=out_shape,
        scratch_shapes=[pltpu.VMEM((seq, seq), F32)],
        compiler_params=_params(est, 2),
        name="ret_latent" if latent else "ret_prompt",
    )(*args)


def _final_body(x_ref, g_ref, yp_ref, ys_ref, *, n_p):
    x = x_ref[...]
    y = x * lax.rsqrt(jnp.mean(x * x, axis=-1, keepdims=True) + EPS) * g_ref[...]
    m = pl.program_id(0)

    @pl.when(m < n_p)
    def _():
        yp_ref[...] = y

    @pl.when(m >= n_p)
    def _():
        ys_ref[...] = y


def _final_call(x, g):
    tm = 512
    n_p = T_P // tm
    blk = (tm, D_MODEL)
    est = 6 * _nbytes(blk, F32) + 2 * _nbytes(blk, F32)
    return pl.pallas_call(
        functools.partial(_final_body, n_p=n_p),
        grid=(T // tm,),
        in_specs=[pl.BlockSpec(blk, lambda m: (m, 0)), pl.BlockSpec((1, D_MODEL), lambda m: (0, 0))],
        out_specs=[
            pl.BlockSpec(blk, lambda m: (jnp.minimum(m, n_p - 1), 0)),
            pl.BlockSpec(blk, lambda m: (jnp.maximum(m - n_p, 0), 0)),
        ],
        out_shape=[jax.ShapeDtypeStruct((T_P, D_MODEL), F32), jax.ShapeDtypeStruct((T_S, D_MODEL), F32)],
        compiler_params=_params(est, 1),
        name="final_norm",
    )(x, g)


def _rope_tables():
    rows = DEC_SEQ // GRID_W
    row = jnp.repeat(jnp.arange(rows, dtype=F32), GRID_W)
    col = jnp.tile(jnp.arange(GRID_W, dtype=F32), rows)
    quarter = HEAD_DIM // 4
    inv_freq = ROPE_THETA ** (-jnp.arange(quarter, dtype=F32) / quarter)
    ang_r = row[:, None] * inv_freq[None, :]
    ang_c = col[:, None] * inv_freq[None, :]
    cr, sr, cc, sc = jnp.cos(ang_r), jnp.sin(ang_r), jnp.cos(ang_c), jnp.sin(ang_c)
    zero = jnp.zeros_like(sr)
    cos = jnp.concatenate([cr, cr, cc, cc], axis=-1)
    s_lo = jnp.concatenate([-sr, zero, -sc, zero], axis=-1)
    s_hi = jnp.concatenate([zero, sr, zero, sc], axis=-1)
    ident = jnp.zeros((DEC_SEQ, HEAD_DIM), F32)
    return (jnp.concatenate([ident + 1.0, cos], axis=0),
            jnp.concatenate([ident, s_lo], axis=0),
            jnp.concatenate([ident, s_hi], axis=0))


def kernel(x_prompt, x_sample, cache_attn_k, cache_attn_v, state_ret_fwd, state_ret_bwd, c, c_ctx,
           norm_attn, norm_ffn, w_mod, b_mod, w_in, q_norm, k_norm, ret_decay_fwd, ret_decay_bwd,
           ret_norm, w_branch_attn, w_branch_ret, w_out, w_ffn_gate, w_ffn_up, w_ffn_down, final_norm):
    xp = x_prompt.reshape(T_P, D_MODEL)
    xs = x_sample.reshape(T_S, D_MODEL)
    ctx_k = cache_attn_k.reshape(DEC_BATCH, PAST_LEN, ATTN_KV)
    ctx_v = cache_attn_v.reshape(DEC_BATCH, PAST_LEN, ATTN_KV)
    lgf = jax.nn.log_sigmoid(ret_decay_fwd[0].astype(F32))
    lgb = jax.nn.log_sigmoid(ret_decay_bwd[0].astype(F32))
    cos, s_lo, s_hi = _rope_tables()

    c_all = jnp.concatenate(
        [c_ctx[None, :], c, jnp.zeros((MOD_ROWS - 1 - DEC_BATCH, D_MODEL), F32)], axis=0)
    mod = _mod_call(c_all, w_mod[0], b_mod)
    mod3 = mod.reshape(MOD_ROWS, 1, 6 * D_MODEL)
    h = _prenorm_call([xp, xs], norm_attn, mod3, 0, 1, "prenorm_attn")

    w_in0 = w_in[0]
    tm = 1024
    tbl = pl.BlockSpec((DEC_SEQ, HEAD_DIM), lambda n, m: (jnp.where(m < T_P // tm, 0, 1), 0))
    hd = pl.BlockSpec((1, HEAD_DIM), lambda n, m: (0, 0))
    rope_extras = [(cos, tbl), (s_lo, tbl), (s_hi, tbl)]

    def proj(col0, n_cols, tn, epilogue, extras, outs, name):
        return _matmul_call([h], [(w_in0, col0)], [(0, 0)], n_cols, tm, tn, epilogue, extras, outs, name)

    def full(n_cols, dtype):
        return jax.ShapeDtypeStruct((T, n_cols), dtype)

    (q_att,) = proj(COL_Q, ATTN_Q, 1024, functools.partial(_epi_q, tn=1024),
                    [(q_norm, hd)] + rope_extras,
                    [(full(ATTN_Q, BF16), _tile_spec(tm, 1024))], "proj_q")
    prompt_kv = jax.ShapeDtypeStruct((T_P + tm, ATTN_KV), F32)
    k_att, k_new = proj(COL_K, ATTN_KV, ATTN_KV, functools.partial(_epi_k, tn=ATTN_KV),
                        [(k_norm, hd)] + rope_extras,
                        [(full(ATTN_KV, BF16), _tile_spec(tm, ATTN_KV)),
                         (prompt_kv, _prompt_tile_spec(tm, ATTN_KV))], "proj_k")
    v_att, v_new = proj(COL_V, ATTN_KV, ATTN_KV, _epi_v, [],
                        [(full(ATTN_KV, BF16), _tile_spec(tm, ATTN_KV)),
                         (prompt_kv, _prompt_tile_spec(tm, ATTN_KV))], "proj_v")
    (q_ret,) = proj(COL_QR, RET_QK, 1024, _epi_cast, [],
                    [(full(RET_QK, BF16), _tile_spec(tm, 1024))], "proj_qr")
    (k_ret,) = proj(COL_KR, RET_QK, 1024, functools.partial(_epi_scale, scale=RET_KEY_DIM ** -0.5), [],
                    [(full(RET_QK, F32), _tile_spec(tm, 1024))], "proj_kr")
    (v_ret,) = proj(COL_VR, RET_V, 1024, _epi_cast, [],
                    [(full(RET_V, BF16), _tile_spec(tm, 1024))], "proj_vr")
    (sg,) = proj(COL_GR, RET_V, 1024, _epi_silu, [],
                 [(full(RET_V, F32), _tile_spec(tm, 1024))], "proj_gr")
    (gates,) = proj(COL_GATES, 2 * D_MODEL, 1024, _epi_sigmoid, [],
                    [(full(2 * D_MODEL, F32), _tile_spec(tm, 1024))], "proj_gates")

    o_a_p = _attn_call(q_att, k_att, v_att, None, None, latent=False)
    o_a_s = _attn_call(q_att, k_att, v_att, ctx_k, ctx_v, latent=True)
    o_r_p, new_sf, new_sb = _ret_call(lgf, lgb, q_ret, k_ret, v_ret, sg, ret_norm,
                                      None, None, latent=False)
    o_r_s = _ret_call(lgf, lgb, q_ret, k_ret, v_ret, sg, ret_norm,
                      state_ret_fwd, state_ret_bwd, latent=True)

    tm2, tn2 = 512, 512
    (merged,) = _matmul_call(
        [(o_a_p, o_a_s), (o_r_p, o_r_s)], [(w_branch_attn[0], 0), (w_branch_ret[0], 0)],
        [(0, 0), (1, 1)], D_MODEL, tm2, tn2, _epi_merge,
        [(gates, _tile_spec(tm2, tn2)), (gates, _tile_spec(tm2, tn2, D_MODEL // tn2))],
        [(full(D_MODEL, BF16), _tile_spec(tm2, tn2))], "merge")

    tm3, tn3 = 512, 1024
    n_p3 = T_P // tm3
    (x1,) = _matmul_call(
        [merged], [(w_out[0], 0)], [(0, 0)], D_MODEL, tm3, tn3,
        functools.partial(_epi_residual2, tm=tm3),
        [(xp, pl.BlockSpec((tm3, tn3), lambda n, m: (jnp.minimum(m, n_p3 - 1), n))),
         (xs, pl.BlockSpec((tm3, tn3), lambda n, m: (jnp.maximum(m - n_p3, 0), n))),
         (mod3, _mod_spec(tm3, tn3, lambda n: 2 * (D_MODEL // tn3) + n, 2))],
        [(full(D_MODEL, F32), _tile_spec(tm3, tn3))], "out_proj")

    h2 = _prenorm_call([x1], norm_ffn, mod3, 3, 4, "prenorm_ffn")

    (act,) = _matmul_call(
        [h2], [(w_ffn_gate[0], 0), (w_ffn_up[0], 0)], [(0, 0), (0, 1)],
        D_FF, 1024, 512, _epi_swiglu, [],
        [(full(D_FF, BF16), _tile_spec(1024, 512))], "ffn_up")

    tm4, tn4 = 512, 1024
    (y_pre,) = _matmul_call(
        [act], [(w_ffn_down[0], 0)], [(0, 0)], D_MODEL, tm4, tn4, _epi_residual,
        [(x1, _tile_spec(tm4, tn4)),
         (mod3, _mod_spec(tm4, tn4, lambda n: 5 * (D_MODEL // tn4) + n, 2))],
        [(full(D_MODEL, F32), _tile_spec(tm4, tn4))], "ffn_down", stream=True, kc=512)

    y_p, y_s = _final_call(y_pre, final_norm[None, :])

    return (y_p.reshape(BATCH, SEQ, D_MODEL),
            y_s.reshape(DEC_BATCH, DEC_SEQ, D_MODEL),
            k_new[:T_P].reshape(BATCH, 1, SEQ, N_KV_HEADS, HEAD_DIM),
            v_new[:T_P].reshape(BATCH, 1, SEQ, N_KV_HEADS, HEAD_DIM),
            new_sf, new_sb)
```

```python
import functools

import jax
import jax.numpy as jnp
from jax import lax
from jax.experimental import pallas as pl
from jax.experimental.pallas import tpu as pltpu

D_MODEL = 2048
BATCH = 16
SEQ = 256
DEC_BATCH = 8
DEC_SEQ = 1024
PAST_LEN = 512
GRID_W = 64
N_HEADS = 16
N_KV_HEADS = 4
HEAD_DIM = 128
ROPE_THETA = 10000.0
N_RET_HEADS = 8
RET_KEY_DIM = 128
RET_VAL_DIM = 256
D_FF = 5632
EPS = 1e-6

ATTN_Q = N_HEADS * HEAD_DIM
ATTN_KV = N_KV_HEADS * HEAD_DIM
RET_QK = N_RET_HEADS * RET_KEY_DIM
RET_V = N_RET_HEADS * RET_VAL_DIM
GQA_GROUP = N_HEADS // N_KV_HEADS

COL_Q = 0
COL_K = COL_Q + ATTN_Q
COL_V = COL_K + ATTN_KV
COL_QR = COL_V + ATTN_KV
COL_KR = COL_QR + RET_QK
COL_VR = COL_KR + RET_QK
COL_GR = COL_VR + RET_V
COL_GATES = COL_GR + RET_V

T_P = BATCH * SEQ
T_S = DEC_BATCH * DEC_SEQ
T = T_P + T_S
MOD_ROWS = 16

V7X_VMEM_BYTES = 64 * 1024 * 1024
VMEM_CAP = V7X_VMEM_BYTES - 6 * 1024 * 1024

BF16 = jnp.bfloat16
F32 = jnp.float32


def _nbytes(shape, dtype):
    n = 1
    for s in shape:
        if s is not None:
            n *= s
    return n * jnp.dtype(dtype).itemsize


def _params(vmem_estimate, n_grid):
    limit = min(VMEM_CAP, int(vmem_estimate) + 8 * 1024 * 1024)
    return pltpu.CompilerParams(
        dimension_semantics=("arbitrary",) * n_grid, vmem_limit_bytes=limit)


def _mod_row(m, tm):
    n_p = T_P // tm
    return jnp.where(m < n_p, 0, 1 + (m - n_p) // (DEC_SEQ // tm))


def _mod_spec(tm, width, col_block, grid_rank):
    if grid_rank == 1:
        return pl.BlockSpec((None, 1, width), lambda m: (_mod_row(m, tm), 0, col_block(0)))
    return pl.BlockSpec((None, 1, width), lambda n, m: (_mod_row(m, tm), 0, col_block(n)))


def _mod_body(c_ref, w_ref, b_ref, o_ref):
    a = jax.nn.silu(c_ref[...]).astype(BF16)
    w = w_ref[...].astype(BF16)
    o_ref[...] = jnp.dot(a, w, preferred_element_type=F32) + b_ref[...]


def _mod_call(c_all, w_mod, b_mod):
    tn = 1024
    n_out = 6 * D_MODEL
    est = 2 * _nbytes((D_MODEL, tn), F32) + _nbytes((D_MODEL, tn), BF16)
    return pl.pallas_call(
        _mod_body,
        grid=(n_out // tn,),
        in_specs=[
            pl.BlockSpec((MOD_ROWS, D_MODEL), lambda n: (0, 0)),
            pl.BlockSpec((D_MODEL, tn), lambda n: (0, n)),
            pl.BlockSpec((1, tn), lambda n: (0, n)),
        ],
        out_specs=pl.BlockSpec((MOD_ROWS, tn), lambda n: (0, n)),
        out_shape=jax.ShapeDtypeStruct((MOD_ROWS, n_out), F32),
        compiler_params=_params(est, 1),
        name="mod_table",
    )(c_all, w_mod, b_mod)


def _modulated_norm(x, g, sc, sh):
    y = x * lax.rsqrt(jnp.mean(x * x, axis=-1, keepdims=True) + EPS)
    return (y * g) * (1.0 + sc) + sh


def _prenorm2_body(xp_ref, xs_ref, g_ref, sc_ref, sh_ref, o_ref, *, n_p):
    m = pl.program_id(0)

    @pl.when(m < n_p)
    def _():
        o_ref[...] = _modulated_norm(xp_ref[...], g_ref[...], sc_ref[...], sh_ref[...]).astype(BF16)

    @pl.when(m >= n_p)
    def _():
        o_ref[...] = _modulated_norm(xs_ref[...], g_ref[...], sc_ref[...], sh_ref[...]).astype(BF16)


def _prenorm1_body(x_ref, g_ref, sc_ref, sh_ref, o_ref):
    o_ref[...] = _modulated_norm(x_ref[...], g_ref[...], sc_ref[...], sh_ref[...]).astype(BF16)


def _prenorm_call(xs, norm_w, mod3, sh_blk, sc_blk, name):
    tm = 512 if len(xs) == 2 else 1024
    n_p = T_P // tm
    blk = (tm, D_MODEL)
    common = [
        pl.BlockSpec((1, D_MODEL), lambda m: (0, 0)),
        _mod_spec(tm, D_MODEL, lambda n: sc_blk, 1),
        _mod_spec(tm, D_MODEL, lambda n: sh_blk, 1),
    ]
    if len(xs) == 2:
        in_specs = [
            pl.BlockSpec(blk, lambda m: (jnp.minimum(m, n_p - 1), 0)),
            pl.BlockSpec(blk, lambda m: (jnp.maximum(m - n_p, 0), 0)),
        ] + common
        body = functools.partial(_prenorm2_body, n_p=n_p)
    else:
        in_specs = [pl.BlockSpec(blk, lambda m: (m, 0))] + common
        body = _prenorm1_body
    est = 2 * len(xs) * _nbytes(blk, F32) + 2 * _nbytes(blk, BF16) + 2 * _nbytes(blk, F32)
    return pl.pallas_call(
        body,
        grid=(T // tm,),
        in_specs=in_specs,
        out_specs=pl.BlockSpec(blk, lambda m: (m, 0)),
        out_shape=jax.ShapeDtypeStruct((T, D_MODEL), BF16),
        compiler_params=_params(est, 1),
        name=name,
    )(*xs, norm_w, mod3, mod3)


def _matmul_body(*refs, lhs_split, n_rhs, col0s, pairs, n_extra, n_out, epilogue,
                 tm, tn, kc, row_chunk, stream, single_tile):
    n_lhs_refs = sum(2 if s else 1 for s in lhs_split)
    lhs_refs = refs[:n_lhs_refs]
    pos = n_lhs_refs
    rhs = refs[pos:pos + n_rhs]
    pos += n_rhs
    extras = refs[pos:pos + n_extra]
    pos += n_extra
    outs = refs[pos:pos + n_out]
    pos += n_out
    wb = refs[pos:pos + n_rhs]
    n, m = pl.program_id(0), pl.program_id(1)
    is_prompt = m < T_P // tm

    def lhs_rows(i, rows):
        first = sum(2 if s else 1 for s in lhs_split[:i])
        if lhs_split[i]:
            return jnp.where(is_prompt, lhs_refs[first][rows, :], lhs_refs[first + 1][rows, :])
        return lhs_refs[first][rows, :]

    def compute(weight):
        for r in range(tm // row_chunk):
            rows = slice(r * row_chunk, (r + 1) * row_chunk)
            zs = [jnp.dot(lhs_rows(i, rows), weight(j), preferred_element_type=F32) for i, j in pairs]
            epilogue(zs, extras, outs, rows)

    if not stream:
        @pl.when(m == 0)
        def _():
            for w_ref, wb_ref in zip(rhs, wb):
                wb_ref[...] = w_ref[...].astype(BF16)

        compute(lambda j: wb[j][...])
        return

    stage = refs[pos + n_rhs:pos + 2 * n_rhs]
    sems = refs[pos + 2 * n_rhs:pos + 3 * n_rhs]
    n_tiles = pl.num_programs(0)
    n_chunks = wb[0].shape[1] // kc

    def chunk_copy(j, tile, chunk, slot):
        col = pl.multiple_of(col0s[j] + tile * tn, tn)
        row = pl.multiple_of(chunk * kc, kc)
        return pltpu.make_async_copy(
            rhs[j].at[pl.ds(row, kc), pl.ds(col, tn)], stage[j].at[slot], sems[j].at[slot])

    nxt = jnp.minimum(n + 1, n_tiles - 1)
    chunk = jnp.minimum(m, n_chunks - 1)
    first = jnp.logical_and(n == 0, m == 0)

    def start_prefetch():
        for j in range(n_rhs):
            chunk_copy(j, nxt, chunk, 0).start()

    @pl.when(first)
    def _():
        for j in range(n_rhs):
            chunk_copy(j, 0, 0, 0).start()
            for c in range(n_chunks):
                if c + 1 < n_chunks:
                    chunk_copy(j, 0, c + 1, (c + 1) % 2).start()
                chunk_copy(j, 0, c, c % 2).wait()
                wb[j][0, c * kc:(c + 1) * kc, :] = stage[j][c % 2].astype(BF16)
        if not single_tile:
            start_prefetch()

    if single_tile:
        compute(lambda j: wb[j][0])
        return

    pl.when(jnp.logical_not(first))(start_prefetch)

    for cur in range(2):
        @pl.when(n % 2 == cur)
        def _(cur=cur):
            compute(lambda j: wb[j][cur])
            for j in range(n_rhs):
                chunk_copy(j, nxt, chunk, 0).wait()
                wb[j][1 - cur, pl.ds(pl.multiple_of(chunk * kc, kc), kc), :] = stage[j][0].astype(BF16)


def _matmul_call(lhs_list, rhs_list, pairs, n_cols, tm, tn, epilogue, extras, outs, name,
                 row_chunk=256, stream=False, kc=256):
    lhs_split = tuple(isinstance(a, tuple) for a in lhs_list)
    k_dim = (lhs_list[0][0] if lhs_split[0] else lhs_list[0]).shape[1]
    n_p = T_P // tm
    grid = (n_cols // tn, T // tm)
    in_specs, lhs_args = [], []
    for a in lhs_list:
        if isinstance(a, tuple):
            in_specs.append(pl.BlockSpec((tm, k_dim), lambda n, m: (jnp.minimum(m, n_p - 1), 0)))
            in_specs.append(pl.BlockSpec((tm, k_dim), lambda n, m: (jnp.maximum(m - n_p, 0), 0)))
            lhs_args += list(a)
        else:
            in_specs.append(pl.BlockSpec((tm, k_dim), lambda n, m: (m, 0)))
            lhs_args.append(a)
    est = 2 * len(lhs_args) * _nbytes((tm, k_dim), BF16)
    single_tile = stream and grid[0] == 1
    if stream:
        assert k_dim % kc == 0 and k_dim // kc <= grid[1], (k_dim, kc, grid)
        n_buf = 1 if single_tile else 2
        in_specs += [pl.BlockSpec(memory_space=pl.ANY) for _ in rhs_list]
        est += len(rhs_list) * (n_buf * _nbytes((k_dim, tn), BF16) + 2 * _nbytes((kc, tn), F32))
        scratch = [pltpu.VMEM((n_buf, k_dim, tn), BF16) for _ in rhs_list]
        scratch += [pltpu.VMEM((2, kc, tn), F32) for _ in rhs_list]
        scratch += [pltpu.SemaphoreType.DMA((2,)) for _ in rhs_list]
    else:
        for _, col0 in rhs_list:
            in_specs.append(pl.BlockSpec((k_dim, tn), lambda n, m, c=col0 // tn: (0, c + n)))
        est += len(rhs_list) * (2 * _nbytes((k_dim, tn), F32) + _nbytes((k_dim, tn), BF16))
        scratch = [pltpu.VMEM((k_dim, tn), BF16) for _ in rhs_list]
    in_specs += [spec for _, spec in extras]
    for arr, spec in list(extras) + list(outs):
        est += 2 * _nbytes(spec.block_shape, arr.dtype)
    est += 3 * len(pairs) * _nbytes((row_chunk, tn), F32)
    body = functools.partial(
        _matmul_body, lhs_split=lhs_split, n_rhs=len(rhs_list),
        col0s=tuple(c for _, c in rhs_list), pairs=tuple(pairs),
        n_extra=len(extras), n_out=len(outs), epilogue=epilogue,
        tm=tm, tn=tn, kc=kc, row_chunk=row_chunk, stream=stream, single_tile=single_tile)
    res = pl.pallas_call(
        body,
        grid=grid,
        in_specs=in_specs,
        out_specs=[spec for _, spec in outs],
        out_shape=[s for s, _ in outs],
        scratch_shapes=scratch,
        compiler_params=_params(est, 2),
        name=name,
    )(*lhs_args, *[w for w, _ in rhs_list], *[a for a, _ in extras])
    return res


def _tile_spec(tm, tn, col_block0=0):
    return pl.BlockSpec((tm, tn), lambda n, m, c=col_block0: (m, c + n))


def _prompt_tile_spec(tm, tn):
    n_p = T_P // tm
    return pl.BlockSpec((tm, tn), lambda n, m: (jnp.minimum(m, n_p), n))


def _head_rms(x, g):
    return x * lax.rsqrt(jnp.mean(x * x, axis=-1, keepdims=True) + EPS) * g


def _rope(y, c, s_lo, s_hi):
    return y * c + pltpu.roll(y, HEAD_DIM - 32, axis=1) * s_lo + pltpu.roll(y, 32, axis=1) * s_hi


def _epi_q(zs, extras, outs, rows, *, tn):
    g_ref, c_ref, slo_ref, shi_ref = extras
    (q_ref,) = outs
    (z,) = zs
    for h in range(tn // HEAD_DIM):
        sl = slice(h * HEAD_DIM, (h + 1) * HEAD_DIM)
        y = _head_rms(z[:, sl], g_ref[...])
        q_ref[rows, sl] = _rope(y, c_ref[rows, :], slo_ref[rows, :], shi_ref[rows, :]).astype(BF16)


def _epi_k(zs, extras, outs, rows, *, tn):
    g_ref, c_ref, slo_ref, shi_ref = extras
    k_att_ref, k_new_ref = outs
    (z,) = zs
    for h in range(tn // HEAD_DIM):
        sl = slice(h * HEAD_DIM, (h + 1) * HEAD_DIM)
        y = _head_rms(z[:, sl], g_ref[...])
        k_new_ref[rows, sl] = y
        k_att_ref[rows, sl] = _rope(y, c_ref[rows, :], slo_ref[rows, :], shi_ref[rows, :]).astype(BF16)


def _epi_v(zs, extras, outs, rows):
    v_att_ref, v_new_ref = outs
    v_att_ref[rows, :] = zs[0].astype(BF16)
    v_new_ref[rows, :] = zs[0]


def _epi_cast(zs, extras, outs, rows):
    outs[0][rows, :] = zs[0].astype(outs[0].dtype)


def _epi_scale(zs, extras, outs, rows, *, scale):
    outs[0][rows, :] = zs[0] * scale


def _epi_silu(zs, extras, outs, rows):
    outs[0][rows, :] = jax.nn.silu(zs[0])


def _epi_sigmoid(zs, extras, outs, rows):
    outs[0][rows, :] = jax.nn.sigmoid(zs[0])


def _epi_merge(zs, extras, outs, rows):
    ga_ref, gr_ref = extras
    outs[0][rows, :] = (ga_ref[rows, :] * zs[0] + gr_ref[rows, :] * zs[1]).astype(BF16)


def _epi_residual_norm(zs, extras, outs, rows, *, tm):
    xp_ref, xs_ref, gate_ref, g_ref, sc_ref, sh_ref = extras
    x1_ref, h_ref = outs
    x = jnp.where(pl.program_id(1) < T_P // tm, xp_ref[rows, :], xs_ref[rows, :])
    x1 = x + gate_ref[...] * zs[0]
    x1_ref[rows, :] = x1
    h_ref[rows, :] = _modulated_norm(x1, g_ref[...], sc_ref[...], sh_ref[...]).astype(BF16)


def _epi_swiglu(zs, extras, outs, rows):
    outs[0][rows, :] = (jax.nn.silu(zs[0]) * zs[1]).astype(BF16)


def _epi_residual(zs, extras, outs, rows):
    x_ref, gate_ref = extras
    outs[0][rows, :] = x_ref[rows, :] + gate_ref[...] * zs[0]


_NT = (((1,), (1,)), ((), ()))


_SOFTMAX_EXP2_SCALE = HEAD_DIM ** -0.5 * 1.4426950408889634


def _attn_body(*refs, has_ctx, seq, tq, n_sub):
    if has_ctx:
        q_ref, k_ref, v_ref, ck_ref, cv_ref, o_ref = refs
        ck = ck_ref[...].astype(BF16)
        cv = cv_ref[...].astype(BF16)
    else:
        q_ref, k_ref, v_ref, o_ref = refs
    chain_rows = tq
    for i, c in [(i, c) for i in range(n_sub) for c in range(tq // chain_rows)]:
        k = k_ref[i * seq:(i + 1) * seq, :]
        v = v_ref[i * seq:(i + 1) * seq, :]
        rows = slice(i * tq + c * chain_rows, i * tq + (c + 1) * chain_rows)
        for h in range(GQA_GROUP):
            sl = slice(h * HEAD_DIM, (h + 1) * HEAD_DIM)
            q = q_ref[rows, sl]
            s_new = lax.dot_general(q, k, _NT, preferred_element_type=F32)
            mx = jnp.max(s_new, axis=-1, keepdims=True)
            if has_ctx:
                s_ctx = lax.dot_general(q, ck, _NT, preferred_element_type=F32)
                mx = jnp.maximum(mx, jnp.max(s_ctx, axis=-1, keepdims=True))
                p_ctx = jnp.exp2((s_ctx - mx) * _SOFTMAX_EXP2_SCALE)
                p_new = jnp.exp2((s_new - mx) * _SOFTMAX_EXP2_SCALE)
                den = jnp.sum(p_ctx, axis=-1, keepdims=True) + jnp.sum(p_new, axis=-1, keepdims=True)
                o = jnp.dot(p_ctx.astype(BF16), cv, preferred_element_type=F32)
                o = o + jnp.dot(p_new.astype(BF16), v, preferred_element_type=F32)
            else:
                p_new = jnp.exp2((s_new - mx) * _SOFTMAX_EXP2_SCALE)
                den = jnp.sum(p_new, axis=-1, keepdims=True)
                o = jnp.dot(p_new.astype(BF16), v, preferred_element_type=F32)
            o_ref[rows, sl] = (o * (1.0 / den)).astype(BF16)


def _attn_call(q_att, k_att, v_att, ctx_k, ctx_v, *, latent):
    gw = GQA_GROUP * HEAD_DIM
    if latent:
        seq, nb, tq, n_sub = DEC_SEQ, DEC_BATCH, 512, 1
        row0 = T_P
    else:
        seq, nb, tq, n_sub = SEQ, BATCH, SEQ, 4
        row0 = 0
    nqt = seq // tq
    q_rows, kv_rows = n_sub * tq, n_sub * seq
    q_spec = pl.BlockSpec((q_rows, gw), lambda b, g, t: (row0 // q_rows + b * nqt + t, g))
    o_spec = pl.BlockSpec((q_rows, gw), lambda b, g, t: (b * nqt + t, g))
    kv_spec = pl.BlockSpec((kv_rows, HEAD_DIM), lambda b, g, t: (row0 // kv_rows + b, g))
    in_specs = [q_spec, kv_spec, kv_spec]
    args = [q_att, k_att, v_att]
    n_keys = seq
    if latent:
        c_spec = pl.BlockSpec((None, PAST_LEN, HEAD_DIM), lambda b, g, t: (b, 0, g))
        in_specs += [c_spec, c_spec]
        args += [ctx_k, ctx_v]
        n_keys += PAST_LEN
    est = 4 * _nbytes((q_rows, gw), BF16) + 4 * _nbytes((kv_rows, HEAD_DIM), BF16)
    est += 4 * _nbytes((PAST_LEN, HEAD_DIM), F32)
    est += 3 * GQA_GROUP * n_sub * _nbytes((tq, n_keys), F32)
    return pl.pallas_call(
        functools.partial(_attn_body, has_ctx=latent, seq=seq, tq=tq, n_sub=n_sub),
        grid=(nb // n_sub, N_KV_HEADS, nqt),
        in_specs=in_specs,
        out_specs=o_spec,
        out_shape=jax.ShapeDtypeStruct((nb * seq, ATTN_Q), BF16),
        compiler_params=_params(est, 3),
        name="attn_latent" if latent else "attn_prompt",
    )(*args)


_TN = (((0,), (0,)), ((), ()))


def _ret_body(*refs, seq, tq, latent, n_sub):
    if latent:
        (lgf_ref, lgb_ref, q_ref, k_ref, v_ref, sg_ref, rn_ref, s0f_ref, s0b_ref,
         o_ref, d_ref) = refs
    else:
        (lgf_ref, lgb_ref, q_ref, k_ref, v_ref, sg_ref, rn_ref,
         o_ref, sf_ref, sb_ref, d_ref) = refs
    h = pl.program_id(0)
    lgf = lgf_ref[h]
    lgb = lgb_ref[h]

    @pl.when(pl.program_id(1) == 0)
    def _():
        for t in range(seq // tq):
            i = lax.broadcasted_iota(jnp.int32, (tq, seq), 0) + t * tq
            j = lax.broadcasted_iota(jnp.int32, (tq, seq), 1)
            diff = (i - j).astype(F32)
            arg = jnp.where(diff >= 0, diff * lgf, -diff * lgb)
            d_ref[t * tq:(t + 1) * tq, :] = jnp.exp(arg) * jnp.where(diff == 0, 2.0, 1.0)

    if latent:
        s0f = s0f_ref[...].astype(BF16)
        s0b = s0b_ref[...].astype(BF16)
    for i in range(n_sub):
        k_f32 = k_ref[i * seq:(i + 1) * seq, :]
        kb = k_f32.astype(BF16)
        v = v_ref[i * seq:(i + 1) * seq, :]
        for t in range(seq // tq):
            rows = slice(i * seq + t * tq, i * seq + (t + 1) * tq)
            q = q_ref[rows, :]
            raw = lax.dot_general(q, kb, _NT, preferred_element_type=F32)
            p = (raw * d_ref[t * tq:(t + 1) * tq, :]).astype(BF16)
            o = jnp.dot(p, v, preferred_element_type=F32)
            if latent:
                pos = (lax.broadcasted_iota(jnp.int32, (tq, RET_VAL_DIM), 0) + t * tq).astype(F32)
                xi_f = jnp.exp((pos + 1.0) * lgf)
                xi_b = jnp.exp((seq - pos) * lgb)
                o = o + jnp.dot(q, s0f, preferred_element_type=F32) * xi_f
                o = o + jnp.dot(q, s0b, preferred_element_type=F32) * xi_b
            mu = jnp.mean(o, axis=-1, keepdims=True)
            oc = o - mu
            var = jnp.mean(oc * oc, axis=-1, keepdims=True)
            y = (oc * lax.rsqrt(var + EPS)) * rn_ref[...]
            o_ref[rows, :] = (y * sg_ref[rows, :]).astype(BF16)
        if not latent:
            j = lax.broadcasted_iota(jnp.int32, (seq, RET_KEY_DIM), 0).astype(F32)
            kz_f = (k_f32 * jnp.exp((seq - 1.0 - j) * lgf)).astype(BF16)
            kz_b = (k_f32 * jnp.exp(j * lgb)).astype(BF16)
            sf_ref[i] = lax.dot_general(kz_f, v, _TN, preferred_element_type=F32)
            sb_ref[i] = lax.dot_general(kz_b, v, _TN, preferred_element_type=F32)


def _ret_call(lgf, lgb, q_ret, k_ret, v_ret, sg, ret_norm, s0f, s0b, *, latent):
    if latent:
        seq, nb, row0, n_sub = DEC_SEQ, DEC_BATCH, T_P, 1
    else:
        seq, nb, row0, n_sub = SEQ, BATCH, 0, 4
    tq = 256
    rows = n_sub * seq
    rb = row0 // rows
    smem = pl.BlockSpec(memory_space=pltpu.SMEM)
    qk_spec = pl.BlockSpec((rows, RET_KEY_DIM), lambda h, b: (rb + b, h))
    v_spec = pl.BlockSpec((rows, RET_VAL_DIM), lambda h, b: (rb + b, h))
    o_spec = pl.BlockSpec((rows, RET_VAL_DIM), lambda h, b: (b, h))
    in_specs = [smem, smem, qk_spec, qk_spec, v_spec, v_spec,
                pl.BlockSpec((1, RET_VAL_DIM), lambda h, b: (0, h))]
    args = [lgf, lgb, q_ret, k_ret, v_ret, sg, ret_norm]
    o_shape = jax.ShapeDtypeStruct((nb * seq, RET_V), BF16)
    if latent:
        st_spec = pl.BlockSpec((None, None, None, RET_KEY_DIM, RET_VAL_DIM),
                               lambda h, b: (b, 0, h, 0, 0))
        in_specs += [st_spec, st_spec]
        args += [s0f, s0b]
        out_specs = o_spec
        out_shape = o_shape
    else:
        st_spec = pl.BlockSpec((n_sub, None, None, RET_KEY_DIM, RET_VAL_DIM),
                               lambda h, b: (b, 0, h, 0, 0))
        st_shape = jax.ShapeDtypeStruct((BATCH, 1, N_RET_HEADS, RET_KEY_DIM, RET_VAL_DIM), F32)
        out_specs = [o_spec, st_spec, st_spec]
        out_shape = [o_shape, st_shape, st_shape]
    est = _nbytes((seq, seq), F32) + 4 * n_sub * _nbytes((tq, seq), F32)
    est += 2 * (_nbytes((rows, RET_KEY_DIM), BF16) + _nbytes((rows, RET_KEY_DIM), F32))
    est += 2 * (2 * _nbytes((rows, RET_VAL_DIM), BF16) + _nbytes((rows, RET_VAL_DIM), F32))
    est += 8 * n_sub * _nbytes((RET_KEY_DIM, RET_VAL_DIM), F32)
    return pl.pallas_call(
        functools.partial(_ret_body, seq=seq, tq=tq, latent=latent, n_sub=n_sub),
        grid=(N_RET_HEADS, nb // n_sub),
        in_specs=in_specs,
        out_specs=out_specs,
        out_shape=out_shape,
        scratch_shapes=[pltpu.VMEM((seq, seq), F32)],
        compiler_params=_params(est, 2),
        name="ret_latent" if latent else "ret_prompt",
    )(*args)


def _final_body(x_ref, g_ref, yp_ref, ys_ref, *, n_p):
    x = x_ref[...]
    y = x * lax.rsqrt(jnp.mean(x * x, axis=-1, keepdims=True) + EPS) * g_ref[...]
    m = pl.program_id(0)

    @pl.when(m < n_p)
    def _():
        yp_ref[...] = y

    @pl.when(m >= n_p)
    def _():
        ys_ref[...] = y


def _final_call(x, g):
    tm = 512
    n_p = T_P // tm
    blk = (tm, D_MODEL)
    est = 6 * _nbytes(blk, F32) + 2 * _nbytes(blk, F32)
    return pl.pallas_call(
        functools.partial(_final_body, n_p=n_p),
        grid=(T // tm,),
        in_specs=[pl.BlockSpec(blk, lambda m: (m, 0)), pl.BlockSpec((1, D_MODEL), lambda m: (0, 0))],
        out_specs=[
            pl.BlockSpec(blk, lambda m: (jnp.minimum(m, n_p - 1), 0)),
            pl.BlockSpec(blk, lambda m: (jnp.maximum(m - n_p, 0), 0)),
        ],
        out_shape=[jax.ShapeDtypeStruct((T_P, D_MODEL), F32), jax.ShapeDtypeStruct((T_S, D_MODEL), F32)],
        compiler_params=_params(est, 1),
        name="final_norm",
    )(x, g)


def _rope_tables():
    rows = DEC_SEQ // GRID_W
    row = jnp.repeat(jnp.arange(rows, dtype=F32), GRID_W)
    col = jnp.tile(jnp.arange(GRID_W, dtype=F32), rows)
    quarter = HEAD_DIM // 4
    inv_freq = ROPE_THETA ** (-jnp.arange(quarter, dtype=F32) / quarter)
    ang_r = row[:, None] * inv_freq[None, :]
    ang_c = col[:, None] * inv_freq[None, :]
    cr, sr, cc, sc = jnp.cos(ang_r), jnp.sin(ang_r), jnp.cos(ang_c), jnp.sin(ang_c)
    zero = jnp.zeros_like(sr)
    cos = jnp.concatenate([cr, cr, cc, cc], axis=-1)
    s_lo = jnp.concatenate([-sr, zero, -sc, zero], axis=-1)
    s_hi = jnp.concatenate([zero, sr, zero, sc], axis=-1)
    ident = jnp.zeros((DEC_SEQ, HEAD_DIM), F32)
    return (jnp.concatenate([ident + 1.0, cos], axis=0),
            jnp.concatenate([ident, s_lo], axis=0),
            jnp.concatenate([ident, s_hi], axis=0))


def kernel(x_prompt, x_sample, cache_attn_k, cache_attn_v, state_ret_fwd, state_ret_bwd, c, c_ctx,
           norm_attn, norm_ffn, w_mod, b_mod, w_in, q_norm, k_norm, ret_decay_fwd, ret_decay_bwd,
           ret_norm, w_branch_attn, w_branch_ret, w_out, w_ffn_gate, w_ffn_up, w_ffn_down, final_norm):
    xp = x_prompt.reshape(T_P, D_MODEL)
    xs = x_sample.reshape(T_S, D_MODEL)
    ctx_k = cache_attn_k.reshape(DEC_BATCH, PAST_LEN, ATTN_KV)
    ctx_v = cache_attn_v.reshape(DEC_BATCH, PAST_LEN, ATTN_KV)
    lgf = jax.nn.log_sigmoid(ret_decay_fwd[0].astype(F32))
    lgb = jax.nn.log_sigmoid(ret_decay_bwd[0].astype(F32))
    cos, s_lo, s_hi = _rope_tables()

    c_all = jnp.concatenate(
        [c_ctx[None, :], c, jnp.zeros((MOD_ROWS - 1 - DEC_BATCH, D_MODEL), F32)], axis=0)
    mod = _mod_call(c_all, w_mod[0], b_mod)
    mod3 = mod.reshape(MOD_ROWS, 1, 6 * D_MODEL)
    h = _prenorm_call([xp, xs], norm_attn, mod3, 0, 1, "prenorm_attn")

    w_in0 = w_in[0]
    tm = 1024
    tbl = pl.BlockSpec((DEC_SEQ, HEAD_DIM), lambda n, m: (jnp.where(m < T_P // tm, 0, 1), 0))
    hd = pl.BlockSpec((1, HEAD_DIM), lambda n, m: (0, 0))
    rope_extras = [(cos, tbl), (s_lo, tbl), (s_hi, tbl)]

    def proj(col0, n_cols, tn, epilogue, extras, outs, name):
        return _matmul_call([h], [(w_in0, col0)], [(0, 0)], n_cols, tm, tn, epilogue, extras, outs, name)

    def full(n_cols, dtype):
        return jax.ShapeDtypeStruct((T, n_cols), dtype)

    (q_att,) = proj(COL_Q, ATTN_Q, 1024, functools.partial(_epi_q, tn=1024),
                    [(q_norm, hd)] + rope_extras,
                    [(full(ATTN_Q, BF16), _tile_spec(tm, 1024))], "proj_q")
    prompt_kv = jax.ShapeDtypeStruct((T_P + tm, ATTN_KV), F32)
    k_att, k_new = proj(COL_K, ATTN_KV, ATTN_KV, functools.partial(_epi_k, tn=ATTN_KV),
                        [(k_norm, hd)] + rope_extras,
                        [(full(ATTN_KV, BF16), _tile_spec(tm, ATTN_KV)),
                         (prompt_kv, _prompt_tile_spec(tm, ATTN_KV))], "proj_k")
    v_att, v_new = proj(COL_V, ATTN_KV, ATTN_KV, _epi_v, [],
                        [(full(ATTN_KV, BF16), _tile_spec(tm, ATTN_KV)),
                         (prompt_kv, _prompt_tile_spec(tm, ATTN_KV))], "proj_v")
    (q_ret,) = proj(COL_QR, RET_QK, 1024, _epi_cast, [],
                    [(full(RET_QK, BF16), _tile_spec(tm, 1024))], "proj_qr")
    (k_ret,) = proj(COL_KR, RET_QK, 1024, functools.partial(_epi_scale, scale=RET_KEY_DIM ** -0.5), [],
                    [(full(RET_QK, F32), _tile_spec(tm, 1024))], "proj_kr")
    (v_ret,) = proj(COL_VR, RET_V, 1024, _epi_cast, [],
                    [(full(RET_V, BF16), _tile_spec(tm, 1024))], "proj_vr")
    (sg,) = proj(COL_GR, RET_V, 1024, _epi_silu, [],
                 [(full(RET_V, F32), _tile_spec(tm, 1024))], "proj_gr")
    (gates,) = proj(COL_GATES, 2 * D_MODEL, 1024, _epi_sigmoid, [],
                    [(full(2 * D_MODEL, F32), _tile_spec(tm, 1024))], "proj_gates")

    o_a_p = _attn_call(q_att, k_att, v_att, None, None, latent=False)
    o_a_s = _attn_call(q_att, k_att, v_att, ctx_k, ctx_v, latent=True)
    o_r_p, new_sf, new_sb = _ret_call(lgf, lgb, q_ret, k_ret, v_ret, sg, ret_norm,
                                      None, None, latent=False)
    o_r_s = _ret_call(lgf, lgb, q_ret, k_ret, v_ret, sg, ret_norm,
                      state_ret_fwd, state_ret_bwd, latent=True)

    tm2, tn2 = 512, 1024
    (merged,) = _matmul_call(
        [(o_a_p, o_a_s), (o_r_p, o_r_s)], [(w_branch_attn[0], 0), (w_branch_ret[0], 0)],
        [(0, 0), (1, 1)], D_MODEL, tm2, tn2, _epi_merge,
        [(gates, _tile_spec(tm2, tn2)), (gates, _tile_spec(tm2, tn2, D_MODEL // tn2))],
        [(full(D_MODEL, BF16), _tile_spec(tm2, tn2))], "merge", stream=True)

    tm3 = 512
    n_p3 = T_P // tm3
    row_vec = pl.BlockSpec((1, D_MODEL), lambda n, m: (0, 0))
    x1, h2 = _matmul_call(
        [merged], [(w_out[0], 0)], [(0, 0)], D_MODEL, tm3, D_MODEL,
        functools.partial(_epi_residual_norm, tm=tm3),
        [(xp, pl.BlockSpec((tm3, D_MODEL), lambda n, m: (jnp.minimum(m, n_p3 - 1), 0))),
         (xs, pl.BlockSpec((tm3, D_MODEL), lambda n, m: (jnp.maximum(m - n_p3, 0), 0))),
         (mod3, _mod_spec(tm3, D_MODEL, lambda n: 2, 2)),
         (norm_ffn, row_vec),
         (mod3, _mod_spec(tm3, D_MODEL, lambda n: 4, 2)),
         (mod3, _mod_spec(tm3, D_MODEL, lambda n: 3, 2))],
        [(full(D_MODEL, F32), _tile_spec(tm3, D_MODEL)),
         (full(D_MODEL, BF16), _tile_spec(tm3, D_MODEL))], "out_proj", stream=True)

    (act,) = _matmul_call(
        [h2], [(w_ffn_gate[0], 0), (w_ffn_up[0], 0)], [(0, 0), (0, 1)],
        D_FF, 1024, 512, _epi_swiglu, [],
        [(full(D_FF, BF16), _tile_spec(1024, 512))], "ffn_up")

    tm4, tn4 = 512, 1024
    (y_pre,) = _matmul_call(
        [act], [(w_ffn_down[0], 0)], [(0, 0)], D_MODEL, tm4, tn4, _epi_residual,
        [(x1, _tile_spec(tm4, tn4)),
         (mod3, _mod_spec(tm4, tn4, lambda n: 5 * (D_MODEL // tn4) + n, 2))],
        [(full(D_MODEL, F32), _tile_spec(tm4, tn4))], "ffn_down", stream=True, kc=512)

    y_p, y_s = _final_call(y_pre, final_norm[None, :])

    return (y_p.reshape(BATCH, SEQ, D_MODEL),
            y_s.reshape(DEC_BATCH, DEC_SEQ, D_MODEL),
            k_new[:T_P].reshape(BATCH, 1, SEQ, N_KV_HEADS, HEAD_DIM),
            v_new[:T_P].reshape(BATCH, 1, SEQ, N_KV_HEADS, HEAD_DIM),
            new_sf, new_sb)
```

```python
import functools

import jax
import jax.numpy as jnp
from jax import lax
from jax.experimental import pallas as pl
from jax.experimental.pallas import tpu as pltpu

D_MODEL = 2048
BATCH = 16
SEQ = 256
DEC_BATCH = 8
DEC_SEQ = 1024
PAST_LEN = 512
GRID_W = 64
N_HEADS = 16
N_KV_HEADS = 4
HEAD_DIM = 128
ROPE_THETA = 10000.0
N_RET_HEADS = 8
RET_KEY_DIM = 128
RET_VAL_DIM = 256
D_FF = 5632
EPS = 1e-6

ATTN_Q = N_HEADS * HEAD_DIM
ATTN_KV = N_KV_HEADS * HEAD_DIM
RET_QK = N_RET_HEADS * RET_KEY_DIM
RET_V = N_RET_HEADS * RET_VAL_DIM
GQA_GROUP = N_HEADS // N_KV_HEADS

COL_Q = 0
COL_K = COL_Q + ATTN_Q
COL_V = COL_K + ATTN_KV
COL_QR = COL_V + ATTN_KV
COL_KR = COL_QR + RET_QK
COL_VR = COL_KR + RET_QK
COL_GR = COL_VR + RET_V
COL_GATES = COL_GR + RET_V

T_P = BATCH * SEQ
T_S = DEC_BATCH * DEC_SEQ
T = T_P + T_S
MOD_ROWS = 16

V7X_VMEM_BYTES = 64 * 1024 * 1024
VMEM_CAP = V7X_VMEM_BYTES - 6 * 1024 * 1024

BF16 = jnp.bfloat16
F32 = jnp.float32


def _nbytes(shape, dtype):
    n = 1
    for s in shape:
        if s is not None:
            n *= s
    return n * jnp.dtype(dtype).itemsize


def _params(vmem_estimate, n_grid):
    limit = min(VMEM_CAP, int(vmem_estimate) + 8 * 1024 * 1024)
    return pltpu.CompilerParams(
        dimension_semantics=("arbitrary",) * n_grid, vmem_limit_bytes=limit)


def _mod_row(m, tm):
    n_p = T_P // tm
    return jnp.where(m < n_p, 0, 1 + (m - n_p) // (DEC_SEQ // tm))


def _mod_spec(tm, width, col_block, grid_rank):
    if grid_rank == 1:
        return pl.BlockSpec((None, 1, width), lambda m: (_mod_row(m, tm), 0, col_block(0)))
    return pl.BlockSpec((None, 1, width), lambda n, m: (_mod_row(m, tm), 0, col_block(n)))


def _mod_body(c_ref, w_ref, b_ref, o_ref):
    a = jax.nn.silu(c_ref[...]).astype(BF16)
    w = w_ref[...].astype(BF16)
    o_ref[...] = jnp.dot(a, w, preferred_element_type=F32) + b_ref[...]


def _mod_call(c_all, w_mod, b_mod):
    tn = 1024
    n_out = 6 * D_MODEL
    est = 2 * _nbytes((D_MODEL, tn), F32) + _nbytes((D_MODEL, tn), BF16)
    return pl.pallas_call(
        _mod_body,
        grid=(n_out // tn,),
        in_specs=[
            pl.BlockSpec((MOD_ROWS, D_MODEL), lambda n: (0, 0)),
            pl.BlockSpec((D_MODEL, tn), lambda n: (0, n)),
            pl.BlockSpec((1, tn), lambda n: (0, n)),
        ],
        out_specs=pl.BlockSpec((MOD_ROWS, tn), lambda n: (0, n)),
        out_shape=jax.ShapeDtypeStruct((MOD_ROWS, n_out), F32),
        compiler_params=_params(est, 1),
        name="mod_table",
    )(c_all, w_mod, b_mod)


def _modulated_norm(x, g, sc, sh):
    y = x * lax.rsqrt(jnp.mean(x * x, axis=-1, keepdims=True) + EPS)
    return (y * g) * (1.0 + sc) + sh


def _prenorm_body(xp_ref, xs_ref, g_ref, sc_ref, sh_ref, o_ref, *, n_p):
    m = pl.program_id(0)

    @pl.when(m < n_p)
    def _():
        o_ref[...] = _modulated_norm(xp_ref[...], g_ref[...], sc_ref[...], sh_ref[...]).astype(BF16)

    @pl.when(m >= n_p)
    def _():
        o_ref[...] = _modulated_norm(xs_ref[...], g_ref[...], sc_ref[...], sh_ref[...]).astype(BF16)


def _prenorm_call(xp, xs, norm_w, mod3, sh_blk, sc_blk, name):
    tm = 1024
    n_p = T_P // tm
    blk = (tm, D_MODEL)
    in_specs = [
        pl.BlockSpec(blk, lambda m: (jnp.minimum(m, n_p - 1), 0)),
        pl.BlockSpec(blk, lambda m: (jnp.maximum(m - n_p, 0), 0)),
        pl.BlockSpec((1, D_MODEL), lambda m: (0, 0)),
        _mod_spec(tm, D_MODEL, lambda n: sc_blk, 1),
        _mod_spec(tm, D_MODEL, lambda n: sh_blk, 1),
    ]
    est = 4 * _nbytes(blk, F32) + 2 * _nbytes(blk, BF16) + _nbytes(blk, F32)
    return pl.pallas_call(
        functools.partial(_prenorm_body, n_p=n_p),
        grid=(T // tm,),
        in_specs=in_specs,
        out_specs=pl.BlockSpec(blk, lambda m: (m, 0)),
        out_shape=jax.ShapeDtypeStruct((T, D_MODEL), BF16),
        compiler_params=_params(est, 1),
        name=name,
    )(xp, xs, norm_w, mod3, mod3)


def _matmul_body(*refs, lhs_split, n_rhs, col0s, pairs, n_extra, n_out, epilogue,
                 tm, tn, kc, row_chunk, stream, single_tile):
    n_lhs_refs = sum(2 if s else 1 for s in lhs_split)
    lhs_refs = refs[:n_lhs_refs]
    pos = n_lhs_refs
    rhs = refs[pos:pos + n_rhs]
    pos += n_rhs
    extras = refs[pos:pos + n_extra]
    pos += n_extra
    outs = refs[pos:pos + n_out]
    pos += n_out
    wb = refs[pos:pos + n_rhs]
    n, m = pl.program_id(0), pl.program_id(1)
    is_prompt = m < T_P // tm

    def lhs_rows(i, rows):
        first = sum(2 if s else 1 for s in lhs_split[:i])
        if lhs_split[i]:
            return jnp.where(is_prompt, lhs_refs[first][rows, :], lhs_refs[first + 1][rows, :])
        return lhs_refs[first][rows, :]

    def compute(weight):
        for r in range(tm // row_chunk):
            rows = slice(r * row_chunk, (r + 1) * row_chunk)
            zs = [jnp.dot(lhs_rows(i, rows), weight(j), preferred_element_type=F32) for i, j in pairs]
            epilogue(zs, extras, outs, rows)

    if not stream:
        @pl.when(m == 0)
        def _():
            for w_ref, wb_ref in zip(rhs, wb):
                wb_ref[...] = w_ref[...].astype(BF16)

        compute(lambda j: wb[j][...])
        return

    stage = refs[pos + n_rhs:pos + 2 * n_rhs]
    sems = refs[pos + 2 * n_rhs:pos + 3 * n_rhs]
    n_tiles = pl.num_programs(0)
    n_chunks = wb[0].shape[1] // kc

    def chunk_copy(j, tile, chunk, slot):
        col = pl.multiple_of(col0s[j] + tile * tn, tn)
        row = pl.multiple_of(chunk * kc, kc)
        return pltpu.make_async_copy(
            rhs[j].at[pl.ds(row, kc), pl.ds(col, tn)], stage[j].at[slot], sems[j].at[slot])

    nxt = jnp.minimum(n + 1, n_tiles - 1)
    chunk = jnp.minimum(m, n_chunks - 1)
    first = jnp.logical_and(n == 0, m == 0)

    def start_prefetch():
        for j in range(n_rhs):
            chunk_copy(j, nxt, chunk, 0).start()

    @pl.when(first)
    def _():
        for j in range(n_rhs):
            chunk_copy(j, 0, 0, 0).start()
            for c in range(n_chunks):
                if c + 1 < n_chunks:
                    chunk_copy(j, 0, c + 1, (c + 1) % 2).start()
                chunk_copy(j, 0, c, c % 2).wait()
                wb[j][0, c * kc:(c + 1) * kc, :] = stage[j][c % 2].astype(BF16)
        if not single_tile:
            start_prefetch()

    if single_tile:
        compute(lambda j: wb[j][0])
        return

    pl.when(jnp.logical_not(first))(start_prefetch)

    for cur in range(2):
        @pl.when(n % 2 == cur)
        def _(cur=cur):
            compute(lambda j: wb[j][cur])
            for j in range(n_rhs):
                chunk_copy(j, nxt, chunk, 0).wait()
                wb[j][1 - cur, pl.ds(pl.multiple_of(chunk * kc, kc), kc), :] = stage[j][0].astype(BF16)


def _matmul_call(lhs_list, rhs_list, pairs, n_cols, tm, tn, epilogue, extras, outs, name,
                 row_chunk=256, stream=False, kc=256):
    lhs_split = tuple(isinstance(a, tuple) for a in lhs_list)
    k_dim = (lhs_list[0][0] if lhs_split[0] else lhs_list[0]).shape[1]
    n_p = T_P // tm
    grid = (n_cols // tn, T // tm)
    in_specs, lhs_args = [], []
    for a in lhs_list:
        if isinstance(a, tuple):
            in_specs.append(pl.BlockSpec((tm, k_dim), lambda n, m: (jnp.minimum(m, n_p - 1), 0)))
            in_specs.append(pl.BlockSpec((tm, k_dim), lambda n, m: (jnp.maximum(m - n_p, 0), 0)))
            lhs_args += list(a)
        else:
            in_specs.append(pl.BlockSpec((tm, k_dim), lambda n, m: (m, 0)))
            lhs_args.append(a)
    est = 2 * len(lhs_args) * _nbytes((tm, k_dim), BF16)
    single_tile = stream and grid[0] == 1
    if stream:
        assert k_dim % kc == 0 and k_dim // kc <= grid[1], (k_dim, kc, grid)
        n_buf = 1 if single_tile else 2
        in_specs += [pl.BlockSpec(memory_space=pl.ANY) for _ in rhs_list]
        est += len(rhs_list) * (n_buf * _nbytes((k_dim, tn), BF16) + 2 * _nbytes((kc, tn), F32))
        scratch = [pltpu.VMEM((n_buf, k_dim, tn), BF16) for _ in rhs_list]
        scratch += [pltpu.VMEM((2, kc, tn), F32) for _ in rhs_list]
        scratch += [pltpu.SemaphoreType.DMA((2,)) for _ in rhs_list]
    else:
        for _, col0 in rhs_list:
            in_specs.append(pl.BlockSpec((k_dim, tn), lambda n, m, c=col0 // tn: (0, c + n)))
        est += len(rhs_list) * (2 * _nbytes((k_dim, tn), F32) + _nbytes((k_dim, tn), BF16))
        scratch = [pltpu.VMEM((k_dim, tn), BF16) for _ in rhs_list]
    in_specs += [spec for _, spec in extras]
    for arr, spec in list(extras) + list(outs):
        est += 2 * _nbytes(spec.block_shape, arr.dtype)
    est += 3 * len(pairs) * _nbytes((row_chunk, tn), F32)
    body = functools.partial(
        _matmul_body, lhs_split=lhs_split, n_rhs=len(rhs_list),
        col0s=tuple(c for _, c in rhs_list), pairs=tuple(pairs),
        n_extra=len(extras), n_out=len(outs), epilogue=epilogue,
        tm=tm, tn=tn, kc=kc, row_chunk=row_chunk, stream=stream, single_tile=single_tile)
    res = pl.pallas_call(
        body,
        grid=grid,
        in_specs=in_specs,
        out_specs=[spec for _, spec in outs],
        out_shape=[s for s, _ in outs],
        scratch_shapes=scratch,
        compiler_params=_params(est, 2),
        name=name,
    )(*lhs_args, *[w for w, _ in rhs_list], *[a for a, _ in extras])
    return res


def _tile_spec(tm, tn, col_block0=0):
    return pl.BlockSpec((tm, tn), lambda n, m, c=col_block0: (m, c + n))


def _prompt_tile_spec(tm, tn):
    n_p = T_P // tm
    return pl.BlockSpec((tm, tn), lambda n, m: (jnp.minimum(m, n_p), n))


def _head_rms(x, g):
    return x * lax.rsqrt(jnp.mean(x * x, axis=-1, keepdims=True) + EPS) * g


def _rope(y, c, s_lo, s_hi):
    return y * c + pltpu.roll(y, HEAD_DIM - 32, axis=1) * s_lo + pltpu.roll(y, 32, axis=1) * s_hi


def _epi_q(zs, extras, outs, rows, *, tn):
    g_ref, c_ref, slo_ref, shi_ref = extras
    (q_ref,) = outs
    (z,) = zs
    for h in range(tn // HEAD_DIM):
        sl = slice(h * HEAD_DIM, (h + 1) * HEAD_DIM)
        y = _head_rms(z[:, sl], g_ref[...])
        q_ref[rows, sl] = _rope(y, c_ref[rows, :], slo_ref[rows, :], shi_ref[rows, :]).astype(BF16)


def _epi_k(zs, extras, outs, rows, *, tn):
    g_ref, c_ref, slo_ref, shi_ref = extras
    k_att_ref, k_new_ref = outs
    (z,) = zs
    for h in range(tn // HEAD_DIM):
        sl = slice(h * HEAD_DIM, (h + 1) * HEAD_DIM)
        y = _head_rms(z[:, sl], g_ref[...])
        k_new_ref[rows, sl] = y
        k_att_ref[rows, sl] = _rope(y, c_ref[rows, :], slo_ref[rows, :], shi_ref[rows, :]).astype(BF16)


def _epi_v(zs, extras, outs, rows):
    v_att_ref, v_new_ref = outs
    v_att_ref[rows, :] = zs[0].astype(BF16)
    v_new_ref[rows, :] = zs[0]


def _epi_cast(zs, extras, outs, rows):
    outs[0][rows, :] = zs[0].astype(outs[0].dtype)


def _epi_scale(zs, extras, outs, rows, *, scale):
    outs[0][rows, :] = zs[0] * scale


def _epi_silu(zs, extras, outs, rows):
    outs[0][rows, :] = jax.nn.silu(zs[0])


def _epi_sigmoid(zs, extras, outs, rows):
    outs[0][rows, :] = jax.nn.sigmoid(zs[0])


def _epi_merge(zs, extras, outs, rows):
    ga_ref, gr_ref = extras
    outs[0][rows, :] = (ga_ref[rows, :] * zs[0] + gr_ref[rows, :] * zs[1]).astype(BF16)


def _epi_residual_norm(zs, extras, outs, rows, *, tm):
    xp_ref, xs_ref, gate_ref, g_ref, sc_ref, sh_ref = extras
    x1_ref, h_ref = outs
    x = jnp.where(pl.program_id(1) < T_P // tm, xp_ref[rows, :], xs_ref[rows, :])
    x1 = x + gate_ref[...] * zs[0]
    x1_ref[rows, :] = x1
    h_ref[rows, :] = _modulated_norm(x1, g_ref[...], sc_ref[...], sh_ref[...]).astype(BF16)


def _epi_swiglu(zs, extras, outs, rows):
    outs[0][rows, :] = (jax.nn.silu(zs[0]) * zs[1]).astype(BF16)


def _epi_residual(zs, extras, outs, rows):
    x_ref, gate_ref = extras
    outs[0][rows, :] = x_ref[rows, :] + gate_ref[...] * zs[0]


_NT = (((1,), (1,)), ((), ()))


_SOFTMAX_EXP2_SCALE = HEAD_DIM ** -0.5 * 1.4426950408889634


def _attn_body(*refs, has_ctx, seq, tq, n_sub):
    if has_ctx:
        q_ref, k_ref, v_ref, ck_ref, cv_ref, o_ref = refs
        g = pl.program_id(1)
        ck = ck_ref[:, g, :].astype(BF16)
        cv = cv_ref[:, g, :].astype(BF16)
    else:
        q_ref, k_ref, v_ref, o_ref = refs
    chain_rows = tq
    for i, c in [(i, c) for i in range(n_sub) for c in range(tq // chain_rows)]:
        k = k_ref[i * seq:(i + 1) * seq, :]
        v = v_ref[i * seq:(i + 1) * seq, :]
        rows = slice(i * tq + c * chain_rows, i * tq + (c + 1) * chain_rows)
        for h in range(GQA_GROUP):
            sl = slice(h * HEAD_DIM, (h + 1) * HEAD_DIM)
            q = q_ref[rows, sl]
            s_new = lax.dot_general(q, k, _NT, preferred_element_type=F32)
            mx = jnp.max(s_new, axis=-1, keepdims=True)
            if has_ctx:
                s_ctx = lax.dot_general(q, ck, _NT, preferred_element_type=F32)
                mx = jnp.maximum(mx, jnp.max(s_ctx, axis=-1, keepdims=True))
                p_ctx = jnp.exp2((s_ctx - mx) * _SOFTMAX_EXP2_SCALE)
                p_new = jnp.exp2((s_new - mx) * _SOFTMAX_EXP2_SCALE)
                den = jnp.sum(p_ctx, axis=-1, keepdims=True) + jnp.sum(p_new, axis=-1, keepdims=True)
                o = jnp.dot(p_ctx.astype(BF16), cv, preferred_element_type=F32)
                o = o + jnp.dot(p_new.astype(BF16), v, preferred_element_type=F32)
            else:
                p_new = jnp.exp2((s_new - mx) * _SOFTMAX_EXP2_SCALE)
                den = jnp.sum(p_new, axis=-1, keepdims=True)
                o = jnp.dot(p_new.astype(BF16), v, preferred_element_type=F32)
            o_ref[rows, sl] = (o * (1.0 / den)).astype(BF16)


def _attn_call(q_att, k_att, v_att, ctx_k, ctx_v, *, latent):
    gw = GQA_GROUP * HEAD_DIM
    if latent:
        seq, nb, tq, n_sub = DEC_SEQ, DEC_BATCH, 512, 1
        row0 = T_P
    else:
        seq, nb, tq, n_sub = SEQ, BATCH, SEQ, 4
        row0 = 0
    nqt = seq // tq
    q_rows, kv_rows = n_sub * tq, n_sub * seq
    q_spec = pl.BlockSpec((q_rows, gw), lambda b, g, t: (row0 // q_rows + b * nqt + t, g))
    o_spec = pl.BlockSpec((q_rows, gw), lambda b, g, t: (b * nqt + t, g))
    kv_spec = pl.BlockSpec((kv_rows, HEAD_DIM), lambda b, g, t: (row0 // kv_rows + b, g))
    in_specs = [q_spec, kv_spec, kv_spec]
    args = [q_att, k_att, v_att]
    n_keys = seq
    if latent:
        c_spec = pl.BlockSpec((None, None, PAST_LEN, N_KV_HEADS, HEAD_DIM),
                              lambda b, g, t: (b, 0, 0, 0, 0))
        in_specs += [c_spec, c_spec]
        args += [ctx_k, ctx_v]
        n_keys += PAST_LEN
    est = 4 * _nbytes((q_rows, gw), BF16) + 4 * _nbytes((kv_rows, HEAD_DIM), BF16)
    est += 4 * _nbytes((PAST_LEN, 8, HEAD_DIM), F32)
    est += 3 * GQA_GROUP * n_sub * _nbytes((tq, n_keys), F32)
    return pl.pallas_call(
        functools.partial(_attn_body, has_ctx=latent, seq=seq, tq=tq, n_sub=n_sub),
        grid=(nb // n_sub, N_KV_HEADS, nqt),
        in_specs=in_specs,
        out_specs=o_spec,
        out_shape=jax.ShapeDtypeStruct((nb * seq, ATTN_Q), BF16),
        compiler_params=_params(est, 3),
        name="attn_latent" if latent else "attn_prompt",
    )(*args)


_TN = (((0,), (0,)), ((), ()))


def _ret_body(*refs, seq, tq, latent, n_sub):
    if latent:
        (lgf_ref, lgb_ref, q_ref, k_ref, v_ref, sg_ref, rn_ref, s0f_ref, s0b_ref,
         o_ref, d_ref, xif_ref, xib_ref) = refs
    else:
        (lgf_ref, lgb_ref, q_ref, k_ref, v_ref, sg_ref, rn_ref,
         o_ref, sf_ref, sb_ref, d_ref) = refs
    h = pl.program_id(0)
    lgf = lgf_ref[h]
    lgb = lgb_ref[h]

    @pl.when(pl.program_id(1) == 0)
    def _():
        for t in range(seq // tq):
            i = lax.broadcasted_iota(jnp.int32, (tq, seq), 0) + t * tq
            j = lax.broadcasted_iota(jnp.int32, (tq, seq), 1)
            diff = (i - j).astype(F32)
            arg = jnp.where(diff >= 0, diff * lgf, -diff * lgb)
            d_ref[t * tq:(t + 1) * tq, :] = jnp.exp(arg) * jnp.where(diff == 0, 2.0, 1.0)
        if latent:
            pos = lax.broadcasted_iota(jnp.int32, (seq, RET_VAL_DIM), 0).astype(F32)
            xif_ref[...] = jnp.exp((pos + 1.0) * lgf)
            xib_ref[...] = jnp.exp((seq - pos) * lgb)

    if latent:
        s0f = s0f_ref[...].astype(BF16)
        s0b = s0b_ref[...].astype(BF16)
    for i in range(n_sub):
        k_f32 = k_ref[i * seq:(i + 1) * seq, :]
        kb = k_f32.astype(BF16)
        v = v_ref[i * seq:(i + 1) * seq, :]
        for t in range(seq // tq):
            rows = slice(i * seq + t * tq, i * seq + (t + 1) * tq)
            q = q_ref[rows, :]
            raw = lax.dot_general(q, kb, _NT, preferred_element_type=F32)
            p = (raw * d_ref[t * tq:(t + 1) * tq, :]).astype(BF16)
            o = jnp.dot(p, v, preferred_element_type=F32)
            if latent:
                tile = slice(t * tq, (t + 1) * tq)
                o = o + jnp.dot(q, s0f, preferred_element_type=F32) * xif_ref[tile, :]
                o = o + jnp.dot(q, s0b, preferred_element_type=F32) * xib_ref[tile, :]
            mu = jnp.mean(o, axis=-1, keepdims=True)
            oc = o - mu
            var = jnp.mean(oc * oc, axis=-1, keepdims=True)
            y = (oc * lax.rsqrt(var + EPS)) * rn_ref[...]
            o_ref[rows, :] = (y * sg_ref[rows, :]).astype(BF16)
        if not latent:
            j = lax.broadcasted_iota(jnp.int32, (seq, RET_KEY_DIM), 0).astype(F32)
            kz_f = (k_f32 * jnp.exp((seq - 1.0 - j) * lgf)).astype(BF16)
            kz_b = (k_f32 * jnp.exp(j * lgb)).astype(BF16)
            sf_ref[i] = lax.dot_general(kz_f, v, _TN, preferred_element_type=F32)
            sb_ref[i] = lax.dot_general(kz_b, v, _TN, preferred_element_type=F32)


def _ret_call(lgf, lgb, q_ret, k_ret, v_ret, sg, ret_norm, s0f, s0b, *, latent):
    if latent:
        seq, nb, row0, n_sub = DEC_SEQ, DEC_BATCH, T_P, 1
    else:
        seq, nb, row0, n_sub = SEQ, BATCH, 0, 8
    tq = 256
    rows = n_sub * seq
    rb = row0 // rows
    smem = pl.BlockSpec(memory_space=pltpu.SMEM)
    qk_spec = pl.BlockSpec((rows, RET_KEY_DIM), lambda h, b: (rb + b, h))
    v_spec = pl.BlockSpec((rows, RET_VAL_DIM), lambda h, b: (rb + b, h))
    o_spec = pl.BlockSpec((rows, RET_VAL_DIM), lambda h, b: (b, h))
    in_specs = [smem, smem, qk_spec, qk_spec, v_spec, v_spec,
                pl.BlockSpec((1, RET_VAL_DIM), lambda h, b: (0, h))]
    args = [lgf, lgb, q_ret, k_ret, v_ret, sg, ret_norm]
    o_shape = jax.ShapeDtypeStruct((nb * seq, RET_V), BF16)
    if latent:
        st_spec = pl.BlockSpec((None, None, None, RET_KEY_DIM, RET_VAL_DIM),
                               lambda h, b: (b, 0, h, 0, 0))
        in_specs += [st_spec, st_spec]
        args += [s0f, s0b]
        out_specs = o_spec
        out_shape = o_shape
    else:
        st_spec = pl.BlockSpec((n_sub, None, None, RET_KEY_DIM, RET_VAL_DIM),
                               lambda h, b: (b, 0, h, 0, 0))
        st_shape = jax.ShapeDtypeStruct((BATCH, 1, N_RET_HEADS, RET_KEY_DIM, RET_VAL_DIM), F32)
        out_specs = [o_spec, st_spec, st_spec]
        out_shape = [o_shape, st_shape, st_shape]
    est = _nbytes((seq, seq), F32) + 4 * n_sub * _nbytes((tq, seq), F32)
    est += 2 * (_nbytes((rows, RET_KEY_DIM), BF16) + _nbytes((rows, RET_KEY_DIM), F32))
    est += 2 * (2 * _nbytes((rows, RET_VAL_DIM), BF16) + _nbytes((rows, RET_VAL_DIM), F32))
    est += 8 * n_sub * _nbytes((RET_KEY_DIM, RET_VAL_DIM), F32)
    return pl.pallas_call(
        functools.partial(_ret_body, seq=seq, tq=tq, latent=latent, n_sub=n_sub),
        grid=(N_RET_HEADS, nb // n_sub),
        in_specs=in_specs,
        out_specs=out_specs,
        out_shape=out_shape,
        scratch_shapes=[pltpu.VMEM((seq, seq), F32)]
        + ([pltpu.VMEM((seq, RET_VAL_DIM), F32)] * 2 if latent else []),
        compiler_params=_params(est, 2),
        name="ret_latent" if latent else "ret_prompt",
    )(*args)


def _final_body(x_ref, g_ref, yp_ref, ys_ref, *, n_p):
    x = x_ref[...]
    y = x * lax.rsqrt(jnp.mean(x * x, axis=-1, keepdims=True) + EPS) * g_ref[...]
    m = pl.program_id(0)

    @pl.when(m < n_p)
    def _():
        yp_ref[...] = y

    @pl.when(m >= n_p)
    def _():
        ys_ref[...] = y


def _final_call(x, g):
    tm = 1024
    n_p = T_P // tm
    blk = (tm, D_MODEL)
    est = 6 * _nbytes(blk, F32) + _nbytes(blk, F32)
    return pl.pallas_call(
        functools.partial(_final_body, n_p=n_p),
        grid=(T // tm,),
        in_specs=[pl.BlockSpec(blk, lambda m: (m, 0)), pl.BlockSpec((1, D_MODEL), lambda m: (0, 0))],
        out_specs=[
            pl.BlockSpec(blk, lambda m: (jnp.minimum(m, n_p - 1), 0)),
            pl.BlockSpec(blk, lambda m: (jnp.maximum(m - n_p, 0), 0)),
        ],
        out_shape=[jax.ShapeDtypeStruct((T_P, D_MODEL), F32), jax.ShapeDtypeStruct((T_S, D_MODEL), F32)],
        compiler_params=_params(est, 1),
        name="final_norm",
    )(x, g)


def _rope_tables():
    rows = DEC_SEQ // GRID_W
    row = jnp.repeat(jnp.arange(rows, dtype=F32), GRID_W)
    col = jnp.tile(jnp.arange(GRID_W, dtype=F32), rows)
    quarter = HEAD_DIM // 4
    inv_freq = ROPE_THETA ** (-jnp.arange(quarter, dtype=F32) / quarter)
    ang_r = row[:, None] * inv_freq[None, :]
    ang_c = col[:, None] * inv_freq[None, :]
    cr, sr, cc, sc = jnp.cos(ang_r), jnp.sin(ang_r), jnp.cos(ang_c), jnp.sin(ang_c)
    zero = jnp.zeros_like(sr)
    cos = jnp.concatenate([cr, cr, cc, cc], axis=-1)
    s_lo = jnp.concatenate([-sr, zero, -sc, zero], axis=-1)
    s_hi = jnp.concatenate([zero, sr, zero, sc], axis=-1)
    ident = jnp.zeros((DEC_SEQ, HEAD_DIM), F32)
    return (jnp.concatenate([ident + 1.0, cos], axis=0),
            jnp.concatenate([ident, s_lo], axis=0),
            jnp.concatenate([ident, s_hi], axis=0))


def kernel(x_prompt, x_sample, cache_attn_k, cache_attn_v, state_ret_fwd, state_ret_bwd, c, c_ctx,
           norm_attn, norm_ffn, w_mod, b_mod, w_in, q_norm, k_norm, ret_decay_fwd, ret_decay_bwd,
           ret_norm, w_branch_attn, w_branch_ret, w_out, w_ffn_gate, w_ffn_up, w_ffn_down, final_norm):
    xp = x_prompt.reshape(T_P, D_MODEL)
    xs = x_sample.reshape(T_S, D_MODEL)
    lgf = jax.nn.log_sigmoid(ret_decay_fwd[0].astype(F32))
    lgb = jax.nn.log_sigmoid(ret_decay_bwd[0].astype(F32))
    cos, s_lo, s_hi = _rope_tables()

    c_all = jnp.concatenate(
        [c_ctx[None, :], c, jnp.zeros((MOD_ROWS - 1 - DEC_BATCH, D_MODEL), F32)], axis=0)
    mod = _mod_call(c_all, w_mod[0], b_mod)
    mod3 = mod.reshape(MOD_ROWS, 1, 6 * D_MODEL)
    h = _prenorm_call(xp, xs, norm_attn, mod3, 0, 1, "prenorm_attn")

    w_in0 = w_in[0]
    tm = 1024
    tbl = pl.BlockSpec((DEC_SEQ, HEAD_DIM), lambda n, m: (jnp.where(m < T_P // tm, 0, 1), 0))
    hd = pl.BlockSpec((1, HEAD_DIM), lambda n, m: (0, 0))
    rope_extras = [(cos, tbl), (s_lo, tbl), (s_hi, tbl)]

    def proj(col0, n_cols, tn, epilogue, extras, outs, name):
        return _matmul_call([h], [(w_in0, col0)], [(0, 0)], n_cols, tm, tn, epilogue, extras, outs, name)

    def full(n_cols, dtype):
        return jax.ShapeDtypeStruct((T, n_cols), dtype)

    (q_att,) = proj(COL_Q, ATTN_Q, 1024, functools.partial(_epi_q, tn=1024),
                    [(q_norm, hd)] + rope_extras,
                    [(full(ATTN_Q, BF16), _tile_spec(tm, 1024))], "proj_q")
    prompt_kv = jax.ShapeDtypeStruct((T_P + tm, ATTN_KV), F32)
    k_att, k_new = proj(COL_K, ATTN_KV, ATTN_KV, functools.partial(_epi_k, tn=ATTN_KV),
                        [(k_norm, hd)] + rope_extras,
                        [(full(ATTN_KV, BF16), _tile_spec(tm, ATTN_KV)),
                         (prompt_kv, _prompt_tile_spec(tm, ATTN_KV))], "proj_k")
    v_att, v_new = proj(COL_V, ATTN_KV, ATTN_KV, _epi_v, [],
                        [(full(ATTN_KV, BF16), _tile_spec(tm, ATTN_KV)),
                         (prompt_kv, _prompt_tile_spec(tm, ATTN_KV))], "proj_v")
    (q_ret,) = proj(COL_QR, RET_QK, 1024, _epi_cast, [],
                    [(full(RET_QK, BF16), _tile_spec(tm, 1024))], "proj_qr")
    (k_ret,) = proj(COL_KR, RET_QK, 1024, functools.partial(_epi_scale, scale=RET_KEY_DIM ** -0.5), [],
                    [(full(RET_QK, F32), _tile_spec(tm, 1024))], "proj_kr")
    (v_ret,) = proj(COL_VR, RET_V, 1024, _epi_cast, [],
                    [(full(RET_V, BF16), _tile_spec(tm, 1024))], "proj_vr")
    (sg,) = proj(COL_GR, RET_V, 1024, _epi_silu, [],
                 [(full(RET_V, F32), _tile_spec(tm, 1024))], "proj_gr")
    (gates,) = proj(COL_GATES, 2 * D_MODEL, 1024, _epi_sigmoid, [],
                    [(full(2 * D_MODEL, F32), _tile_spec(tm, 1024))], "proj_gates")

    o_a_p = _attn_call(q_att, k_att, v_att, None, None, latent=False)
    o_a_s = _attn_call(q_att, k_att, v_att, cache_attn_k, cache_attn_v, latent=True)
    o_r_p, new_sf, new_sb = _ret_call(lgf, lgb, q_ret, k_ret, v_ret, sg, ret_norm,
                                      None, None, latent=False)
    o_r_s = _ret_call(lgf, lgb, q_ret, k_ret, v_ret, sg, ret_norm,
                      state_ret_fwd, state_ret_bwd, latent=True)

    tm2, tn2 = 512, 1024
    (merged,) = _matmul_call(
        [(o_a_p, o_a_s), (o_r_p, o_r_s)], [(w_branch_attn[0], 0), (w_branch_ret[0], 0)],
        [(0, 0), (1, 1)], D_MODEL, tm2, tn2, _epi_merge,
        [(gates, _tile_spec(tm2, tn2)), (gates, _tile_spec(tm2, tn2, D_MODEL // tn2))],
        [(full(D_MODEL, BF16), _tile_spec(tm2, tn2))], "merge", stream=True)

    tm3 = 512
    n_p3 = T_P // tm3
    row_vec = pl.BlockSpec((1, D_MODEL), lambda n, m: (0, 0))
    x1, h2 = _matmul_call(
        [merged], [(w_out[0], 0)], [(0, 0)], D_MODEL, tm3, D_MODEL,
        functools.partial(_epi_residual_norm, tm=tm3),
        [(xp, pl.BlockSpec((tm3, D_MODEL), lambda n, m: (jnp.minimum(m, n_p3 - 1), 0))),
         (xs, pl.BlockSpec((tm3, D_MODEL), lambda n, m: (jnp.maximum(m - n_p3, 0), 0))),
         (mod3, _mod_spec(tm3, D_MODEL, lambda n: 2, 2)),
         (norm_ffn, row_vec),
         (mod3, _mod_spec(tm3, D_MODEL, lambda n: 4, 2)),
         (mod3, _mod_spec(tm3, D_MODEL, lambda n: 3, 2))],
        [(full(D_MODEL, F32), _tile_spec(tm3, D_MODEL)),
         (full(D_MODEL, BF16), _tile_spec(tm3, D_MODEL))], "out_proj", stream=True)

    (act,) = _matmul_call(
        [h2], [(w_ffn_gate[0], 0), (w_ffn_up[0], 0)], [(0, 0), (0, 1)],
        D_FF, 1024, 512, _epi_swiglu, [],
        [(full(D_FF, BF16), _tile_spec(1024, 512))], "ffn_up")

    tm4, tn4 = 512, 1024
    (y_pre,) = _matmul_call(
        [act], [(w_ffn_down[0], 0)], [(0, 0)], D_MODEL, tm4, tn4, _epi_residual,
        [(x1, _tile_spec(tm4, tn4)),
         (mod3, _mod_spec(tm4, tn4, lambda n: 5 * (D_MODEL // tn4) + n, 2))],
        [(full(D_MODEL, F32), _tile_spec(tm4, tn4))], "ffn_down", stream=True, kc=512)

    y_p, y_s = _final_call(y_pre, final_norm[None, :])

    return (y_p.reshape(BATCH, SEQ, D_MODEL),
            y_s.reshape(DEC_BATCH, DEC_SEQ, D_MODEL),
            k_new[:T_P].reshape(BATCH, 1, SEQ, N_KV_HEADS, HEAD_DIM),
            v_new[:T_P].reshape(BATCH, 1, SEQ, N_KV_HEADS, HEAD_DIM),
            new_sf, new_sb)
```

```python
import functools

import jax
import jax.numpy as jnp
from jax import lax
from jax.experimental import pallas as pl
from jax.experimental.pallas import tpu as pltpu

D_MODEL = 2048
BATCH = 16
SEQ = 256
DEC_BATCH = 8
DEC_SEQ = 1024
PAST_LEN = 512
GRID_W = 64
N_HEADS = 16
N_KV_HEADS = 4
HEAD_DIM = 128
ROPE_THETA = 10000.0
N_RET_HEADS = 8
RET_KEY_DIM = 128
RET_VAL_DIM = 256
D_FF = 5632
EPS = 1e-6

ATTN_Q = N_HEADS * HEAD_DIM
ATTN_KV = N_KV_HEADS * HEAD_DIM
RET_QK = N_RET_HEADS * RET_KEY_DIM
RET_V = N_RET_HEADS * RET_VAL_DIM
GQA_GROUP = N_HEADS // N_KV_HEADS

COL_Q = 0
COL_K = COL_Q + ATTN_Q
COL_V = COL_K + ATTN_KV
COL_QR = COL_V + ATTN_KV
COL_KR = COL_QR + RET_QK
COL_VR = COL_KR + RET_QK
COL_GR = COL_VR + RET_V
COL_GATES = COL_GR + RET_V

T_P = BATCH * SEQ
T_S = DEC_BATCH * DEC_SEQ
T = T_P + T_S
MOD_ROWS = 16

V7X_VMEM_BYTES = 64 * 1024 * 1024
VMEM_CAP = V7X_VMEM_BYTES - 6 * 1024 * 1024

BF16 = jnp.bfloat16
F32 = jnp.float32


def _nbytes(shape, dtype):
    n = 1
    for s in shape:
        if s is not None:
            n *= s
    return n * jnp.dtype(dtype).itemsize


def _params(vmem_estimate, n_grid):
    limit = min(VMEM_CAP, int(vmem_estimate) + 8 * 1024 * 1024)
    return pltpu.CompilerParams(
        dimension_semantics=("arbitrary",) * n_grid, vmem_limit_bytes=limit)


def _mod_row(m, tm):
    n_p = T_P // tm
    return jnp.where(m < n_p, 0, 1 + (m - n_p) // (DEC_SEQ // tm))


def _mod_spec(tm, width, col_block, grid_rank):
    if grid_rank == 1:
        return pl.BlockSpec((None, 1, width), lambda m: (_mod_row(m, tm), 0, col_block(0)))
    return pl.BlockSpec((None, 1, width), lambda n, m: (_mod_row(m, tm), 0, col_block(n)))


def _mod_body(c_ref, w_ref, b_ref, o_ref):
    a = jax.nn.silu(c_ref[...]).astype(BF16)
    w = w_ref[...].astype(BF16)
    o_ref[...] = jnp.dot(a, w, preferred_element_type=F32) + b_ref[...]


def _mod_call(c_all, w_mod, b_mod):
    tn = 1024
    n_out = 6 * D_MODEL
    est = 2 * _nbytes((D_MODEL, tn), F32) + _nbytes((D_MODEL, tn), BF16)
    return pl.pallas_call(
        _mod_body,
        grid=(n_out // tn,),
        in_specs=[
            pl.BlockSpec((MOD_ROWS, D_MODEL), lambda n: (0, 0)),
            pl.BlockSpec((D_MODEL, tn), lambda n: (0, n)),
            pl.BlockSpec((1, tn), lambda n: (0, n)),
        ],
        out_specs=pl.BlockSpec((MOD_ROWS, tn), lambda n: (0, n)),
        out_shape=jax.ShapeDtypeStruct((MOD_ROWS, n_out), F32),
        compiler_params=_params(est, 1),
        name="mod_table",
    )(c_all, w_mod, b_mod)


def _modulated_norm(x, g, sc, sh):
    y = x * lax.rsqrt(jnp.mean(x * x, axis=-1, keepdims=True) + EPS)
    return (y * g) * (1.0 + sc) + sh


def _prenorm_body(xp_ref, xs_ref, g_ref, sc_ref, sh_ref, o_ref, *, n_p):
    m = pl.program_id(0)

    @pl.when(m < n_p)
    def _():
        o_ref[...] = _modulated_norm(xp_ref[...], g_ref[...], sc_ref[...], sh_ref[...]).astype(BF16)

    @pl.when(m >= n_p)
    def _():
        o_ref[...] = _modulated_norm(xs_ref[...], g_ref[...], sc_ref[...], sh_ref[...]).astype(BF16)


def _prenorm_call(xp, xs, norm_w, mod3, sh_blk, sc_blk, name):
    tm = 1024
    n_p = T_P // tm
    blk = (tm, D_MODEL)
    in_specs = [
        pl.BlockSpec(blk, lambda m: (jnp.minimum(m, n_p - 1), 0)),
        pl.BlockSpec(blk, lambda m: (jnp.maximum(m - n_p, 0), 0)),
        pl.BlockSpec((1, D_MODEL), lambda m: (0, 0)),
        _mod_spec(tm, D_MODEL, lambda n: sc_blk, 1),
        _mod_spec(tm, D_MODEL, lambda n: sh_blk, 1),
    ]
    est = 4 * _nbytes(blk, F32) + 2 * _nbytes(blk, BF16) + _nbytes(blk, F32)
    return pl.pallas_call(
        functools.partial(_prenorm_body, n_p=n_p),
        grid=(T // tm,),
        in_specs=in_specs,
        out_specs=pl.BlockSpec(blk, lambda m: (m, 0)),
        out_shape=jax.ShapeDtypeStruct((T, D_MODEL), BF16),
        compiler_params=_params(est, 1),
        name=name,
    )(xp, xs, norm_w, mod3, mod3)


def _matmul_body(*refs, lhs_split, n_rhs, col0s, pairs, n_extra, n_out, epilogue,
                 tm, tn, kc, row_chunk, stream, single_tile):
    n_lhs_refs = sum(2 if s else 1 for s in lhs_split)
    lhs_refs = refs[:n_lhs_refs]
    pos = n_lhs_refs
    rhs = refs[pos:pos + n_rhs]
    pos += n_rhs
    extras = refs[pos:pos + n_extra]
    pos += n_extra
    outs = refs[pos:pos + n_out]
    pos += n_out
    wb = refs[pos:pos + n_rhs]
    n, m = pl.program_id(0), pl.program_id(1)
    is_prompt = m < T_P // tm

    def lhs_rows(i, rows):
        first = sum(2 if s else 1 for s in lhs_split[:i])
        if lhs_split[i]:
            return jnp.where(is_prompt, lhs_refs[first][rows, :], lhs_refs[first + 1][rows, :])
        return lhs_refs[first][rows, :]

    def compute(weight):
        for r in range(tm // row_chunk):
            rows = slice(r * row_chunk, (r + 1) * row_chunk)
            zs = [jnp.dot(lhs_rows(i, rows), weight(j), preferred_element_type=F32) for i, j in pairs]
            epilogue(zs, extras, outs, rows)

    if not stream:
        @pl.when(m == 0)
        def _():
            for w_ref, wb_ref in zip(rhs, wb):
                wb_ref[...] = w_ref[...].astype(BF16)

        compute(lambda j: wb[j][...])
        return

    stage = refs[pos + n_rhs:pos + 2 * n_rhs]
    sems = refs[pos + 2 * n_rhs:pos + 3 * n_rhs]
    n_tiles = pl.num_programs(0)
    n_chunks = wb[0].shape[1] // kc

    def chunk_copy(j, tile, chunk, slot):
        col = pl.multiple_of(col0s[j] + tile * tn, tn)
        row = pl.multiple_of(chunk * kc, kc)
        return pltpu.make_async_copy(
            rhs[j].at[pl.ds(row, kc), pl.ds(col, tn)], stage[j].at[slot], sems[j].at[slot])

    nxt = jnp.minimum(n + 1, n_tiles - 1)
    chunk = jnp.minimum(m, n_chunks - 1)
    first = jnp.logical_and(n == 0, m == 0)

    def start_prefetch():
        for j in range(n_rhs):
            chunk_copy(j, nxt, chunk, 0).start()

    @pl.when(first)
    def _():
        for j in range(n_rhs):
            chunk_copy(j, 0, 0, 0).start()
            for c in range(n_chunks):
                if c + 1 < n_chunks:
                    chunk_copy(j, 0, c + 1, (c + 1) % 2).start()
                chunk_copy(j, 0, c, c % 2).wait()
                wb[j][0, c * kc:(c + 1) * kc, :] = stage[j][c % 2].astype(BF16)
        if not single_tile:
            start_prefetch()

    if single_tile:
        compute(lambda j: wb[j][0])
        return

    pl.when(jnp.logical_not(first))(start_prefetch)

    for cur in range(2):
        @pl.when(n % 2 == cur)
        def _(cur=cur):
            compute(lambda j: wb[j][cur])
            for j in range(n_rhs):
                chunk_copy(j, nxt, chunk, 0).wait()
                wb[j][1 - cur, pl.ds(pl.multiple_of(chunk * kc, kc), kc), :] = stage[j][0].astype(BF16)


def _matmul_call(lhs_list, rhs_list, pairs, n_cols, tm, tn, epilogue, extras, outs, name,
                 row_chunk=256, stream=False, kc=256):
    lhs_split = tuple(isinstance(a, tuple) for a in lhs_list)
    k_dim = (lhs_list[0][0] if lhs_split[0] else lhs_list[0]).shape[1]
    n_p = T_P // tm
    grid = (n_cols // tn, T // tm)
    in_specs, lhs_args = [], []
    for a in lhs_list:
        if isinstance(a, tuple):
            in_specs.append(pl.BlockSpec((tm, k_dim), lambda n, m: (jnp.minimum(m, n_p - 1), 0)))
            in_specs.append(pl.BlockSpec((tm, k_dim), lambda n, m: (jnp.maximum(m - n_p, 0), 0)))
            lhs_args += list(a)
        else:
            in_specs.append(pl.BlockSpec((tm, k_dim), lambda n, m: (m, 0)))
            lhs_args.append(a)
    est = 2 * len(lhs_args) * _nbytes((tm, k_dim), BF16)
    single_tile = stream and grid[0] == 1
    if stream:
        assert k_dim % kc == 0 and k_dim // kc <= grid[1], (k_dim, kc, grid)
        n_buf = 1 if single_tile else 2
        in_specs += [pl.BlockSpec(memory_space=pl.ANY) for _ in rhs_list]
        est += len(rhs_list) * (n_buf * _nbytes((k_dim, tn), BF16) + 2 * _nbytes((kc, tn), F32))
        scratch = [pltpu.VMEM((n_buf, k_dim, tn), BF16) for _ in rhs_list]
        scratch += [pltpu.VMEM((2, kc, tn), F32) for _ in rhs_list]
        scratch += [pltpu.SemaphoreType.DMA((2,)) for _ in rhs_list]
    else:
        for _, col0 in rhs_list:
            in_specs.append(pl.BlockSpec((k_dim, tn), lambda n, m, c=col0 // tn: (0, c + n)))
        est += len(rhs_list) * (2 * _nbytes((k_dim, tn), F32) + _nbytes((k_dim, tn), BF16))
        scratch = [pltpu.VMEM((k_dim, tn), BF16) for _ in rhs_list]
    in_specs += [spec for _, spec in extras]
    for arr, spec in list(extras) + list(outs):
        est += 2 * _nbytes(spec.block_shape, arr.dtype)
    est += 3 * len(pairs) * _nbytes((row_chunk, tn), F32)
    body = functools.partial(
        _matmul_body, lhs_split=lhs_split, n_rhs=len(rhs_list),
        col0s=tuple(c for _, c in rhs_list), pairs=tuple(pairs),
        n_extra=len(extras), n_out=len(outs), epilogue=epilogue,
        tm=tm, tn=tn, kc=kc, row_chunk=row_chunk, stream=stream, single_tile=single_tile)
    res = pl.pallas_call(
        body,
        grid=grid,
        in_specs=in_specs,
        out_specs=[spec for _, spec in outs],
        out_shape=[s for s, _ in outs],
        scratch_shapes=scratch,
        compiler_params=_params(est, 2),
        name=name,
    )(*lhs_args, *[w for w, _ in rhs_list], *[a for a, _ in extras])
    return res


def _tile_spec(tm, tn, col_block0=0):
    return pl.BlockSpec((tm, tn), lambda n, m, c=col_block0: (m, c + n))


def _prompt_tile_spec(tm, tn):
    n_p = T_P // tm
    return pl.BlockSpec((tm, tn), lambda n, m: (jnp.minimum(m, n_p), n))


def _head_rms(x, g):
    return x * lax.rsqrt(jnp.mean(x * x, axis=-1, keepdims=True) + EPS) * g


def _rope(y, c, s_lo, s_hi):
    return y * c + pltpu.roll(y, HEAD_DIM - 32, axis=1) * s_lo + pltpu.roll(y, 32, axis=1) * s_hi


def _epi_q(zs, extras, outs, rows, *, tn):
    g_ref, c_ref, slo_ref, shi_ref = extras
    (q_ref,) = outs
    (z,) = zs
    for h in range(tn // HEAD_DIM):
        sl = slice(h * HEAD_DIM, (h + 1) * HEAD_DIM)
        y = _head_rms(z[:, sl], g_ref[...])
        q_ref[rows, sl] = _rope(y, c_ref[rows, :], slo_ref[rows, :], shi_ref[rows, :]).astype(BF16)


def _epi_k(zs, extras, outs, rows, *, tn):
    g_ref, c_ref, slo_ref, shi_ref = extras
    k_att_ref, k_new_ref = outs
    (z,) = zs
    for h in range(tn // HEAD_DIM):
        sl = slice(h * HEAD_DIM, (h + 1) * HEAD_DIM)
        y = _head_rms(z[:, sl], g_ref[...])
        k_new_ref[rows, sl] = y
        k_att_ref[rows, sl] = _rope(y, c_ref[rows, :], slo_ref[rows, :], shi_ref[rows, :]).astype(BF16)


def _epi_v(zs, extras, outs, rows):
    v_att_ref, v_new_ref = outs
    v_att_ref[rows, :] = zs[0].astype(BF16)
    v_new_ref[rows, :] = zs[0]


def _epi_cast(zs, extras, outs, rows):
    outs[0][rows, :] = zs[0].astype(outs[0].dtype)


def _epi_scale(zs, extras, outs, rows, *, scale):
    outs[0][rows, :] = zs[0] * scale


def _epi_silu(zs, extras, outs, rows):
    outs[0][rows, :] = jax.nn.silu(zs[0])


def _epi_sigmoid(zs, extras, outs, rows):
    outs[0][rows, :] = jax.nn.sigmoid(zs[0])


def _epi_merge(zs, extras, outs, rows):
    ga_ref, gr_ref = extras
    outs[0][rows, :] = (ga_ref[rows, :] * zs[0] + gr_ref[rows, :] * zs[1]).astype(BF16)


def _epi_residual_norm(zs, extras, outs, rows, *, tm):
    xp_ref, xs_ref, gate_ref, g_ref, sc_ref, sh_ref = extras
    x1_ref, h_ref = outs
    x = jnp.where(pl.program_id(1) < T_P // tm, xp_ref[rows, :], xs_ref[rows, :])
    x1 = x + gate_ref[...] * zs[0]
    x1_ref[rows, :] = x1
    h_ref[rows, :] = _modulated_norm(x1, g_ref[...], sc_ref[...], sh_ref[...]).astype(BF16)


def _epi_swiglu(zs, extras, outs, rows):
    outs[0][rows, :] = (jax.nn.silu(zs[0]) * zs[1]).astype(BF16)


def _epi_residual(zs, extras, outs, rows):
    x_ref, gate_ref = extras
    outs[0][rows, :] = x_ref[rows, :] + gate_ref[...] * zs[0]


_NT = (((1,), (1,)), ((), ()))


_SOFTMAX_EXP2_SCALE = HEAD_DIM ** -0.5 * 1.4426950408889634


def _attn_body(*refs, has_ctx, seq, tq, n_sub):
    if has_ctx:
        q_ref, k_ref, v_ref, ck_ref, cv_ref, o_ref = refs
        g = pl.program_id(1)
        ck = ck_ref[:, g, :].astype(BF16)
        cv = cv_ref[:, g, :].astype(BF16)
    else:
        q_ref, k_ref, v_ref, o_ref = refs
    chain_rows = tq
    for i, c in [(i, c) for i in range(n_sub) for c in range(tq // chain_rows)]:
        k = k_ref[i * seq:(i + 1) * seq, :]
        v = v_ref[i * seq:(i + 1) * seq, :]
        rows = slice(i * tq + c * chain_rows, i * tq + (c + 1) * chain_rows)
        for h in range(GQA_GROUP):
            sl = slice(h * HEAD_DIM, (h + 1) * HEAD_DIM)
            q = q_ref[rows, sl]
            s_new = lax.dot_general(q, k, _NT, preferred_element_type=F32)
            mx = jnp.max(s_new, axis=-1, keepdims=True)
            if has_ctx:
                s_ctx = lax.dot_general(q, ck, _NT, preferred_element_type=F32)
                mx = jnp.maximum(mx, jnp.max(s_ctx, axis=-1, keepdims=True))
                p_ctx = jnp.exp2((s_ctx - mx) * _SOFTMAX_EXP2_SCALE)
                p_new = jnp.exp2((s_new - mx) * _SOFTMAX_EXP2_SCALE)
                den = jnp.sum(p_ctx, axis=-1, keepdims=True) + jnp.sum(p_new, axis=-1, keepdims=True)
                o = jnp.dot(p_ctx.astype(BF16), cv, preferred_element_type=F32)
                o = o + jnp.dot(p_new.astype(BF16), v, preferred_element_type=F32)
            else:
                p_new = jnp.exp2((s_new - mx) * _SOFTMAX_EXP2_SCALE)
                den = jnp.sum(p_new, axis=-1, keepdims=True)
                o = jnp.dot(p_new.astype(BF16), v, preferred_element_type=F32)
            o_ref[rows, sl] = (o * (1.0 / den)).astype(BF16)


def _attn_call(q_att, k_att, v_att, ctx_k, ctx_v, *, latent):
    gw = GQA_GROUP * HEAD_DIM
    if latent:
        seq, nb, tq, n_sub = DEC_SEQ, DEC_BATCH, 1024, 1
        row0 = T_P
    else:
        seq, nb, tq, n_sub = SEQ, BATCH, SEQ, 4
        row0 = 0
    nqt = seq // tq
    q_rows, kv_rows = n_sub * tq, n_sub * seq
    q_spec = pl.BlockSpec((q_rows, gw), lambda b, g, t: (row0 // q_rows + b * nqt + t, g))
    o_spec = pl.BlockSpec((q_rows, gw), lambda b, g, t: (b * nqt + t, g))
    kv_spec = pl.BlockSpec((kv_rows, HEAD_DIM), lambda b, g, t: (row0 // kv_rows + b, g))
    in_specs = [q_spec, kv_spec, kv_spec]
    args = [q_att, k_att, v_att]
    n_keys = seq
    if latent:
        c_spec = pl.BlockSpec((None, None, PAST_LEN, N_KV_HEADS, HEAD_DIM),
                              lambda b, g, t: (b, 0, 0, 0, 0))
        in_specs += [c_spec, c_spec]
        args += [ctx_k, ctx_v]
        n_keys += PAST_LEN
    est = 4 * _nbytes((q_rows, gw), BF16) + 4 * _nbytes((kv_rows, HEAD_DIM), BF16)
    est += 4 * _nbytes((PAST_LEN, 8, HEAD_DIM), F32)
    est += 3 * GQA_GROUP * n_sub * _nbytes((tq, n_keys), F32)
    return pl.pallas_call(
        functools.partial(_attn_body, has_ctx=latent, seq=seq, tq=tq, n_sub=n_sub),
        grid=(nb // n_sub, N_KV_HEADS, nqt),
        in_specs=in_specs,
        out_specs=o_spec,
        out_shape=jax.ShapeDtypeStruct((nb * seq, ATTN_Q), BF16),
        compiler_params=_params(est, 3),
        name="attn_latent" if latent else "attn_prompt",
    )(*args)


_TN = (((0,), (0,)), ((), ()))


def _ret_body(*refs, seq, tq, latent, n_sub):
    if latent:
        (lgf_ref, lgb_ref, q_ref, k_ref, v_ref, sg_ref, rn_ref, s0f_ref, s0b_ref,
         o_ref, d_ref, xif_ref, xib_ref) = refs
    else:
        (lgf_ref, lgb_ref, q_ref, k_ref, v_ref, sg_ref, rn_ref,
         o_ref, sf_ref, sb_ref, d_ref) = refs
    h = pl.program_id(0)
    lgf = lgf_ref[h]
    lgb = lgb_ref[h]

    @pl.when(pl.program_id(1) == 0)
    def _():
        for t in range(seq // tq):
            i = lax.broadcasted_iota(jnp.int32, (tq, seq), 0) + t * tq
            j = lax.broadcasted_iota(jnp.int32, (tq, seq), 1)
            diff = (i - j).astype(F32)
            arg = jnp.where(diff >= 0, diff * lgf, -diff * lgb)
            d_ref[t * tq:(t + 1) * tq, :] = jnp.exp(arg) * jnp.where(diff == 0, 2.0, 1.0)
        if latent:
            pos = lax.broadcasted_iota(jnp.int32, (seq, RET_VAL_DIM), 0).astype(F32)
            xif_ref[...] = jnp.exp((pos + 1.0) * lgf)
            xib_ref[...] = jnp.exp((seq - pos) * lgb)

    if latent:
        s0f = s0f_ref[...].astype(BF16)
        s0b = s0b_ref[...].astype(BF16)
    for i in range(n_sub):
        k_f32 = k_ref[i * seq:(i + 1) * seq, :]
        kb = k_f32.astype(BF16)
        v = v_ref[i * seq:(i + 1) * seq, :]
        for t in range(seq // tq):
            rows = slice(i * seq + t * tq, i * seq + (t + 1) * tq)
            q = q_ref[rows, :]
            raw = lax.dot_general(q, kb, _NT, preferred_element_type=F32)
            p = (raw * d_ref[t * tq:(t + 1) * tq, :]).astype(BF16)
            o = jnp.dot(p, v, preferred_element_type=F32)
            if latent:
                tile = slice(t * tq, (t + 1) * tq)
                o = o + jnp.dot(q, s0f, preferred_element_type=F32) * xif_ref[tile, :]
                o = o + jnp.dot(q, s0b, preferred_element_type=F32) * xib_ref[tile, :]
            mu = jnp.mean(o, axis=-1, keepdims=True)
            oc = o - mu
            var = jnp.mean(oc * oc, axis=-1, keepdims=True)
            y = (oc * lax.rsqrt(var + EPS)) * rn_ref[...]
            o_ref[rows, :] = (y * sg_ref[rows, :]).astype(BF16)
        if not latent:
            j = lax.broadcasted_iota(jnp.int32, (seq, RET_KEY_DIM), 0).astype(F32)
            kz_f = (k_f32 * jnp.exp((seq - 1.0 - j) * lgf)).astype(BF16)
            kz_b = (k_f32 * jnp.exp(j * lgb)).astype(BF16)
            sf_ref[i] = lax.dot_general(kz_f, v, _TN, preferred_element_type=F32)
            sb_ref[i] = lax.dot_general(kz_b, v, _TN, preferred_element_type=F32)


def _ret_call(lgf, lgb, q_ret, k_ret, v_ret, sg, ret_norm, s0f, s0b, *, latent):
    if latent:
        seq, nb, row0, n_sub = DEC_SEQ, DEC_BATCH, T_P, 1
    else:
        seq, nb, row0, n_sub = SEQ, BATCH, 0, 8
    tq = 256
    rows = n_sub * seq
    rb = row0 // rows
    smem = pl.BlockSpec(memory_space=pltpu.SMEM)
    qk_spec = pl.BlockSpec((rows, RET_KEY_DIM), lambda h, b: (rb + b, h))
    v_spec = pl.BlockSpec((rows, RET_VAL_DIM), lambda h, b: (rb + b, h))
    o_spec = pl.BlockSpec((rows, RET_VAL_DIM), lambda h, b: (b, h))
    in_specs = [smem, smem, qk_spec, qk_spec, v_spec, v_spec,
                pl.BlockSpec((1, RET_VAL_DIM), lambda h, b: (0, h))]
    args = [lgf, lgb, q_ret, k_ret, v_ret, sg, ret_norm]
    o_shape = jax.ShapeDtypeStruct((nb * seq, RET_V), BF16)
    if latent:
        st_spec = pl.BlockSpec((None, None, None, RET_KEY_DIM, RET_VAL_DIM),
                               lambda h, b: (b, 0, h, 0, 0))
        in_specs += [st_spec, st_spec]
        args += [s0f, s0b]
        out_specs = o_spec
        out_shape = o_shape
    else:
        st_spec = pl.BlockSpec((n_sub, None, None, RET_KEY_DIM, RET_VAL_DIM),
                               lambda h, b: (b, 0, h, 0, 0))
        st_shape = jax.ShapeDtypeStruct((BATCH, 1, N_RET_HEADS, RET_KEY_DIM, RET_VAL_DIM), F32)
        out_specs = [o_spec, st_spec, st_spec]
        out_shape = [o_shape, st_shape, st_shape]
    est = _nbytes((seq, seq), F32) + 4 * n_sub * _nbytes((tq, seq), F32)
    est += 2 * (_nbytes((rows, RET_KEY_DIM), BF16) + _nbytes((rows, RET_KEY_DIM), F32))
    est += 2 * (2 * _nbytes((rows, RET_VAL_DIM), BF16) + _nbytes((rows, RET_VAL_DIM), F32))
    est += 8 * n_sub * _nbytes((RET_KEY_DIM, RET_VAL_DIM), F32)
    return pl.pallas_call(
        functools.partial(_ret_body, seq=seq, tq=tq, latent=latent, n_sub=n_sub),
        grid=(N_RET_HEADS, nb // n_sub),
        in_specs=in_specs,
        out_specs=out_specs,
        out_shape=out_shape,
        scratch_shapes=[pltpu.VMEM((seq, seq), F32)]
        + ([pltpu.VMEM((seq, RET_VAL_DIM), F32)] * 2 if latent else []),
        compiler_params=_params(est, 2),
        name="ret_latent" if latent else "ret_prompt",
    )(*args)


def _final_body(x_ref, g_ref, yp_ref, ys_ref, *, n_p):
    x = x_ref[...]
    y = x * lax.rsqrt(jnp.mean(x * x, axis=-1, keepdims=True) + EPS) * g_ref[...]
    m = pl.program_id(0)

    @pl.when(m < n_p)
    def _():
        yp_ref[...] = y

    @pl.when(m >= n_p)
    def _():
        ys_ref[...] = y


def _final_call(x, g):
    tm = 1024
    n_p = T_P // tm
    blk = (tm, D_MODEL)
    est = 6 * _nbytes(blk, F32) + _nbytes(blk, F32)
    return pl.pallas_call(
        functools.partial(_final_body, n_p=n_p),
        grid=(T // tm,),
        in_specs=[pl.BlockSpec(blk, lambda m: (m, 0)), pl.BlockSpec((1, D_MODEL), lambda m: (0, 0))],
        out_specs=[
            pl.BlockSpec(blk, lambda m: (jnp.minimum(m, n_p - 1), 0)),
            pl.BlockSpec(blk, lambda m: (jnp.maximum(m - n_p, 0), 0)),
        ],
        out_shape=[jax.ShapeDtypeStruct((T_P, D_MODEL), F32), jax.ShapeDtypeStruct((T_S, D_MODEL), F32)],
        compiler_params=_params(est, 1),
        name="final_norm",
    )(x, g)


def _rope_tables():
    rows = DEC_SEQ // GRID_W
    row = jnp.repeat(jnp.arange(rows, dtype=F32), GRID_W)
    col = jnp.tile(jnp.arange(GRID_W, dtype=F32), rows)
    quarter = HEAD_DIM // 4
    inv_freq = ROPE_THETA ** (-jnp.arange(quarter, dtype=F32) / quarter)
    ang_r = row[:, None] * inv_freq[None, :]
    ang_c = col[:, None] * inv_freq[None, :]
    cr, sr, cc, sc = jnp.cos(ang_r), jnp.sin(ang_r), jnp.cos(ang_c), jnp.sin(ang_c)
    zero = jnp.zeros_like(sr)
    cos = jnp.concatenate([cr, cr, cc, cc], axis=-1)
    s_lo = jnp.concatenate([-sr, zero, -sc, zero], axis=-1)
    s_hi = jnp.concatenate([zero, sr, zero, sc], axis=-1)
    ident = jnp.zeros((DEC_SEQ, HEAD_DIM), F32)
    return (jnp.concatenate([ident + 1.0, cos], axis=0),
            jnp.concatenate([ident, s_lo], axis=0),
            jnp.concatenate([ident, s_hi], axis=0))


def kernel(x_prompt, x_sample, cache_attn_k, cache_attn_v, state_ret_fwd, state_ret_bwd, c, c_ctx,
           norm_attn, norm_ffn, w_mod, b_mod, w_in, q_norm, k_norm, ret_decay_fwd, ret_decay_bwd,
           ret_norm, w_branch_attn, w_branch_ret, w_out, w_ffn_gate, w_ffn_up, w_ffn_down, final_norm):
    xp = x_prompt.reshape(T_P, D_MODEL)
    xs = x_sample.reshape(T_S, D_MODEL)
    lgf = jax.nn.log_sigmoid(ret_decay_fwd[0].astype(F32))
    lgb = jax.nn.log_sigmoid(ret_decay_bwd[0].astype(F32))
    cos, s_lo, s_hi = _rope_tables()

    c_all = jnp.concatenate(
        [c_ctx[None, :], c, jnp.zeros((MOD_ROWS - 1 - DEC_BATCH, D_MODEL), F32)], axis=0)
    mod = _mod_call(c_all, w_mod[0], b_mod)
    mod3 = mod.reshape(MOD_ROWS, 1, 6 * D_MODEL)
    h = _prenorm_call(xp, xs, norm_attn, mod3, 0, 1, "prenorm_attn")

    w_in0 = w_in[0]
    tm = 1024
    tbl = pl.BlockSpec((DEC_SEQ, HEAD_DIM), lambda n, m: (jnp.where(m < T_P // tm, 0, 1), 0))
    hd = pl.BlockSpec((1, HEAD_DIM), lambda n, m: (0, 0))
    rope_extras = [(cos, tbl), (s_lo, tbl), (s_hi, tbl)]

    def proj(col0, n_cols, tn, epilogue, extras, outs, name):
        return _matmul_call([h], [(w_in0, col0)], [(0, 0)], n_cols, tm, tn, epilogue, extras, outs, name)

    def full(n_cols, dtype):
        return jax.ShapeDtypeStruct((T, n_cols), dtype)

    (q_att,) = proj(COL_Q, ATTN_Q, 1024, functools.partial(_epi_q, tn=1024),
                    [(q_norm, hd)] + rope_extras,
                    [(full(ATTN_Q, BF16), _tile_spec(tm, 1024))], "proj_q")
    prompt_kv = jax.ShapeDtypeStruct((T_P + tm, ATTN_KV), F32)
    k_att, k_new = proj(COL_K, ATTN_KV, ATTN_KV, functools.partial(_epi_k, tn=ATTN_KV),
                        [(k_norm, hd)] + rope_extras,
                        [(full(ATTN_KV, BF16), _tile_spec(tm, ATTN_KV)),
                         (prompt_kv, _prompt_tile_spec(tm, ATTN_KV))], "proj_k")
    v_att, v_new = proj(COL_V, ATTN_KV, ATTN_KV, _epi_v, [],
                        [(full(ATTN_KV, BF16), _tile_spec(tm, ATTN_KV)),
                         (prompt_kv, _prompt_tile_spec(tm, ATTN_KV))], "proj_v")
    (q_ret,) = proj(COL_QR, RET_QK, 1024, _epi_cast, [],
                    [(full(RET_QK, BF16), _tile_spec(tm, 1024))], "proj_qr")
    (k_ret,) = proj(COL_KR, RET_QK, 1024, functools.partial(_epi_scale, scale=RET_KEY_DIM ** -0.5), [],
                    [(full(RET_QK, F32), _tile_spec(tm, 1024))], "proj_kr")
    def proj_tall(col0, n_cols, epilogue, dtype, name):
        return _matmul_call([h], [(w_in0, col0)], [(0, 0)], n_cols, 2048, 1024, epilogue, [],
                            [(full(n_cols, dtype), _tile_spec(2048, 1024))], name,
                            stream=True, kc=512)

    (v_ret,) = proj_tall(COL_VR, RET_V, _epi_cast, BF16, "proj_vr")
    (sg,) = proj(COL_GR, RET_V, 1024, _epi_silu, [],
                 [(full(RET_V, F32), _tile_spec(tm, 1024))], "proj_gr")
    (gates,) = proj_tall(COL_GATES, 2 * D_MODEL, _epi_sigmoid, F32, "proj_gates")

    o_a_p = _attn_call(q_att, k_att, v_att, None, None, latent=False)
    o_a_s = _attn_call(q_att, k_att, v_att, cache_attn_k, cache_attn_v, latent=True)
    o_r_p, new_sf, new_sb = _ret_call(lgf, lgb, q_ret, k_ret, v_ret, sg, ret_norm,
                                      None, None, latent=False)
    o_r_s = _ret_call(lgf, lgb, q_ret, k_ret, v_ret, sg, ret_norm,
                      state_ret_fwd, state_ret_bwd, latent=True)

    tm2, tn2 = 512, 1024
    (merged,) = _matmul_call(
        [(o_a_p, o_a_s), (o_r_p, o_r_s)], [(w_branch_attn[0], 0), (w_branch_ret[0], 0)],
        [(0, 0), (1, 1)], D_MODEL, tm2, tn2, _epi_merge,
        [(gates, _tile_spec(tm2, tn2)), (gates, _tile_spec(tm2, tn2, D_MODEL // tn2))],
        [(full(D_MODEL, BF16), _tile_spec(tm2, tn2))], "merge", stream=True)

    tm3 = 512
    n_p3 = T_P // tm3
    row_vec = pl.BlockSpec((1, D_MODEL), lambda n, m: (0, 0))
    x1, h2 = _matmul_call(
        [merged], [(w_out[0], 0)], [(0, 0)], D_MODEL, tm3, D_MODEL,
        functools.partial(_epi_residual_norm, tm=tm3),
        [(xp, pl.BlockSpec((tm3, D_MODEL), lambda n, m: (jnp.minimum(m, n_p3 - 1), 0))),
         (xs, pl.BlockSpec((tm3, D_MODEL), lambda n, m: (jnp.maximum(m - n_p3, 0), 0))),
         (mod3, _mod_spec(tm3, D_MODEL, lambda n: 2, 2)),
         (norm_ffn, row_vec),
         (mod3, _mod_spec(tm3, D_MODEL, lambda n: 4, 2)),
         (mod3, _mod_spec(tm3, D_MODEL, lambda n: 3, 2))],
        [(full(D_MODEL, F32), _tile_spec(tm3, D_MODEL)),
         (full(D_MODEL, BF16), _tile_spec(tm3, D_MODEL))], "out_proj", stream=True)

    (act,) = _matmul_call(
        [h2], [(w_ffn_gate[0], 0), (w_ffn_up[0], 0)], [(0, 0), (0, 1)],
        D_FF, 1024, 512, _epi_swiglu, [],
        [(full(D_FF, BF16), _tile_spec(1024, 512))], "ffn_up")

    tm4, tn4 = 512, 1024
    (y_pre,) = _matmul_call(
        [act], [(w_ffn_down[0], 0)], [(0, 0)], D_MODEL, tm4, tn4, _epi_residual,
        [(x1, _tile_spec(tm4, tn4)),
         (mod3, _mod_spec(tm4, tn4, lambda n: 5 * (D_MODEL // tn4) + n, 2))],
        [(full(D_MODEL, F32), _tile_spec(tm4, tn4))], "ffn_down", stream=True, kc=512)

    y_p, y_s = _final_call(y_pre, final_norm[None, :])

    return (y_p.reshape(BATCH, SEQ, D_MODEL),
            y_s.reshape(DEC_BATCH, DEC_SEQ, D_MODEL),
            k_new[:T_P].reshape(BATCH, 1, SEQ, N_KV_HEADS, HEAD_DIM),
            v_new[:T_P].reshape(BATCH, 1, SEQ, N_KV_HEADS, HEAD_DIM),
            new_sf, new_sb)
```

```python
import functools

import jax
import jax.numpy as jnp
from jax import lax
from jax.experimental import pallas as pl
from jax.experimental.pallas import tpu as pltpu

D_MODEL = 2048
BATCH = 16
SEQ = 256
DEC_BATCH = 8
DEC_SEQ = 1024
PAST_LEN = 512
GRID_W = 64
N_HEADS = 16
N_KV_HEADS = 4
HEAD_DIM = 128
ROPE_THETA = 10000.0
N_RET_HEADS = 8
RET_KEY_DIM = 128
RET_VAL_DIM = 256
D_FF = 5632
EPS = 1e-6

ATTN_Q = N_HEADS * HEAD_DIM
ATTN_KV = N_KV_HEADS * HEAD_DIM
RET_QK = N_RET_HEADS * RET_KEY_DIM
RET_V = N_RET_HEADS * RET_VAL_DIM
GQA_GROUP = N_HEADS // N_KV_HEADS

COL_Q = 0
COL_K = COL_Q + ATTN_Q
COL_V = COL_K + ATTN_KV
COL_QR = COL_V + ATTN_KV
COL_KR = COL_QR + RET_QK
COL_VR = COL_KR + RET_QK
COL_GR = COL_VR + RET_V
COL_GATES = COL_GR + RET_V

T_P = BATCH * SEQ
T_S = DEC_BATCH * DEC_SEQ
T = T_P + T_S
MOD_ROWS = 16

V7X_VMEM_BYTES = 64 * 1024 * 1024
VMEM_CAP = V7X_VMEM_BYTES - 6 * 1024 * 1024

BF16 = jnp.bfloat16
F32 = jnp.float32


def _nbytes(shape, dtype):
    n = 1
    for s in shape:
        if s is not None:
            n *= s
    return n * jnp.dtype(dtype).itemsize


def _params(vmem_estimate, n_grid):
    limit = min(VMEM_CAP, int(vmem_estimate) + 8 * 1024 * 1024)
    return pltpu.CompilerParams(
        dimension_semantics=("arbitrary",) * n_grid, vmem_limit_bytes=limit)


def _mod_row(m, tm):
    n_p = T_P // tm
    return jnp.where(m < n_p, 0, 1 + (m - n_p) // (DEC_SEQ // tm))


def _mod_spec(tm, width, col_block, grid_rank):
    if grid_rank == 1:
        return pl.BlockSpec((None, 1, width), lambda m: (_mod_row(m, tm), 0, col_block(0)))
    return pl.BlockSpec((None, 1, width), lambda n, m: (_mod_row(m, tm), 0, col_block(n)))


def _mod_body(c_ref, w_ref, b_ref, o_ref):
    a = jax.nn.silu(c_ref[...]).astype(BF16)
    w = w_ref[...].astype(BF16)
    o_ref[...] = jnp.dot(a, w, preferred_element_type=F32) + b_ref[...]


def _mod_call(c_all, w_mod, b_mod):
    tn = 1024
    n_out = 6 * D_MODEL
    est = 2 * _nbytes((D_MODEL, tn), F32) + _nbytes((D_MODEL, tn), BF16)
    return pl.pallas_call(
        _mod_body,
        grid=(n_out // tn,),
        in_specs=[
            pl.BlockSpec((MOD_ROWS, D_MODEL), lambda n: (0, 0)),
            pl.BlockSpec((D_MODEL, tn), lambda n: (0, n)),
            pl.BlockSpec((1, tn), lambda n: (0, n)),
        ],
        out_specs=pl.BlockSpec((MOD_ROWS, tn), lambda n: (0, n)),
        out_shape=jax.ShapeDtypeStruct((MOD_ROWS, n_out), F32),
        compiler_params=_params(est, 1),
        name="mod_table",
    )(c_all, w_mod, b_mod)


def _modulated_norm(x, g, sc, sh):
    y = x * lax.rsqrt(jnp.mean(x * x, axis=-1, keepdims=True) + EPS)
    return (y * g) * (1.0 + sc) + sh


def _prenorm_body(xp_ref, xs_ref, g_ref, sc_ref, sh_ref, o_ref, *, n_p):
    m = pl.program_id(0)

    @pl.when(m < n_p)
    def _():
        o_ref[...] = _modulated_norm(xp_ref[...], g_ref[...], sc_ref[...], sh_ref[...]).astype(BF16)

    @pl.when(m >= n_p)
    def _():
        o_ref[...] = _modulated_norm(xs_ref[...], g_ref[...], sc_ref[...], sh_ref[...]).astype(BF16)


def _prenorm_call(xp, xs, norm_w, mod3, sh_blk, sc_blk, name):
    tm = 1024
    n_p = T_P // tm
    blk = (tm, D_MODEL)
    in_specs = [
        pl.BlockSpec(blk, lambda m: (jnp.minimum(m, n_p - 1), 0)),
        pl.BlockSpec(blk, lambda m: (jnp.maximum(m - n_p, 0), 0)),
        pl.BlockSpec((1, D_MODEL), lambda m: (0, 0)),
        _mod_spec(tm, D_MODEL, lambda n: sc_blk, 1),
        _mod_spec(tm, D_MODEL, lambda n: sh_blk, 1),
    ]
    est = 4 * _nbytes(blk, F32) + 2 * _nbytes(blk, BF16) + _nbytes(blk, F32)
    return pl.pallas_call(
        functools.partial(_prenorm_body, n_p=n_p),
        grid=(T // tm,),
        in_specs=in_specs,
        out_specs=pl.BlockSpec(blk, lambda m: (m, 0)),
        out_shape=jax.ShapeDtypeStruct((T, D_MODEL), BF16),
        compiler_params=_params(est, 1),
        name=name,
    )(xp, xs, norm_w, mod3, mod3)


def _matmul_body(*refs, lhs_split, n_rhs, col0s, pairs, n_extra, n_out, epilogue,
                 tm, tn, kc, row_chunk, stream, single_tile):
    n_lhs_refs = sum(2 if s else 1 for s in lhs_split)
    lhs_refs = refs[:n_lhs_refs]
    pos = n_lhs_refs
    rhs = refs[pos:pos + n_rhs]
    pos += n_rhs
    extras = refs[pos:pos + n_extra]
    pos += n_extra
    outs = refs[pos:pos + n_out]
    pos += n_out
    wb = refs[pos:pos + n_rhs]
    n, m = pl.program_id(0), pl.program_id(1)
    is_prompt = m < T_P // tm

    def lhs_rows(i, rows):
        first = sum(2 if s else 1 for s in lhs_split[:i])
        if lhs_split[i]:
            return jnp.where(is_prompt, lhs_refs[first][rows, :], lhs_refs[first + 1][rows, :])
        return lhs_refs[first][rows, :]

    def compute(weight):
        for r in range(tm // row_chunk):
            rows = slice(r * row_chunk, (r + 1) * row_chunk)
            zs = [jnp.dot(lhs_rows(i, rows), weight(j), preferred_element_type=F32) for i, j in pairs]
            epilogue(zs, extras, outs, rows)

    if not stream:
        @pl.when(m == 0)
        def _():
            for w_ref, wb_ref in zip(rhs, wb):
                wb_ref[...] = w_ref[...].astype(BF16)

        compute(lambda j: wb[j][...])
        return

    stage = refs[pos + n_rhs:pos + 2 * n_rhs]
    sems = refs[pos + 2 * n_rhs:pos + 3 * n_rhs]
    n_tiles = pl.num_programs(0)
    n_chunks = wb[0].shape[1] // kc

    def chunk_copy(j, tile, chunk, slot):
        col = pl.multiple_of(col0s[j] + tile * tn, tn)
        row = pl.multiple_of(chunk * kc, kc)
        return pltpu.make_async_copy(
            rhs[j].at[pl.ds(row, kc), pl.ds(col, tn)], stage[j].at[slot], sems[j].at[slot])

    nxt = jnp.minimum(n + 1, n_tiles - 1)
    chunk = jnp.minimum(m, n_chunks - 1)
    first = jnp.logical_and(n == 0, m == 0)

    def start_prefetch():
        for j in range(n_rhs):
            chunk_copy(j, nxt, chunk, 0).start()

    @pl.when(first)
    def _():
        for j in range(n_rhs):
            chunk_copy(j, 0, 0, 0).start()
            for c in range(n_chunks):
                if c + 1 < n_chunks:
                    chunk_copy(j, 0, c + 1, (c + 1) % 2).start()
                chunk_copy(j, 0, c, c % 2).wait()
                wb[j][0, c * kc:(c + 1) * kc, :] = stage[j][c % 2].astype(BF16)
        if not single_tile:
            start_prefetch()

    if single_tile:
        compute(lambda j: wb[j][0])
        return

    pl.when(jnp.logical_not(first))(start_prefetch)

    for cur in range(2):
        @pl.when(n % 2 == cur)
        def _(cur=cur):
            compute(lambda j: wb[j][cur])
            for j in range(n_rhs):
                chunk_copy(j, nxt, chunk, 0).wait()
                wb[j][1 - cur, pl.ds(pl.multiple_of(chunk * kc, kc), kc), :] = stage[j][0].astype(BF16)


def _matmul_call(lhs_list, rhs_list, pairs, n_cols, tm, tn, epilogue, extras, outs, name,
                 row_chunk=256, stream=False, kc=256):
    lhs_split = tuple(isinstance(a, tuple) for a in lhs_list)
    k_dim = (lhs_list[0][0] if lhs_split[0] else lhs_list[0]).shape[1]
    n_p = T_P // tm
    grid = (n_cols // tn, T // tm)
    in_specs, lhs_args = [], []
    for a in lhs_list:
        if isinstance(a, tuple):
            in_specs.append(pl.BlockSpec((tm, k_dim), lambda n, m: (jnp.minimum(m, n_p - 1), 0)))
            in_specs.append(pl.BlockSpec((tm, k_dim), lambda n, m: (jnp.maximum(m - n_p, 0), 0)))
            lhs_args += list(a)
        else:
            in_specs.append(pl.BlockSpec((tm, k_dim), lambda n, m: (m, 0)))
            lhs_args.append(a)
    est = 2 * len(lhs_args) * _nbytes((tm, k_dim), BF16)
    single_tile = stream and grid[0] == 1
    if stream:
        assert k_dim % kc == 0 and k_dim // kc <= grid[1], (k_dim, kc, grid)
        n_buf = 1 if single_tile else 2
        in_specs += [pl.BlockSpec(memory_space=pl.ANY) for _ in rhs_list]
        est += len(rhs_list) * (n_buf * _nbytes((k_dim, tn), BF16) + 2 * _nbytes((kc, tn), F32))
        scratch = [pltpu.VMEM((n_buf, k_dim, tn), BF16) for _ in rhs_list]
        scratch += [pltpu.VMEM((2, kc, tn), F32) for _ in rhs_list]
        scratch += [pltpu.SemaphoreType.DMA((2,)) for _ in rhs_list]
    else:
        for _, col0 in rhs_list:
            in_specs.append(pl.BlockSpec((k_dim, tn), lambda n, m, c=col0 // tn: (0, c + n)))
        est += len(rhs_list) * (2 * _nbytes((k_dim, tn), F32) + _nbytes((k_dim, tn), BF16))
        scratch = [pltpu.VMEM((k_dim, tn), BF16) for _ in rhs_list]
    in_specs += [spec for _, spec in extras]
    for arr, spec in list(extras) + list(outs):
        est += 2 * _nbytes(spec.block_shape, arr.dtype)
    est += 3 * len(pairs) * _nbytes((row_chunk, tn), F32)
    body = functools.partial(
        _matmul_body, lhs_split=lhs_split, n_rhs=len(rhs_list),
        col0s=tuple(c for _, c in rhs_list), pairs=tuple(pairs),
        n_extra=len(extras), n_out=len(outs), epilogue=epilogue,
        tm=tm, tn=tn, kc=kc, row_chunk=row_chunk, stream=stream, single_tile=single_tile)
    res = pl.pallas_call(
        body,
        grid=grid,
        in_specs=in_specs,
        out_specs=[spec for _, spec in outs],
        out_shape=[s for s, _ in outs],
        scratch_shapes=scratch,
        compiler_params=_params(est, 2),
        name=name,
    )(*lhs_args, *[w for w, _ in rhs_list], *[a for a, _ in extras])
    return res


def _tile_spec(tm, tn, col_block0=0):
    return pl.BlockSpec((tm, tn), lambda n, m, c=col_block0: (m, c + n))


def _prompt_tile_spec(tm, tn):
    n_p = T_P // tm
    return pl.BlockSpec((tm, tn), lambda n, m: (jnp.minimum(m, n_p), n))


def _head_rms(x, g):
    return x * lax.rsqrt(jnp.mean(x * x, axis=-1, keepdims=True) + EPS) * g


def _rope(y, c, s_lo, s_hi):
    return y * c + pltpu.roll(y, HEAD_DIM - 32, axis=1) * s_lo + pltpu.roll(y, 32, axis=1) * s_hi


def _epi_q(zs, extras, outs, rows, *, tn):
    g_ref, c_ref, slo_ref, shi_ref = extras
    (q_ref,) = outs
    (z,) = zs
    for h in range(tn // HEAD_DIM):
        sl = slice(h * HEAD_DIM, (h + 1) * HEAD_DIM)
        y = _head_rms(z[:, sl], g_ref[...])
        q_ref[rows, sl] = _rope(y, c_ref[rows, :], slo_ref[rows, :], shi_ref[rows, :]).astype(BF16)


def _epi_k(zs, extras, outs, rows, *, tn):
    g_ref, c_ref, slo_ref, shi_ref = extras
    k_att_ref, k_new_ref = outs
    (z,) = zs
    for h in range(tn // HEAD_DIM):
        sl = slice(h * HEAD_DIM, (h + 1) * HEAD_DIM)
        y = _head_rms(z[:, sl], g_ref[...])
        k_new_ref[rows, sl] = y
        k_att_ref[rows, sl] = _rope(y, c_ref[rows, :], slo_ref[rows, :], shi_ref[rows, :]).astype(BF16)


def _epi_v(zs, extras, outs, rows):
    v_att_ref, v_new_ref = outs
    v_att_ref[rows, :] = zs[0].astype(BF16)
    v_new_ref[rows, :] = zs[0]


def _epi_cast(zs, extras, outs, rows):
    outs[0][rows, :] = zs[0].astype(outs[0].dtype)


def _epi_scale(zs, extras, outs, rows, *, scale):
    outs[0][rows, :] = zs[0] * scale


def _epi_silu(zs, extras, outs, rows):
    outs[0][rows, :] = jax.nn.silu(zs[0])


def _epi_sigmoid(zs, extras, outs, rows):
    outs[0][rows, :] = jax.nn.sigmoid(zs[0])


def _epi_merge(zs, extras, outs, rows):
    ga_ref, gr_ref = extras
    outs[0][rows, :] = (ga_ref[rows, :] * zs[0] + gr_ref[rows, :] * zs[1]).astype(BF16)


def _epi_residual_norm(zs, extras, outs, rows, *, tm):
    xp_ref, xs_ref, gate_ref, g_ref, sc_ref, sh_ref = extras
    x1_ref, h_ref = outs
    x = jnp.where(pl.program_id(1) < T_P // tm, xp_ref[rows, :], xs_ref[rows, :])
    x1 = x + gate_ref[...] * zs[0]
    x1_ref[rows, :] = x1
    h_ref[rows, :] = _modulated_norm(x1, g_ref[...], sc_ref[...], sh_ref[...]).astype(BF16)


def _epi_swiglu(zs, extras, outs, rows):
    outs[0][rows, :] = (jax.nn.silu(zs[0]) * zs[1]).astype(BF16)


def _epi_residual(zs, extras, outs, rows):
    x_ref, gate_ref = extras
    outs[0][rows, :] = x_ref[rows, :] + gate_ref[...] * zs[0]


_NT = (((1,), (1,)), ((), ()))


_SOFTMAX_EXP2_SCALE = HEAD_DIM ** -0.5 * 1.4426950408889634


def _attn_body(*refs, has_ctx, seq, tq, n_sub):
    if has_ctx:
        q_ref, k_ref, v_ref, ck_ref, cv_ref, o_ref = refs
        g = pl.program_id(1)
        ck = ck_ref[:, g, :].astype(BF16)
        cv = cv_ref[:, g, :].astype(BF16)
    else:
        q_ref, k_ref, v_ref, o_ref = refs
    chain_rows = tq
    for i, c in [(i, c) for i in range(n_sub) for c in range(tq // chain_rows)]:
        k = k_ref[i * seq:(i + 1) * seq, :]
        v = v_ref[i * seq:(i + 1) * seq, :]
        rows = slice(i * tq + c * chain_rows, i * tq + (c + 1) * chain_rows)
        for h in range(GQA_GROUP):
            sl = slice(h * HEAD_DIM, (h + 1) * HEAD_DIM)
            q = q_ref[rows, sl]
            s_new = lax.dot_general(q, k, _NT, preferred_element_type=F32)
            mx = jnp.max(s_new, axis=-1, keepdims=True)
            if has_ctx:
                s_ctx = lax.dot_general(q, ck, _NT, preferred_element_type=F32)
                mx = jnp.maximum(mx, jnp.max(s_ctx, axis=-1, keepdims=True))
                p_ctx = jnp.exp2((s_ctx - mx) * _SOFTMAX_EXP2_SCALE)
                p_new = jnp.exp2((s_new - mx) * _SOFTMAX_EXP2_SCALE)
                den = jnp.sum(p_ctx, axis=-1, keepdims=True) + jnp.sum(p_new, axis=-1, keepdims=True)
                o = jnp.dot(p_ctx.astype(BF16), cv, preferred_element_type=F32)
                o = o + jnp.dot(p_new.astype(BF16), v, preferred_element_type=F32)
            else:
                p_new = jnp.exp2((s_new - mx) * _SOFTMAX_EXP2_SCALE)
                den = jnp.sum(p_new, axis=-1, keepdims=True)
                o = jnp.dot(p_new.astype(BF16), v, preferred_element_type=F32)
            o_ref[rows, sl] = (o * (1.0 / den)).astype(BF16)


def _attn_call(q_att, k_att, v_att, ctx_k, ctx_v, *, latent):
    gw = GQA_GROUP * HEAD_DIM
    if latent:
        seq, nb, tq, n_sub = DEC_SEQ, DEC_BATCH, 1024, 1
        row0 = T_P
    else:
        seq, nb, tq, n_sub = SEQ, BATCH, SEQ, 4
        row0 = 0
    nqt = seq // tq
    q_rows, kv_rows = n_sub * tq, n_sub * seq
    q_spec = pl.BlockSpec((q_rows, gw), lambda b, g, t: (row0 // q_rows + b * nqt + t, g))
    o_spec = pl.BlockSpec((q_rows, gw), lambda b, g, t: (b * nqt + t, g))
    kv_spec = pl.BlockSpec((kv_rows, HEAD_DIM), lambda b, g, t: (row0 // kv_rows + b, g))
    in_specs = [q_spec, kv_spec, kv_spec]
    args = [q_att, k_att, v_att]
    n_keys = seq
    if latent:
        c_spec = pl.BlockSpec((None, None, PAST_LEN, N_KV_HEADS, HEAD_DIM),
                              lambda b, g, t: (b, 0, 0, 0, 0))
        in_specs += [c_spec, c_spec]
        args += [ctx_k, ctx_v]
        n_keys += PAST_LEN
    est = 4 * _nbytes((q_rows, gw), BF16) + 4 * _nbytes((kv_rows, HEAD_DIM), BF16)
    est += 4 * _nbytes((PAST_LEN, 8, HEAD_DIM), F32)
    est += 3 * GQA_GROUP * n_sub * _nbytes((tq, n_keys), F32)
    return pl.pallas_call(
        functools.partial(_attn_body, has_ctx=latent, seq=seq, tq=tq, n_sub=n_sub),
        grid=(nb // n_sub, N_KV_HEADS, nqt),
        in_specs=in_specs,
        out_specs=o_spec,
        out_shape=jax.ShapeDtypeStruct((nb * seq, ATTN_Q), BF16),
        compiler_params=_params(est, 3),
        name="attn_latent" if latent else "attn_prompt",
    )(*args)


_TN = (((0,), (0,)), ((), ()))


def _ret_body(*refs, seq, tq, latent, n_sub):
    if latent:
        (lgf_ref, lgb_ref, q_ref, k_ref, v_ref, sg_ref, rn_ref, s0f_ref, s0b_ref,
         o_ref, d_ref, xif_ref, xib_ref) = refs
    else:
        (lgf_ref, lgb_ref, q_ref, k_ref, v_ref, sg_ref, rn_ref,
         o_ref, sf_ref, sb_ref, d_ref) = refs
    h = pl.program_id(0)
    lgf = lgf_ref[h]
    lgb = lgb_ref[h]

    @pl.when(pl.program_id(1) == 0)
    def _():
        for t in range(seq // tq):
            i = lax.broadcasted_iota(jnp.int32, (tq, seq), 0) + t * tq
            j = lax.broadcasted_iota(jnp.int32, (tq, seq), 1)
            diff = (i - j).astype(F32)
            arg = jnp.where(diff >= 0, diff * lgf, -diff * lgb)
            d_ref[t * tq:(t + 1) * tq, :] = jnp.exp(arg) * jnp.where(diff == 0, 2.0, 1.0)
        if latent:
            pos = lax.broadcasted_iota(jnp.int32, (seq, RET_VAL_DIM), 0).astype(F32)
            xif_ref[...] = jnp.exp((pos + 1.0) * lgf)
            xib_ref[...] = jnp.exp((seq - pos) * lgb)

    if latent:
        s0f = s0f_ref[...].astype(BF16)
        s0b = s0b_ref[...].astype(BF16)
    for i in range(n_sub):
        k_f32 = k_ref[i * seq:(i + 1) * seq, :]
        kb = k_f32.astype(BF16)
        v = v_ref[i * seq:(i + 1) * seq, :]
        for t in range(seq // tq):
            rows = slice(i * seq + t * tq, i * seq + (t + 1) * tq)
            q = q_ref[rows, :]
            raw = lax.dot_general(q, kb, _NT, preferred_element_type=F32)
            p = (raw * d_ref[t * tq:(t + 1) * tq, :]).astype(BF16)
            o = jnp.dot(p, v, preferred_element_type=F32)
            if latent:
                tile = slice(t * tq, (t + 1) * tq)
                o = o + jnp.dot(q, s0f, preferred_element_type=F32) * xif_ref[tile, :]
                o = o + jnp.dot(q, s0b, preferred_element_type=F32) * xib_ref[tile, :]
            mu = jnp.mean(o, axis=-1, keepdims=True)
            oc = o - mu
            var = jnp.mean(oc * oc, axis=-1, keepdims=True)
            y = (oc * lax.rsqrt(var + EPS)) * rn_ref[...]
            o_ref[rows, :] = (y * sg_ref[rows, :]).astype(BF16)
        if not latent:
            j = lax.broadcasted_iota(jnp.int32, (seq, RET_KEY_DIM), 0).astype(F32)
            kz_f = (k_f32 * jnp.exp((seq - 1.0 - j) * lgf)).astype(BF16)
            kz_b = (k_f32 * jnp.exp(j * lgb)).astype(BF16)
            sf_ref[i] = lax.dot_general(kz_f, v, _TN, preferred_element_type=F32)
            sb_ref[i] = lax.dot_general(kz_b, v, _TN, preferred_element_type=F32)


def _ret_call(lgf, lgb, q_ret, k_ret, v_ret, sg, ret_norm, s0f, s0b, *, latent):
    if latent:
        seq, nb, row0, n_sub = DEC_SEQ, DEC_BATCH, T_P, 1
    else:
        seq, nb, row0, n_sub = SEQ, BATCH, 0, 8
    tq = 256
    rows = n_sub * seq
    rb = row0 // rows
    smem = pl.BlockSpec(memory_space=pltpu.SMEM)
    qk_spec = pl.BlockSpec((rows, RET_KEY_DIM), lambda h, b: (rb + b, h))
    v_spec = pl.BlockSpec((rows, RET_VAL_DIM), lambda h, b: (rb + b, h))
    o_spec = pl.BlockSpec((rows, RET_VAL_DIM), lambda h, b: (b, h))
    in_specs = [smem, smem, qk_spec, qk_spec, v_spec, v_spec,
                pl.BlockSpec((1, RET_VAL_DIM), lambda h, b: (0, h))]
    args = [lgf, lgb, q_ret, k_ret, v_ret, sg, ret_norm]
    o_shape = jax.ShapeDtypeStruct((nb * seq, RET_V), BF16)
    if latent:
        st_spec = pl.BlockSpec((None, None, None, RET_KEY_DIM, RET_VAL_DIM),
                               lambda h, b: (b, 0, h, 0, 0))
        in_specs += [st_spec, st_spec]
        args += [s0f, s0b]
        out_specs = o_spec
        out_shape = o_shape
    else:
        st_spec = pl.BlockSpec((n_sub, None, None, RET_KEY_DIM, RET_VAL_DIM),
                               lambda h, b: (b, 0, h, 0, 0))
        st_shape = jax.ShapeDtypeStruct((BATCH, 1, N_RET_HEADS, RET_KEY_DIM, RET_VAL_DIM), F32)
        out_specs = [o_spec, st_spec, st_spec]
        out_shape = [o_shape, st_shape, st_shape]
    est = _nbytes((seq, seq), F32) + 4 * n_sub * _nbytes((tq, seq), F32)
    est += 2 * (_nbytes((rows, RET_KEY_DIM), BF16) + _nbytes((rows, RET_KEY_DIM), F32))
    est += 2 * (2 * _nbytes((rows, RET_VAL_DIM), BF16) + _nbytes((rows, RET_VAL_DIM), F32))
    est += 8 * n_sub * _nbytes((RET_KEY_DIM, RET_VAL_DIM), F32)
    return pl.pallas_call(
        functools.partial(_ret_body, seq=seq, tq=tq, latent=latent, n_sub=n_sub),
        grid=(N_RET_HEADS, nb // n_sub),
        in_specs=in_specs,
        out_specs=out_specs,
        out_shape=out_shape,
        scratch_shapes=[pltpu.VMEM((seq, seq), F32)]
        + ([pltpu.VMEM((seq, RET_VAL_DIM), F32)] * 2 if latent else []),
        compiler_params=_params(est, 2),
        name="ret_latent" if latent else "ret_prompt",
    )(*args)


def _final_body(x_ref, g_ref, yp_ref, ys_ref, *, n_p):
    x = x_ref[...]
    y = x * lax.rsqrt(jnp.mean(x * x, axis=-1, keepdims=True) + EPS) * g_ref[...]
    m = pl.program_id(0)

    @pl.when(m < n_p)
    def _():
        yp_ref[...] = y

    @pl.when(m >= n_p)
    def _():
        ys_ref[...] = y


def _final_call(x, g):
    tm = 1024
    n_p = T_P // tm
    blk = (tm, D_MODEL)
    est = 6 * _nbytes(blk, F32) + _nbytes(blk, F32)
    return pl.pallas_call(
        functools.partial(_final_body, n_p=n_p),
        grid=(T // tm,),
        in_specs=[pl.BlockSpec(blk, lambda m: (m, 0)), pl.BlockSpec((1, D_MODEL), lambda m: (0, 0))],
        out_specs=[
            pl.BlockSpec(blk, lambda m: (jnp.minimum(m, n_p - 1), 0)),
            pl.BlockSpec(blk, lambda m: (jnp.maximum(m - n_p, 0), 0)),
        ],
        out_shape=[jax.ShapeDtypeStruct((T_P, D_MODEL), F32), jax.ShapeDtypeStruct((T_S, D_MODEL), F32)],
        compiler_params=_params(est, 1),
        name="final_norm",
    )(x, g)


def _rope_tables():
    rows = DEC_SEQ // GRID_W
    row = jnp.repeat(jnp.arange(rows, dtype=F32), GRID_W)
    col = jnp.tile(jnp.arange(GRID_W, dtype=F32), rows)
    quarter = HEAD_DIM // 4
    inv_freq = ROPE_THETA ** (-jnp.arange(quarter, dtype=F32) / quarter)
    ang_r = row[:, None] * inv_freq[None, :]
    ang_c = col[:, None] * inv_freq[None, :]
    cr, sr, cc, sc = jnp.cos(ang_r), jnp.sin(ang_r), jnp.cos(ang_c), jnp.sin(ang_c)
    zero = jnp.zeros_like(sr)
    cos = jnp.concatenate([cr, cr, cc, cc], axis=-1)
    s_lo = jnp.concatenate([-sr, zero, -sc, zero], axis=-1)
    s_hi = jnp.concatenate([zero, sr, zero, sc], axis=-1)
    ident = jnp.zeros((DEC_SEQ, HEAD_DIM), F32)
    return (jnp.concatenate([ident + 1.0, cos], axis=0),
            jnp.concatenate([ident, s_lo], axis=0),
            jnp.concatenate([ident, s_hi], axis=0))


def kernel(x_prompt, x_sample, cache_attn_k, cache_attn_v, state_ret_fwd, state_ret_bwd, c, c_ctx,
           norm_attn, norm_ffn, w_mod, b_mod, w_in, q_norm, k_norm, ret_decay_fwd, ret_decay_bwd,
           ret_norm, w_branch_attn, w_branch_ret, w_out, w_ffn_gate, w_ffn_up, w_ffn_down, final_norm):
    xp = x_prompt.reshape(T_P, D_MODEL)
    xs = x_sample.reshape(T_S, D_MODEL)
    lgf = jax.nn.log_sigmoid(ret_decay_fwd[0].astype(F32))
    lgb = jax.nn.log_sigmoid(ret_decay_bwd[0].astype(F32))
    cos, s_lo, s_hi = _rope_tables()

    c_all = jnp.concatenate(
        [c_ctx[None, :], c, jnp.zeros((MOD_ROWS - 1 - DEC_BATCH, D_MODEL), F32)], axis=0)
    mod = _mod_call(c_all, w_mod[0], b_mod)
    mod3 = mod.reshape(MOD_ROWS, 1, 6 * D_MODEL)
    h = _prenorm_call(xp, xs, norm_attn, mod3, 0, 1, "prenorm_attn")

    w_in0 = w_in[0]
    tm = 1024
    tbl = pl.BlockSpec((DEC_SEQ, HEAD_DIM), lambda n, m: (jnp.where(m < T_P // tm, 0, 1), 0))
    hd = pl.BlockSpec((1, HEAD_DIM), lambda n, m: (0, 0))
    rope_extras = [(cos, tbl), (s_lo, tbl), (s_hi, tbl)]

    def proj(col0, n_cols, tn, epilogue, extras, outs, name):
        return _matmul_call([h], [(w_in0, col0)], [(0, 0)], n_cols, tm, tn, epilogue, extras, outs, name)

    def full(n_cols, dtype):
        return jax.ShapeDtypeStruct((T, n_cols), dtype)

    (q_att,) = proj(COL_Q, ATTN_Q, 1024, functools.partial(_epi_q, tn=1024),
                    [(q_norm, hd)] + rope_extras,
                    [(full(ATTN_Q, BF16), _tile_spec(tm, 1024))], "proj_q")
    prompt_kv = jax.ShapeDtypeStruct((T_P + tm, ATTN_KV), F32)
    k_att, k_new = proj(COL_K, ATTN_KV, ATTN_KV, functools.partial(_epi_k, tn=ATTN_KV),
                        [(k_norm, hd)] + rope_extras,
                        [(full(ATTN_KV, BF16), _tile_spec(tm, ATTN_KV)),
                         (prompt_kv, _prompt_tile_spec(tm, ATTN_KV))], "proj_k")
    v_att, v_new = proj(COL_V, ATTN_KV, ATTN_KV, _epi_v, [],
                        [(full(ATTN_KV, BF16), _tile_spec(tm, ATTN_KV)),
                         (prompt_kv, _prompt_tile_spec(tm, ATTN_KV))], "proj_v")

    def proj_tall(col0, n_cols, epilogue, dtype, name):
        return _matmul_call([h], [(w_in0, col0)], [(0, 0)], n_cols, 2048, 1024, epilogue, [],
                            [(full(n_cols, dtype), _tile_spec(2048, 1024))], name,
                            stream=True, kc=512)

    (q_ret,) = proj_tall(COL_QR, RET_QK, _epi_cast, BF16, "proj_qr")
    (k_ret,) = proj_tall(COL_KR, RET_QK, functools.partial(_epi_scale, scale=RET_KEY_DIM ** -0.5),
                         F32, "proj_kr")
    (v_ret,) = proj_tall(COL_VR, RET_V, _epi_cast, BF16, "proj_vr")
    (sg,) = proj_tall(COL_GR, RET_V, _epi_silu, F32, "proj_gr")
    (gates,) = proj_tall(COL_GATES, 2 * D_MODEL, _epi_sigmoid, F32, "proj_gates")

    o_a_p = _attn_call(q_att, k_att, v_att, None, None, latent=False)
    o_a_s = _attn_call(q_att, k_att, v_att, cache_attn_k, cache_attn_v, latent=True)
    o_r_p, new_sf, new_sb = _ret_call(lgf, lgb, q_ret, k_ret, v_ret, sg, ret_norm,
                                      None, None, latent=False)
    o_r_s = _ret_call(lgf, lgb, q_ret, k_ret, v_ret, sg, ret_norm,
                      state_ret_fwd, state_ret_bwd, latent=True)

    tm2, tn2 = 512, 1024
    (merged,) = _matmul_call(
        [(o_a_p, o_a_s), (o_r_p, o_r_s)], [(w_branch_attn[0], 0), (w_branch_ret[0], 0)],
        [(0, 0), (1, 1)], D_MODEL, tm2, tn2, _epi_merge,
        [(gates, _tile_spec(tm2, tn2)), (gates, _tile_spec(tm2, tn2, D_MODEL // tn2))],
        [(full(D_MODEL, BF16), _tile_spec(tm2, tn2))], "merge", stream=True)

    tm3 = 512
    n_p3 = T_P // tm3
    row_vec = pl.BlockSpec((1, D_MODEL), lambda n, m: (0, 0))
    x1, h2 = _matmul_call(
        [merged], [(w_out[0], 0)], [(0, 0)], D_MODEL, tm3, D_MODEL,
        functools.partial(_epi_residual_norm, tm=tm3),
        [(xp, pl.BlockSpec((tm3, D_MODEL), lambda n, m: (jnp.minimum(m, n_p3 - 1), 0))),
         (xs, pl.BlockSpec((tm3, D_MODEL), lambda n, m: (jnp.maximum(m - n_p3, 0), 0))),
         (mod3, _mod_spec(tm3, D_MODEL, lambda n: 2, 2)),
         (norm_ffn, row_vec),
         (mod3, _mod_spec(tm3, D_MODEL, lambda n: 4, 2)),
         (mod3, _mod_spec(tm3, D_MODEL, lambda n: 3, 2))],
        [(full(D_MODEL, F32), _tile_spec(tm3, D_MODEL)),
         (full(D_MODEL, BF16), _tile_spec(tm3, D_MODEL))], "out_proj", stream=True)

    (act,) = _matmul_call(
        [h2], [(w_ffn_gate[0], 0), (w_ffn_up[0], 0)], [(0, 0), (0, 1)],
        D_FF, 2048, 512, _epi_swiglu, [],
        [(full(D_FF, BF16), _tile_spec(2048, 512))], "ffn_up", stream=True, kc=512)

    tm4, tn4 = 512, 1024
    (y_pre,) = _matmul_call(
        [act], [(w_ffn_down[0], 0)], [(0, 0)], D_MODEL, tm4, tn4, _epi_residual,
        [(x1, _tile_spec(tm4, tn4)),
         (mod3, _mod_spec(tm4, tn4, lambda n: 5 * (D_MODEL // tn4) + n, 2))],
        [(full(D_MODEL, F32), _tile_spec(tm4, tn4))], "ffn_down", stream=True, kc=512)

    y_p, y_s = _final_call(y_pre, final_norm[None, :])

    return (y_p.reshape(BATCH, SEQ, D_MODEL),
            y_s.reshape(DEC_BATCH, DEC_SEQ, D_MODEL),
            k_new[:T_P].reshape(BATCH, 1, SEQ, N_KV_HEADS, HEAD_DIM),
            v_new[:T_P].reshape(BATCH, 1, SEQ, N_KV_HEADS, HEAD_DIM),
            new_sf, new_sb)
```

```python
import functools

import jax
import jax.numpy as jnp
from jax import lax
from jax.experimental import pallas as pl
from jax.experimental.pallas import tpu as pltpu

D_MODEL = 2048
BATCH = 16
SEQ = 256
DEC_BATCH = 8
DEC_SEQ = 1024
PAST_LEN = 512
GRID_W = 64
N_HEADS = 16
N_KV_HEADS = 4
HEAD_DIM = 128
ROPE_THETA = 10000.0
N_RET_HEADS = 8
RET_KEY_DIM = 128
RET_VAL_DIM = 256
D_FF = 5632
EPS = 1e-6

ATTN_Q = N_HEADS * HEAD_DIM
ATTN_KV = N_KV_HEADS * HEAD_DIM
RET_QK = N_RET_HEADS * RET_KEY_DIM
RET_V = N_RET_HEADS * RET_VAL_DIM
GQA_GROUP = N_HEADS // N_KV_HEADS

COL_Q = 0
COL_K = COL_Q + ATTN_Q
COL_V = COL_K + ATTN_KV
COL_QR = COL_V + ATTN_KV
COL_KR = COL_QR + RET_QK
COL_VR = COL_KR + RET_QK
COL_GR = COL_VR + RET_V
COL_GATES = COL_GR + RET_V

T_P = BATCH * SEQ
T_S = DEC_BATCH * DEC_SEQ
T = T_P + T_S
MOD_ROWS = 16

V7X_VMEM_BYTES = 64 * 1024 * 1024
VMEM_CAP = V7X_VMEM_BYTES - 6 * 1024 * 1024

BF16 = jnp.bfloat16
F32 = jnp.float32


def _nbytes(shape, dtype):
    n = 1
    for s in shape:
        if s is not None:
            n *= s
    return n * jnp.dtype(dtype).itemsize


def _params(vmem_estimate, n_grid):
    limit = min(VMEM_CAP, int(vmem_estimate) + 8 * 1024 * 1024)
    return pltpu.CompilerParams(
        dimension_semantics=("arbitrary",) * n_grid, vmem_limit_bytes=limit)


def _mod_row(m, tm):
    n_p = T_P // tm
    return jnp.where(m < n_p, 0, 1 + (m - n_p) // (DEC_SEQ // tm))


def _mod_spec(tm, width, col_block, grid_rank):
    if grid_rank == 1:
        return pl.BlockSpec((None, 1, width), lambda m: (_mod_row(m, tm), 0, col_block(0)))
    return pl.BlockSpec((None, 1, width), lambda n, m: (_mod_row(m, tm), 0, col_block(n)))


def _mod_body(c_ref, w_ref, b_ref, o_ref):
    a = jax.nn.silu(c_ref[...]).astype(BF16)
    w = w_ref[...].astype(BF16)
    o_ref[...] = jnp.dot(a, w, preferred_element_type=F32) + b_ref[...]


def _mod_call(c_all, w_mod, b_mod):
    tn = 1024
    n_out = 6 * D_MODEL
    est = 2 * _nbytes((D_MODEL, tn), F32) + _nbytes((D_MODEL, tn), BF16)
    return pl.pallas_call(
        _mod_body,
        grid=(n_out // tn,),
        in_specs=[
            pl.BlockSpec((MOD_ROWS, D_MODEL), lambda n: (0, 0)),
            pl.BlockSpec((D_MODEL, tn), lambda n: (0, n)),
            pl.BlockSpec((1, tn), lambda n: (0, n)),
        ],
        out_specs=pl.BlockSpec((MOD_ROWS, tn), lambda n: (0, n)),
        out_shape=jax.ShapeDtypeStruct((MOD_ROWS, n_out), F32),
        compiler_params=_params(est, 1),
        name="mod_table",
    )(c_all, w_mod, b_mod)


def _modulated_norm(x, g, sc, sh):
    y = x * lax.rsqrt(jnp.mean(x * x, axis=-1, keepdims=True) + EPS)
    return (y * g) * (1.0 + sc) + sh


def _prenorm_body(xp_ref, xs_ref, g_ref, sc_ref, sh_ref, o_ref, *, n_p):
    m = pl.program_id(0)

    @pl.when(m < n_p)
    def _():
        o_ref[...] = _modulated_norm(xp_ref[...], g_ref[...], sc_ref[...], sh_ref[...]).astype(BF16)

    @pl.when(m >= n_p)
    def _():
        o_ref[...] = _modulated_norm(xs_ref[...], g_ref[...], sc_ref[...], sh_ref[...]).astype(BF16)


def _prenorm_call(xp, xs, norm_w, mod3, sh_blk, sc_blk, name):
    tm = 1024
    n_p = T_P // tm
    blk = (tm, D_MODEL)
    in_specs = [
        pl.BlockSpec(blk, lambda m: (jnp.minimum(m, n_p - 1), 0)),
        pl.BlockSpec(blk, lambda m: (jnp.maximum(m - n_p, 0), 0)),
        pl.BlockSpec((1, D_MODEL), lambda m: (0, 0)),
        _mod_spec(tm, D_MODEL, lambda n: sc_blk, 1),
        _mod_spec(tm, D_MODEL, lambda n: sh_blk, 1),
    ]
    est = 4 * _nbytes(blk, F32) + 2 * _nbytes(blk, BF16) + _nbytes(blk, F32)
    return pl.pallas_call(
        functools.partial(_prenorm_body, n_p=n_p),
        grid=(T // tm,),
        in_specs=in_specs,
        out_specs=pl.BlockSpec(blk, lambda m: (m, 0)),
        out_shape=jax.ShapeDtypeStruct((T, D_MODEL), BF16),
        compiler_params=_params(est, 1),
        name=name,
    )(xp, xs, norm_w, mod3, mod3)


def _matmul_body(*refs, lhs_split, n_rhs, col0s, pairs, n_extra, n_out, epilogue,
                 tm, tn, kc, row_chunk, stream, single_tile):
    n_lhs_refs = sum(2 if s else 1 for s in lhs_split)
    lhs_refs = refs[:n_lhs_refs]
    pos = n_lhs_refs
    rhs = refs[pos:pos + n_rhs]
    pos += n_rhs
    extras = refs[pos:pos + n_extra]
    pos += n_extra
    outs = refs[pos:pos + n_out]
    pos += n_out
    wb = refs[pos:pos + n_rhs]
    n, m = pl.program_id(0), pl.program_id(1)
    is_prompt = m < T_P // tm

    def lhs_rows(i, rows):
        first = sum(2 if s else 1 for s in lhs_split[:i])
        if lhs_split[i]:
            return jnp.where(is_prompt, lhs_refs[first][rows, :], lhs_refs[first + 1][rows, :])
        return lhs_refs[first][rows, :]

    def compute(weight):
        for r in range(tm // row_chunk):
            rows = slice(r * row_chunk, (r + 1) * row_chunk)
            zs = [jnp.dot(lhs_rows(i, rows), weight(j), preferred_element_type=F32) for i, j in pairs]
            epilogue(zs, extras, outs, rows)

    if not stream:
        @pl.when(m == 0)
        def _():
            for w_ref, wb_ref in zip(rhs, wb):
                wb_ref[...] = w_ref[...].astype(BF16)

        compute(lambda j: wb[j][...])
        return

    stage = refs[pos + n_rhs:pos + 2 * n_rhs]
    sems = refs[pos + 2 * n_rhs:pos + 3 * n_rhs]
    n_tiles = pl.num_programs(0)
    n_chunks = wb[0].shape[1] // kc

    def chunk_copy(j, tile, chunk, slot):
        col = pl.multiple_of(col0s[j] + tile * tn, tn)
        row = pl.multiple_of(chunk * kc, kc)
        return pltpu.make_async_copy(
            rhs[j].at[pl.ds(row, kc), pl.ds(col, tn)], stage[j].at[slot], sems[j].at[slot])

    nxt = jnp.minimum(n + 1, n_tiles - 1)
    chunk = jnp.minimum(m, n_chunks - 1)
    first = jnp.logical_and(n == 0, m == 0)

    def start_prefetch():
        for j in range(n_rhs):
            chunk_copy(j, nxt, chunk, 0).start()

    @pl.when(first)
    def _():
        for j in range(n_rhs):
            chunk_copy(j, 0, 0, 0).start()
            for c in range(n_chunks):
                if c + 1 < n_chunks:
                    chunk_copy(j, 0, c + 1, (c + 1) % 2).start()
                chunk_copy(j, 0, c, c % 2).wait()
                wb[j][0, c * kc:(c + 1) * kc, :] = stage[j][c % 2].astype(BF16)
        if not single_tile:
            start_prefetch()

    if single_tile:
        compute(lambda j: wb[j][0])
        return

    pl.when(jnp.logical_not(first))(start_prefetch)

    for cur in range(2):
        @pl.when(n % 2 == cur)
        def _(cur=cur):
            compute(lambda j: wb[j][cur])
            for j in range(n_rhs):
                chunk_copy(j, nxt, chunk, 0).wait()
                wb[j][1 - cur, pl.ds(pl.multiple_of(chunk * kc, kc), kc), :] = stage[j][0].astype(BF16)


def _matmul_call(lhs_list, rhs_list, pairs, n_cols, tm, tn, epilogue, extras, outs, name,
                 row_chunk=256, stream=False, kc=256):
    lhs_split = tuple(isinstance(a, tuple) for a in lhs_list)
    k_dim = (lhs_list[0][0] if lhs_split[0] else lhs_list[0]).shape[1]
    n_p = T_P // tm
    grid = (n_cols // tn, T // tm)
    in_specs, lhs_args = [], []
    for a in lhs_list:
        if isinstance(a, tuple):
            in_specs.append(pl.BlockSpec((tm, k_dim), lambda n, m: (jnp.minimum(m, n_p - 1), 0)))
            in_specs.append(pl.BlockSpec((tm, k_dim), lambda n, m: (jnp.maximum(m - n_p, 0), 0)))
            lhs_args += list(a)
        else:
            in_specs.append(pl.BlockSpec((tm, k_dim), lambda n, m: (m, 0)))
            lhs_args.append(a)
    est = 2 * len(lhs_args) * _nbytes((tm, k_dim), BF16)
    single_tile = stream and grid[0] == 1
    if stream:
        assert k_dim % kc == 0 and k_dim // kc <= grid[1], (k_dim, kc, grid)
        n_buf = 1 if single_tile else 2
        in_specs += [pl.BlockSpec(memory_space=pl.ANY) for _ in rhs_list]
        est += len(rhs_list) * (n_buf * _nbytes((k_dim, tn), BF16) + 2 * _nbytes((kc, tn), F32))
        scratch = [pltpu.VMEM((n_buf, k_dim, tn), BF16) for _ in rhs_list]
        scratch += [pltpu.VMEM((2, kc, tn), F32) for _ in rhs_list]
        scratch += [pltpu.SemaphoreType.DMA((2,)) for _ in rhs_list]
    else:
        for _, col0 in rhs_list:
            in_specs.append(pl.BlockSpec((k_dim, tn), lambda n, m, c=col0 // tn: (0, c + n)))
        est += len(rhs_list) * (2 * _nbytes((k_dim, tn), F32) + _nbytes((k_dim, tn), BF16))
        scratch = [pltpu.VMEM((k_dim, tn), BF16) for _ in rhs_list]
    in_specs += [spec for _, spec in extras]
    for arr, spec in list(extras) + list(outs):
        est += 2 * _nbytes(spec.block_shape, arr.dtype)
    est += 3 * len(pairs) * _nbytes((row_chunk, tn), F32)
    body = functools.partial(
        _matmul_body, lhs_split=lhs_split, n_rhs=len(rhs_list),
        col0s=tuple(c for _, c in rhs_list), pairs=tuple(pairs),
        n_extra=len(extras), n_out=len(outs), epilogue=epilogue,
        tm=tm, tn=tn, kc=kc, row_chunk=row_chunk, stream=stream, single_tile=single_tile)
    res = pl.pallas_call(
        body,
        grid=grid,
        in_specs=in_specs,
        out_specs=[spec for _, spec in outs],
        out_shape=[s for s, _ in outs],
        scratch_shapes=scratch,
        compiler_params=_params(est, 2),
        name=name,
    )(*lhs_args, *[w for w, _ in rhs_list], *[a for a, _ in extras])
    return res


def _tile_spec(tm, tn, col_block0=0):
    return pl.BlockSpec((tm, tn), lambda n, m, c=col_block0: (m, c + n))


def _prompt_tile_spec(tm, tn):
    n_p = T_P // tm
    return pl.BlockSpec((tm, tn), lambda n, m: (jnp.minimum(m, n_p), n))


def _head_rms(x, g):
    return x * lax.rsqrt(jnp.mean(x * x, axis=-1, keepdims=True) + EPS) * g


def _rope(y, c, s_lo, s_hi):
    return y * c + pltpu.roll(y, HEAD_DIM - 32, axis=1) * s_lo + pltpu.roll(y, 32, axis=1) * s_hi


def _epi_q(zs, extras, outs, rows, *, tn):
    g_ref, c_ref, slo_ref, shi_ref = extras
    (q_ref,) = outs
    (z,) = zs
    for h in range(tn // HEAD_DIM):
        sl = slice(h * HEAD_DIM, (h + 1) * HEAD_DIM)
        y = _head_rms(z[:, sl], g_ref[...])
        q_ref[rows, sl] = _rope(y, c_ref[rows, :], slo_ref[rows, :], shi_ref[rows, :]).astype(BF16)


def _epi_k(zs, extras, outs, rows, *, tn):
    g_ref, c_ref, slo_ref, shi_ref = extras
    k_att_ref, k_new_ref = outs
    (z,) = zs
    for h in range(tn // HEAD_DIM):
        sl = slice(h * HEAD_DIM, (h + 1) * HEAD_DIM)
        y = _head_rms(z[:, sl], g_ref[...])
        k_new_ref[rows, sl] = y
        k_att_ref[rows, sl] = _rope(y, c_ref[rows, :], slo_ref[rows, :], shi_ref[rows, :]).astype(BF16)


def _epi_v(zs, extras, outs, rows):
    v_att_ref, v_new_ref = outs
    v_att_ref[rows, :] = zs[0].astype(BF16)
    v_new_ref[rows, :] = zs[0]


def _epi_cast(zs, extras, outs, rows):
    outs[0][rows, :] = zs[0].astype(outs[0].dtype)


def _epi_scale(zs, extras, outs, rows, *, scale):
    outs[0][rows, :] = zs[0] * scale


def _epi_silu(zs, extras, outs, rows):
    outs[0][rows, :] = jax.nn.silu(zs[0])


def _epi_sigmoid(zs, extras, outs, rows):
    outs[0][rows, :] = jax.nn.sigmoid(zs[0])


def _epi_merge(zs, extras, outs, rows):
    ga_ref, gr_ref = extras
    outs[0][rows, :] = (ga_ref[rows, :] * zs[0] + gr_ref[rows, :] * zs[1]).astype(BF16)


def _epi_residual_norm(zs, extras, outs, rows, *, tm):
    xp_ref, xs_ref, gate_ref, g_ref, sc_ref, sh_ref = extras
    x1_ref, h_ref = outs
    x = jnp.where(pl.program_id(1) < T_P // tm, xp_ref[rows, :], xs_ref[rows, :])
    x1 = x + gate_ref[...] * zs[0]
    x1_ref[rows, :] = x1
    h_ref[rows, :] = _modulated_norm(x1, g_ref[...], sc_ref[...], sh_ref[...]).astype(BF16)


def _epi_swiglu(zs, extras, outs, rows):
    outs[0][rows, :] = (jax.nn.silu(zs[0]) * zs[1]).astype(BF16)


def _epi_residual(zs, extras, outs, rows):
    x_ref, gate_ref = extras
    outs[0][rows, :] = x_ref[rows, :] + gate_ref[...] * zs[0]


_NT = (((1,), (1,)), ((), ()))


_SOFTMAX_EXP2_SCALE = HEAD_DIM ** -0.5 * 1.4426950408889634


def _attn_body(*refs, has_ctx, seq, tq, n_sub):
    if has_ctx:
        q_ref, k_ref, v_ref, ck_ref, cv_ref, o_ref = refs
        g = pl.program_id(1)
        ck = ck_ref[:, g, :].astype(BF16)
        cv = cv_ref[:, g, :].astype(BF16)
    else:
        q_ref, k_ref, v_ref, o_ref = refs
    chain_rows = tq
    for i, c in [(i, c) for i in range(n_sub) for c in range(tq // chain_rows)]:
        k = k_ref[i * seq:(i + 1) * seq, :]
        v = v_ref[i * seq:(i + 1) * seq, :]
        rows = slice(i * tq + c * chain_rows, i * tq + (c + 1) * chain_rows)
        for h in range(GQA_GROUP):
            sl = slice(h * HEAD_DIM, (h + 1) * HEAD_DIM)
            q = q_ref[rows, sl]
            s_new = lax.dot_general(q, k, _NT, preferred_element_type=F32)
            mx = jnp.max(s_new, axis=-1, keepdims=True)
            if has_ctx:
                s_ctx = lax.dot_general(q, ck, _NT, preferred_element_type=F32)
                mx = jnp.maximum(mx, jnp.max(s_ctx, axis=-1, keepdims=True))
                p_ctx = jnp.exp2((s_ctx - mx) * _SOFTMAX_EXP2_SCALE)
                p_new = jnp.exp2((s_new - mx) * _SOFTMAX_EXP2_SCALE)
                den = jnp.sum(p_ctx, axis=-1, keepdims=True) + jnp.sum(p_new, axis=-1, keepdims=True)
                o = jnp.dot(p_ctx.astype(BF16), cv, preferred_element_type=F32)
                o = o + jnp.dot(p_new.astype(BF16), v, preferred_element_type=F32)
            else:
                p_new = jnp.exp2((s_new - mx) * _SOFTMAX_EXP2_SCALE)
                den = jnp.sum(p_new, axis=-1, keepdims=True)
                o = jnp.dot(p_new.astype(BF16), v, preferred_element_type=F32)
            o_ref[rows, sl] = (o * (1.0 / den)).astype(BF16)


def _attn_call(q_att, k_att, v_att, ctx_k, ctx_v, *, latent):
    gw = GQA_GROUP * HEAD_DIM
    if latent:
        seq, nb, tq, n_sub = DEC_SEQ, DEC_BATCH, 1024, 1
        row0 = T_P
    else:
        seq, nb, tq, n_sub = SEQ, BATCH, SEQ, 4
        row0 = 0
    nqt = seq // tq
    q_rows, kv_rows = n_sub * tq, n_sub * seq
    q_spec = pl.BlockSpec((q_rows, gw), lambda b, g, t: (row0 // q_rows + b * nqt + t, g))
    o_spec = pl.BlockSpec((q_rows, gw), lambda b, g, t: (b * nqt + t, g))
    kv_spec = pl.BlockSpec((kv_rows, HEAD_DIM), lambda b, g, t: (row0 // kv_rows + b, g))
    in_specs = [q_spec, kv_spec, kv_spec]
    args = [q_att, k_att, v_att]
    n_keys = seq
    if latent:
        c_spec = pl.BlockSpec((None, None, PAST_LEN, N_KV_HEADS, HEAD_DIM),
                              lambda b, g, t: (b, 0, 0, 0, 0))
        in_specs += [c_spec, c_spec]
        args += [ctx_k, ctx_v]
        n_keys += PAST_LEN
    est = 4 * _nbytes((q_rows, gw), BF16) + 4 * _nbytes((kv_rows, HEAD_DIM), BF16)
    est += 4 * _nbytes((PAST_LEN, 8, HEAD_DIM), F32)
    est += 3 * GQA_GROUP * n_sub * _nbytes((tq, n_keys), F32)
    return pl.pallas_call(
        functools.partial(_attn_body, has_ctx=latent, seq=seq, tq=tq, n_sub=n_sub),
        grid=(nb // n_sub, N_KV_HEADS, nqt),
        in_specs=in_specs,
        out_specs=o_spec,
        out_shape=jax.ShapeDtypeStruct((nb * seq, ATTN_Q), BF16),
        compiler_params=_params(est, 3),
        name="attn_latent" if latent else "attn_prompt",
    )(*args)


_TN = (((0,), (0,)), ((), ()))


def _ret_body(*refs, seq, chunk, tq, latent, n_sub):
    if latent:
        (lgf_ref, lgb_ref, q_ref, k_ref, v_ref, sg_ref, rn_ref, s0f_ref, s0b_ref,
         o_ref, d_ref, zf_ref, zb_ref, xif_ref, xib_ref) = refs
    else:
        (lgf_ref, lgb_ref, q_ref, k_ref, v_ref, sg_ref, rn_ref,
         o_ref, sf_ref, sb_ref, d_ref, zf_ref, zb_ref) = refs
    h = pl.program_id(0)
    lgf = lgf_ref[h]
    lgb = lgb_ref[h]
    n_c = seq // chunk

    @pl.when(pl.program_id(1) == 0)
    def _():
        for t in range(chunk // tq):
            i = lax.broadcasted_iota(jnp.int32, (tq, chunk), 0) + t * tq
            j = lax.broadcasted_iota(jnp.int32, (tq, chunk), 1)
            diff = (i - j).astype(F32)
            arg = jnp.where(diff >= 0, diff * lgf, -diff * lgb)
            d_ref[t * tq:(t + 1) * tq, :] = jnp.exp(arg) * jnp.where(diff == 0, 2.0, 1.0)
        loc_k = lax.broadcasted_iota(jnp.int32, (chunk, RET_KEY_DIM), 0).astype(F32)
        zf_ref[...] = jnp.exp((chunk - 1.0 - loc_k) * lgf)
        zb_ref[...] = jnp.exp(loc_k * lgb)
        if latent:
            loc_v = lax.broadcasted_iota(jnp.int32, (chunk, RET_VAL_DIM), 0).astype(F32)
            xif_ref[...] = jnp.exp((loc_v + 1.0) * lgf)
            xib_ref[...] = jnp.exp((chunk - loc_v) * lgb)

    def chunk_state(k_f32, z_ref, v):
        return lax.dot_general((k_f32 * z_ref[...]).astype(BF16), v, _TN, preferred_element_type=F32)

    for i in range(n_sub):
        ks = [k_ref[i * seq + c * chunk:i * seq + (c + 1) * chunk, :] for c in range(n_c)]
        vs = [v_ref[i * seq + c * chunk:i * seq + (c + 1) * chunk, :] for c in range(n_c)]
        if latent:
            st_shape = (RET_KEY_DIM, RET_VAL_DIM)
            gch_f = jnp.exp(jnp.zeros(st_shape, F32) + chunk * lgf)
            gch_b = jnp.exp(jnp.zeros(st_shape, F32) + chunk * lgb)
            sf_in = [s0f_ref[...]]
            for c in range(1, n_c):
                sf_in.append(gch_f * sf_in[-1] + chunk_state(ks[c - 1], zf_ref, vs[c - 1]))
            sb_in = [s0b_ref[...]]
            for c in range(n_c - 2, -1, -1):
                sb_in.insert(0, gch_b * sb_in[0] + chunk_state(ks[c + 1], zb_ref, vs[c + 1]))
        for c in range(n_c):
            kb = ks[c].astype(BF16)
            if latent:
                sf_c = sf_in[c].astype(BF16)
                sb_c = sb_in[c].astype(BF16)
            for t in range(chunk // tq):
                tile = slice(t * tq, (t + 1) * tq)
                rows = slice(i * seq + c * chunk + t * tq, i * seq + c * chunk + (t + 1) * tq)
                q = q_ref[rows, :]
                raw = lax.dot_general(q, kb, _NT, preferred_element_type=F32)
                p = (raw * d_ref[tile, :]).astype(BF16)
                o = jnp.dot(p, vs[c], preferred_element_type=F32)
                if latent:
                    o = o + jnp.dot(q, sf_c, preferred_element_type=F32) * xif_ref[tile, :]
                    o = o + jnp.dot(q, sb_c, preferred_element_type=F32) * xib_ref[tile, :]
                mu = jnp.mean(o, axis=-1, keepdims=True)
                oc = o - mu
                var = jnp.mean(oc * oc, axis=-1, keepdims=True)
                y = (oc * lax.rsqrt(var + EPS)) * rn_ref[...]
                o_ref[rows, :] = (y * sg_ref[rows, :]).astype(BF16)
        if not latent:
            sf_ref[i] = chunk_state(ks[0], zf_ref, vs[0])
            sb_ref[i] = chunk_state(ks[0], zb_ref, vs[0])


def _ret_call(lgf, lgb, q_ret, k_ret, v_ret, sg, ret_norm, s0f, s0b, *, latent):
    if latent:
        seq, chunk, nb, row0, n_sub = DEC_SEQ, 512, DEC_BATCH, T_P, 1
    else:
        seq, chunk, nb, row0, n_sub = SEQ, SEQ, BATCH, 0, 8
    tq = 256
    rows = n_sub * seq
    rb = row0 // rows
    smem = pl.BlockSpec(memory_space=pltpu.SMEM)
    qk_spec = pl.BlockSpec((rows, RET_KEY_DIM), lambda h, b: (rb + b, h))
    v_spec = pl.BlockSpec((rows, RET_VAL_DIM), lambda h, b: (rb + b, h))
    o_spec = pl.BlockSpec((rows, RET_VAL_DIM), lambda h, b: (b, h))
    in_specs = [smem, smem, qk_spec, qk_spec, v_spec, v_spec,
                pl.BlockSpec((1, RET_VAL_DIM), lambda h, b: (0, h))]
    args = [lgf, lgb, q_ret, k_ret, v_ret, sg, ret_norm]
    o_shape = jax.ShapeDtypeStruct((nb * seq, RET_V), BF16)
    if latent:
        st_spec = pl.BlockSpec((None, None, None, RET_KEY_DIM, RET_VAL_DIM),
                               lambda h, b: (b, 0, h, 0, 0))
        in_specs += [st_spec, st_spec]
        args += [s0f, s0b]
        out_specs = o_spec
        out_shape = o_shape
    else:
        st_spec = pl.BlockSpec((n_sub, None, None, RET_KEY_DIM, RET_VAL_DIM),
                               lambda h, b: (b, 0, h, 0, 0))
        st_shape = jax.ShapeDtypeStruct((BATCH, 1, N_RET_HEADS, RET_KEY_DIM, RET_VAL_DIM), F32)
        out_specs = [o_spec, st_spec, st_spec]
        out_shape = [o_shape, st_shape, st_shape]
    est = _nbytes((chunk, chunk), F32) + 4 * n_sub * _nbytes((tq, chunk), F32)
    est += 2 * _nbytes((chunk, RET_KEY_DIM), F32) + 2 * _nbytes((chunk, RET_VAL_DIM), F32)
    est += 2 * (_nbytes((rows, RET_KEY_DIM), BF16) + _nbytes((rows, RET_KEY_DIM), F32))
    est += 2 * (2 * _nbytes((rows, RET_VAL_DIM), BF16) + _nbytes((rows, RET_VAL_DIM), F32))
    est += 8 * n_sub * _nbytes((RET_KEY_DIM, RET_VAL_DIM), F32)
    return pl.pallas_call(
        functools.partial(_ret_body, seq=seq, chunk=chunk, tq=tq, latent=latent, n_sub=n_sub),
        grid=(N_RET_HEADS, nb // n_sub),
        in_specs=in_specs,
        out_specs=out_specs,
        out_shape=out_shape,
        scratch_shapes=[pltpu.VMEM((chunk, chunk), F32)]
        + [pltpu.VMEM((chunk, RET_KEY_DIM), F32)] * 2
        + ([pltpu.VMEM((chunk, RET_VAL_DIM), F32)] * 2 if latent else []),
        compiler_params=_params(est, 2),
        name="ret_latent" if latent else "ret_prompt",
    )(*args)


def _final_body(x_ref, g_ref, yp_ref, ys_ref, *, n_p):
    x = x_ref[...]
    y = x * lax.rsqrt(jnp.mean(x * x, axis=-1, keepdims=True) + EPS) * g_ref[...]
    m = pl.program_id(0)

    @pl.when(m < n_p)
    def _():
        yp_ref[...] = y

    @pl.when(m >= n_p)
    def _():
        ys_ref[...] = y


def _final_call(x, g):
    tm = 1024
    n_p = T_P // tm
    blk = (tm, D_MODEL)
    est = 6 * _nbytes(blk, F32) + _nbytes(blk, F32)
    return pl.pallas_call(
        functools.partial(_final_body, n_p=n_p),
        grid=(T // tm,),
        in_specs=[pl.BlockSpec(blk, lambda m: (m, 0)), pl.BlockSpec((1, D_MODEL), lambda m: (0, 0))],
        out_specs=[
            pl.BlockSpec(blk, lambda m: (jnp.minimum(m, n_p - 1), 0)),
            pl.BlockSpec(blk, lambda m: (jnp.maximum(m - n_p, 0), 0)),
        ],
        out_shape=[jax.ShapeDtypeStruct((T_P, D_MODEL), F32), jax.ShapeDtypeStruct((T_S, D_MODEL), F32)],
        compiler_params=_params(est, 1),
        name="final_norm",
    )(x, g)


def _rope_tables():
    rows = DEC_SEQ // GRID_W
    row = jnp.repeat(jnp.arange(rows, dtype=F32), GRID_W)
    col = jnp.tile(jnp.arange(GRID_W, dtype=F32), rows)
    quarter = HEAD_DIM // 4
    inv_freq = ROPE_THETA ** (-jnp.arange(quarter, dtype=F32) / quarter)
    ang_r = row[:, None] * inv_freq[None, :]
    ang_c = col[:, None] * inv_freq[None, :]
    cr, sr, cc, sc = jnp.cos(ang_r), jnp.sin(ang_r), jnp.cos(ang_c), jnp.sin(ang_c)
    zero = jnp.zeros_like(sr)
    cos = jnp.concatenate([cr, cr, cc, cc], axis=-1)
    s_lo = jnp.concatenate([-sr, zero, -sc, zero], axis=-1)
    s_hi = jnp.concatenate([zero, sr, zero, sc], axis=-1)
    ident = jnp.zeros((DEC_SEQ, HEAD_DIM), F32)
    return (jnp.concatenate([ident + 1.0, cos], axis=0),
            jnp.concatenate([ident, s_lo], axis=0),
            jnp.concatenate([ident, s_hi], axis=0))


def kernel(x_prompt, x_sample, cache_attn_k, cache_attn_v, state_ret_fwd, state_ret_bwd, c, c_ctx,
           norm_attn, norm_ffn, w_mod, b_mod, w_in, q_norm, k_norm, ret_decay_fwd, ret_decay_bwd,
           ret_norm, w_branch_attn, w_branch_ret, w_out, w_ffn_gate, w_ffn_up, w_ffn_down, final_norm):
    xp = x_prompt.reshape(T_P, D_MODEL)
    xs = x_sample.reshape(T_S, D_MODEL)
    lgf = jax.nn.log_sigmoid(ret_decay_fwd[0].astype(F32))
    lgb = jax.nn.log_sigmoid(ret_decay_bwd[0].astype(F32))
    cos, s_lo, s_hi = _rope_tables()

    c_all = jnp.concatenate(
        [c_ctx[None, :], c, jnp.zeros((MOD_ROWS - 1 - DEC_BATCH, D_MODEL), F32)], axis=0)
    mod = _mod_call(c_all, w_mod[0], b_mod)
    mod3 = mod.reshape(MOD_ROWS, 1, 6 * D_MODEL)
    h = _prenorm_call(xp, xs, norm_attn, mod3, 0, 1, "prenorm_attn")

    w_in0 = w_in[0]
    tm = 1024
    tbl = pl.BlockSpec((DEC_SEQ, HEAD_DIM), lambda n, m: (jnp.where(m < T_P // tm, 0, 1), 0))
    hd = pl.BlockSpec((1, HEAD_DIM), lambda n, m: (0, 0))
    rope_extras = [(cos, tbl), (s_lo, tbl), (s_hi, tbl)]

    def proj(col0, n_cols, tn, epilogue, extras, outs, name):
        return _matmul_call([h], [(w_in0, col0)], [(0, 0)], n_cols, tm, tn, epilogue, extras, outs, name)

    def full(n_cols, dtype):
        return jax.ShapeDtypeStruct((T, n_cols), dtype)

    (q_att,) = proj(COL_Q, ATTN_Q, 1024, functools.partial(_epi_q, tn=1024),
                    [(q_norm, hd)] + rope_extras,
                    [(full(ATTN_Q, BF16), _tile_spec(tm, 1024))], "proj_q")
    prompt_kv = jax.ShapeDtypeStruct((T_P + tm, ATTN_KV), F32)
    k_att, k_new = proj(COL_K, ATTN_KV, ATTN_KV, functools.partial(_epi_k, tn=ATTN_KV),
                        [(k_norm, hd)] + rope_extras,
                        [(full(ATTN_KV, BF16), _tile_spec(tm, ATTN_KV)),
                         (prompt_kv, _prompt_tile_spec(tm, ATTN_KV))], "proj_k")
    v_att, v_new = proj(COL_V, ATTN_KV, ATTN_KV, _epi_v, [],
                        [(full(ATTN_KV, BF16), _tile_spec(tm, ATTN_KV)),
                         (prompt_kv, _prompt_tile_spec(tm, ATTN_KV))], "proj_v")

    def proj_tall(col0, n_cols, epilogue, dtype, name):
        return _matmul_call([h], [(w_in0, col0)], [(0, 0)], n_cols, 2048, 1024, epilogue, [],
                            [(full(n_cols, dtype), _tile_spec(2048, 1024))], name,
                            stream=True, kc=512)

    (q_ret,) = proj(COL_QR, RET_QK, 1024, _epi_cast, [],
                    [(full(RET_QK, BF16), _tile_spec(tm, 1024))], "proj_qr")
    (k_ret,) = proj(COL_KR, RET_QK, 1024, functools.partial(_epi_scale, scale=RET_KEY_DIM ** -0.5), [],
                    [(full(RET_QK, F32), _tile_spec(tm, 1024))], "proj_kr")
    (v_ret,) = proj_tall(COL_VR, RET_V, _epi_cast, BF16, "proj_vr")
    (sg,) = proj_tall(COL_GR, RET_V, _epi_silu, F32, "proj_gr")
    (gates,) = proj_tall(COL_GATES, 2 * D_MODEL, _epi_sigmoid, F32, "proj_gates")

    o_a_p = _attn_call(q_att, k_att, v_att, None, None, latent=False)
    o_a_s = _attn_call(q_att, k_att, v_att, cache_attn_k, cache_attn_v, latent=True)
    o_r_p, new_sf, new_sb = _ret_call(lgf, lgb, q_ret, k_ret, v_ret, sg, ret_norm,
                                      None, None, latent=False)
    o_r_s = _ret_call(lgf, lgb, q_ret, k_ret, v_ret, sg, ret_norm,
                      state_ret_fwd, state_ret_bwd, latent=True)

    tm2, tn2 = 512, 1024
    (merged,) = _matmul_call(
        [(o_a_p, o_a_s), (o_r_p, o_r_s)], [(w_branch_attn[0], 0), (w_branch_ret[0], 0)],
        [(0, 0), (1, 1)], D_MODEL, tm2, tn2, _epi_merge,
        [(gates, _tile_spec(tm2, tn2)), (gates, _tile_spec(tm2, tn2, D_MODEL // tn2))],
        [(full(D_MODEL, BF16), _tile_spec(tm2, tn2))], "merge", stream=True)

    tm3 = 512
    n_p3 = T_P // tm3
    row_vec = pl.BlockSpec((1, D_MODEL), lambda n, m: (0, 0))
    x1, h2 = _matmul_call(
        [merged], [(w_out[0], 0)], [(0, 0)], D_MODEL, tm3, D_MODEL,
        functools.partial(_epi_residual_norm, tm=tm3),
        [(xp, pl.BlockSpec((tm3, D_MODEL), lambda n, m: (jnp.minimum(m, n_p3 - 1), 0))),
         (xs, pl.BlockSpec((tm3, D_MODEL), lambda n, m: (jnp.maximum(m - n_p3, 0), 0))),
         (mod3, _mod_spec(tm3, D_MODEL, lambda n: 2, 2)),
         (norm_ffn, row_vec),
         (mod3, _mod_spec(tm3, D_MODEL, lambda n: 4, 2)),
         (mod3, _mod_spec(tm3, D_MODEL, lambda n: 3, 2))],
        [(full(D_MODEL, F32), _tile_spec(tm3, D_MODEL)),
         (full(D_MODEL, BF16), _tile_spec(tm3, D_MODEL))], "out_proj", stream=True)

    (act,) = _matmul_call(
        [h2], [(w_ffn_gate[0], 0), (w_ffn_up[0], 0)], [(0, 0), (0, 1)],
        D_FF, 2048, 512, _epi_swiglu, [],
        [(full(D_FF, BF16), _tile_spec(2048, 512))], "ffn_up", stream=True, kc=512)

    tm4, tn4 = 512, 1024
    (y_pre,) = _matmul_call(
        [act], [(w_ffn_down[0], 0)], [(0, 0)], D_MODEL, tm4, tn4, _epi_residual,
        [(x1, _tile_spec(tm4, tn4)),
         (mod3, _mod_spec(tm4, tn4, lambda n: 5 * (D_MODEL // tn4) + n, 2))],
        [(full(D_MODEL, F32), _tile_spec(tm4, tn4))], "ffn_down", stream=True, kc=512)

    y_p, y_s = _final_call(y_pre, final_norm[None, :])

    return (y_p.reshape(BATCH, SEQ, D_MODEL),
            y_s.reshape(DEC_BATCH, DEC_SEQ, D_MODEL),
            k_new[:T_P].reshape(BATCH, 1, SEQ, N_KV_HEADS, HEAD_DIM),
            v_new[:T_P].reshape(BATCH, 1, SEQ, N_KV_HEADS, HEAD_DIM),
            new_sf, new_sb)
```

```python
import functools

import jax
import jax.numpy as jnp
from jax import lax
from jax.experimental import pallas as pl
from jax.experimental.pallas import tpu as pltpu

D_MODEL = 2048
BATCH = 16
SEQ = 256
DEC_BATCH = 8
DEC_SEQ = 1024
PAST_LEN = 512
GRID_W = 64
N_HEADS = 16
N_KV_HEADS = 4
HEAD_DIM = 128
ROPE_THETA = 10000.0
N_RET_HEADS = 8
RET_KEY_DIM = 128
RET_VAL_DIM = 256
D_FF = 5632
EPS = 1e-6

ATTN_Q = N_HEADS * HEAD_DIM
ATTN_KV = N_KV_HEADS * HEAD_DIM
RET_QK = N_RET_HEADS * RET_KEY_DIM
RET_V = N_RET_HEADS * RET_VAL_DIM
GQA_GROUP = N_HEADS // N_KV_HEADS

COL_Q = 0
COL_K = COL_Q + ATTN_Q
COL_V = COL_K + ATTN_KV
COL_QR = COL_V + ATTN_KV
COL_KR = COL_QR + RET_QK
COL_VR = COL_KR + RET_QK
COL_GR = COL_VR + RET_V
COL_GATES = COL_GR + RET_V

T_P = BATCH * SEQ
T_S = DEC_BATCH * DEC_SEQ
T = T_P + T_S
MOD_ROWS = 16

V7X_VMEM_BYTES = 64 * 1024 * 1024
VMEM_CAP = V7X_VMEM_BYTES - 6 * 1024 * 1024

BF16 = jnp.bfloat16
F32 = jnp.float32


def _nbytes(shape, dtype):
    n = 1
    for s in shape:
        if s is not None:
            n *= s
    return n * jnp.dtype(dtype).itemsize


def _params(vmem_estimate, n_grid):
    limit = min(VMEM_CAP, int(vmem_estimate) + 8 * 1024 * 1024)
    return pltpu.CompilerParams(
        dimension_semantics=("arbitrary",) * n_grid, vmem_limit_bytes=limit)


def _mod_row(m, tm):
    n_p = T_P // tm
    return jnp.where(m < n_p, 0, 1 + (m - n_p) // (DEC_SEQ // tm))


def _mod_spec(tm, width, col_block, grid_rank):
    if grid_rank == 1:
        return pl.BlockSpec((None, 1, width), lambda m: (_mod_row(m, tm), 0, col_block(0)))
    return pl.BlockSpec((None, 1, width), lambda n, m: (_mod_row(m, tm), 0, col_block(n)))


def _mod_body(c_ref, w_ref, b_ref, o_ref):
    a = jax.nn.silu(c_ref[...]).astype(BF16)
    w = w_ref[...].astype(BF16)
    o_ref[...] = jnp.dot(a, w, preferred_element_type=F32) + b_ref[...]


def _mod_call(c_all, w_mod, b_mod):
    tn = 1024
    n_out = 6 * D_MODEL
    est = 2 * _nbytes((D_MODEL, tn), F32) + _nbytes((D_MODEL, tn), BF16)
    return pl.pallas_call(
        _mod_body,
        grid=(n_out // tn,),
        in_specs=[
            pl.BlockSpec((MOD_ROWS, D_MODEL), lambda n: (0, 0)),
            pl.BlockSpec((D_MODEL, tn), lambda n: (0, n)),
            pl.BlockSpec((1, tn), lambda n: (0, n)),
        ],
        out_specs=pl.BlockSpec((MOD_ROWS, tn), lambda n: (0, n)),
        out_shape=jax.ShapeDtypeStruct((MOD_ROWS, n_out), F32),
        compiler_params=_params(est, 1),
        name="mod_table",
    )(c_all, w_mod, b_mod)


def _modulated_norm(x, g, sc, sh):
    y = x * lax.rsqrt(jnp.mean(x * x, axis=-1, keepdims=True) + EPS)
    return (y * g) * (1.0 + sc) + sh


def _prenorm_body(xp_ref, xs_ref, g_ref, sc_ref, sh_ref, o_ref, *, n_p):
    m = pl.program_id(0)

    @pl.when(m < n_p)
    def _():
        o_ref[...] = _modulated_norm(xp_ref[...], g_ref[...], sc_ref[...], sh_ref[...]).astype(BF16)

    @pl.when(m >= n_p)
    def _():
        o_ref[...] = _modulated_norm(xs_ref[...], g_ref[...], sc_ref[...], sh_ref[...]).astype(BF16)


def _prenorm_call(xp, xs, norm_w, mod3, sh_blk, sc_blk, name):
    tm = 1024
    n_p = T_P // tm
    blk = (tm, D_MODEL)
    in_specs = [
        pl.BlockSpec(blk, lambda m: (jnp.minimum(m, n_p - 1), 0)),
        pl.BlockSpec(blk, lambda m: (jnp.maximum(m - n_p, 0), 0)),
        pl.BlockSpec((1, D_MODEL), lambda m: (0, 0)),
        _mod_spec(tm, D_MODEL, lambda n: sc_blk, 1),
        _mod_spec(tm, D_MODEL, lambda n: sh_blk, 1),
    ]
    est = 4 * _nbytes(blk, F32) + 2 * _nbytes(blk, BF16) + _nbytes(blk, F32)
    return pl.pallas_call(
        functools.partial(_prenorm_body, n_p=n_p),
        grid=(T // tm,),
        in_specs=in_specs,
        out_specs=pl.BlockSpec(blk, lambda m: (m, 0)),
        out_shape=jax.ShapeDtypeStruct((T, D_MODEL), BF16),
        compiler_params=_params(est, 1),
        name=name,
    )(xp, xs, norm_w, mod3, mod3)


def _matmul_body(*refs, lhs_split, n_rhs, col0s, pairs, n_extra, n_out, epilogue,
                 tm, tn, kc, row_chunk, stream, single_tile):
    n_lhs_refs = sum(2 if s else 1 for s in lhs_split)
    lhs_refs = refs[:n_lhs_refs]
    pos = n_lhs_refs
    rhs = refs[pos:pos + n_rhs]
    pos += n_rhs
    extras = refs[pos:pos + n_extra]
    pos += n_extra
    outs = refs[pos:pos + n_out]
    pos += n_out
    wb = refs[pos:pos + n_rhs]
    n, m = pl.program_id(0), pl.program_id(1)
    is_prompt = m < T_P // tm

    def lhs_rows(i, rows):
        first = sum(2 if s else 1 for s in lhs_split[:i])
        if lhs_split[i]:
            return jnp.where(is_prompt, lhs_refs[first][rows, :], lhs_refs[first + 1][rows, :])
        return lhs_refs[first][rows, :]

    def compute(weight):
        for r in range(tm // row_chunk):
            rows = slice(r * row_chunk, (r + 1) * row_chunk)
            zs = [jnp.dot(lhs_rows(i, rows), weight(j), preferred_element_type=F32) for i, j in pairs]
            epilogue(zs, extras, outs, rows)

    if not stream:
        @pl.when(m == 0)
        def _():
            for w_ref, wb_ref in zip(rhs, wb):
                wb_ref[...] = w_ref[...].astype(BF16)

        compute(lambda j: wb[j][...])
        return

    stage = refs[pos + n_rhs:pos + 2 * n_rhs]
    sems = refs[pos + 2 * n_rhs:pos + 3 * n_rhs]
    n_tiles = pl.num_programs(0)
    n_chunks = wb[0].shape[1] // kc

    def chunk_copy(j, tile, chunk, slot):
        col = pl.multiple_of(col0s[j] + tile * tn, tn)
        row = pl.multiple_of(chunk * kc, kc)
        return pltpu.make_async_copy(
            rhs[j].at[pl.ds(row, kc), pl.ds(col, tn)], stage[j].at[slot], sems[j].at[slot])

    nxt = jnp.minimum(n + 1, n_tiles - 1)
    chunk = jnp.minimum(m, n_chunks - 1)
    first = jnp.logical_and(n == 0, m == 0)

    def start_prefetch():
        for j in range(n_rhs):
            chunk_copy(j, nxt, chunk, 0).start()

    @pl.when(first)
    def _():
        for j in range(n_rhs):
            chunk_copy(j, 0, 0, 0).start()
            for c in range(n_chunks):
                if c + 1 < n_chunks:
                    chunk_copy(j, 0, c + 1, (c + 1) % 2).start()
                chunk_copy(j, 0, c, c % 2).wait()
                wb[j][0, c * kc:(c + 1) * kc, :] = stage[j][c % 2].astype(BF16)
        if not single_tile:
            start_prefetch()

    if single_tile:
        compute(lambda j: wb[j][0])
        return

    pl.when(jnp.logical_not(first))(start_prefetch)

    for cur in range(2):
        @pl.when(n % 2 == cur)
        def _(cur=cur):
            compute(lambda j: wb[j][cur])
            for j in range(n_rhs):
                chunk_copy(j, nxt, chunk, 0).wait()
                wb[j][1 - cur, pl.ds(pl.multiple_of(chunk * kc, kc), kc), :] = stage[j][0].astype(BF16)


def _matmul_call(lhs_list, rhs_list, pairs, n_cols, tm, tn, epilogue, extras, outs, name,
                 row_chunk=256, stream=False, kc=256):
    lhs_split = tuple(isinstance(a, tuple) for a in lhs_list)
    k_dim = (lhs_list[0][0] if lhs_split[0] else lhs_list[0]).shape[1]
    n_p = T_P // tm
    grid = (n_cols // tn, T // tm)
    in_specs, lhs_args = [], []
    for a in lhs_list:
        if isinstance(a, tuple):
            in_specs.append(pl.BlockSpec((tm, k_dim), lambda n, m: (jnp.minimum(m, n_p - 1), 0)))
            in_specs.append(pl.BlockSpec((tm, k_dim), lambda n, m: (jnp.maximum(m - n_p, 0), 0)))
            lhs_args += list(a)
        else:
            in_specs.append(pl.BlockSpec((tm, k_dim), lambda n, m: (m, 0)))
            lhs_args.append(a)
    est = 2 * len(lhs_args) * _nbytes((tm, k_dim), BF16)
    single_tile = stream and grid[0] == 1
    if stream:
        assert k_dim % kc == 0 and k_dim // kc <= grid[1], (k_dim, kc, grid)
        n_buf = 1 if single_tile else 2
        in_specs += [pl.BlockSpec(memory_space=pl.ANY) for _ in rhs_list]
        est += len(rhs_list) * (n_buf * _nbytes((k_dim, tn), BF16) + 2 * _nbytes((kc, tn), F32))
        scratch = [pltpu.VMEM((n_buf, k_dim, tn), BF16) for _ in rhs_list]
        scratch += [pltpu.VMEM((2, kc, tn), F32) for _ in rhs_list]
        scratch += [pltpu.SemaphoreType.DMA((2,)) for _ in rhs_list]
    else:
        for _, col0 in rhs_list:
            in_specs.append(pl.BlockSpec((k_dim, tn), lambda n, m, c=col0 // tn: (0, c + n)))
        est += len(rhs_list) * (2 * _nbytes((k_dim, tn), F32) + _nbytes((k_dim, tn), BF16))
        scratch = [pltpu.VMEM((k_dim, tn), BF16) for _ in rhs_list]
    in_specs += [spec for _, spec in extras]
    for arr, spec in list(extras) + list(outs):
        est += 2 * _nbytes(spec.block_shape, arr.dtype)
    est += 3 * len(pairs) * _nbytes((row_chunk, tn), F32)
    body = functools.partial(
        _matmul_body, lhs_split=lhs_split, n_rhs=len(rhs_list),
        col0s=tuple(c for _, c in rhs_list), pairs=tuple(pairs),
        n_extra=len(extras), n_out=len(outs), epilogue=epilogue,
        tm=tm, tn=tn, kc=kc, row_chunk=row_chunk, stream=stream, single_tile=single_tile)
    res = pl.pallas_call(
        body,
        grid=grid,
        in_specs=in_specs,
        out_specs=[spec for _, spec in outs],
        out_shape=[s for s, _ in outs],
        scratch_shapes=scratch,
        compiler_params=_params(est, 2),
        name=name,
    )(*lhs_args, *[w for w, _ in rhs_list], *[a for a, _ in extras])
    return res


def _tile_spec(tm, tn, col_block0=0):
    return pl.BlockSpec((tm, tn), lambda n, m, c=col_block0: (m, c + n))


def _prompt_tile_spec(tm, tn):
    n_p = T_P // tm
    return pl.BlockSpec((tm, tn), lambda n, m: (jnp.minimum(m, n_p), n))


def _head_rms(x, g):
    return x * lax.rsqrt(jnp.mean(x * x, axis=-1, keepdims=True) + EPS) * g


def _rope(y, c, s_lo, s_hi):
    return y * c + pltpu.roll(y, HEAD_DIM - 32, axis=1) * s_lo + pltpu.roll(y, 32, axis=1) * s_hi


def _epi_q(zs, extras, outs, rows, *, tn):
    g_ref, c_ref, slo_ref, shi_ref = extras
    (q_ref,) = outs
    (z,) = zs
    for h in range(tn // HEAD_DIM):
        sl = slice(h * HEAD_DIM, (h + 1) * HEAD_DIM)
        y = _head_rms(z[:, sl], g_ref[...])
        q_ref[rows, sl] = _rope(y, c_ref[rows, :], slo_ref[rows, :], shi_ref[rows, :]).astype(BF16)


def _epi_k(zs, extras, outs, rows, *, tn):
    g_ref, c_ref, slo_ref, shi_ref = extras
    k_att_ref, k_new_ref = outs
    (z,) = zs
    for h in range(tn // HEAD_DIM):
        sl = slice(h * HEAD_DIM, (h + 1) * HEAD_DIM)
        y = _head_rms(z[:, sl], g_ref[...])
        k_new_ref[rows, sl] = y
        k_att_ref[rows, sl] = _rope(y, c_ref[rows, :], slo_ref[rows, :], shi_ref[rows, :]).astype(BF16)


def _epi_v(zs, extras, outs, rows):
    v_att_ref, v_new_ref = outs
    v_att_ref[rows, :] = zs[0].astype(BF16)
    v_new_ref[rows, :] = zs[0]


def _epi_cast(zs, extras, outs, rows):
    outs[0][rows, :] = zs[0].astype(outs[0].dtype)


def _epi_scale(zs, extras, outs, rows, *, scale):
    outs[0][rows, :] = zs[0] * scale


def _epi_silu(zs, extras, outs, rows):
    outs[0][rows, :] = jax.nn.silu(zs[0])


def _epi_sigmoid(zs, extras, outs, rows):
    outs[0][rows, :] = jax.nn.sigmoid(zs[0])


def _epi_merge(zs, extras, outs, rows):
    ga_ref, gr_ref = extras
    outs[0][rows, :] = (ga_ref[rows, :] * zs[0] + gr_ref[rows, :] * zs[1]).astype(BF16)


def _epi_residual_norm(zs, extras, outs, rows, *, tm):
    xp_ref, xs_ref, gate_ref, g_ref, sc_ref, sh_ref = extras
    x1_ref, h_ref = outs
    x = jnp.where(pl.program_id(1) < T_P // tm, xp_ref[rows, :], xs_ref[rows, :])
    x1 = x + gate_ref[...] * zs[0]
    x1_ref[rows, :] = x1
    h_ref[rows, :] = _modulated_norm(x1, g_ref[...], sc_ref[...], sh_ref[...]).astype(BF16)


def _epi_swiglu(zs, extras, outs, rows):
    outs[0][rows, :] = (jax.nn.silu(zs[0]) * zs[1]).astype(BF16)


def _epi_residual(zs, extras, outs, rows):
    x_ref, gate_ref = extras
    outs[0][rows, :] = x_ref[rows, :] + gate_ref[...] * zs[0]


_NT = (((1,), (1,)), ((), ()))


_SOFTMAX_EXP2_SCALE = HEAD_DIM ** -0.5 * 1.4426950408889634


def _attn_body(*refs, has_ctx, seq, tq, n_sub):
    if has_ctx:
        q_ref, k_ref, v_ref, ck_ref, cv_ref, o_ref = refs
        g = pl.program_id(1)
        ck = ck_ref[:, g, :].astype(BF16)
        cv = cv_ref[:, g, :].astype(BF16)
    else:
        q_ref, k_ref, v_ref, o_ref = refs
    chain_rows = tq
    for i, c in [(i, c) for i in range(n_sub) for c in range(tq // chain_rows)]:
        k = k_ref[i * seq:(i + 1) * seq, :]
        v = v_ref[i * seq:(i + 1) * seq, :]
        rows = slice(i * tq + c * chain_rows, i * tq + (c + 1) * chain_rows)
        for h in range(GQA_GROUP):
            sl = slice(h * HEAD_DIM, (h + 1) * HEAD_DIM)
            q = q_ref[rows, sl]
            s_new = lax.dot_general(q, k, _NT, preferred_element_type=F32)
            mx = jnp.max(s_new, axis=-1, keepdims=True)
            if has_ctx:
                s_ctx = lax.dot_general(q, ck, _NT, preferred_element_type=F32)
                mx = jnp.maximum(mx, jnp.max(s_ctx, axis=-1, keepdims=True))
                p_ctx = jnp.exp2((s_ctx - mx) * _SOFTMAX_EXP2_SCALE)
                p_new = jnp.exp2((s_new - mx) * _SOFTMAX_EXP2_SCALE)
                den = jnp.sum(p_ctx, axis=-1, keepdims=True) + jnp.sum(p_new, axis=-1, keepdims=True)
                o = jnp.dot(p_ctx.astype(BF16), cv, preferred_element_type=F32)
                o = o + jnp.dot(p_new.astype(BF16), v, preferred_element_type=F32)
            else:
                p_new = jnp.exp2((s_new - mx) * _SOFTMAX_EXP2_SCALE)
                den = jnp.sum(p_new, axis=-1, keepdims=True)
                o = jnp.dot(p_new.astype(BF16), v, preferred_element_type=F32)
            o_ref[rows, sl] = (o * (1.0 / den)).astype(BF16)


def _attn_call(q_att, k_att, v_att, ctx_k, ctx_v, *, latent):
    gw = GQA_GROUP * HEAD_DIM
    if latent:
        seq, nb, tq, n_sub = DEC_SEQ, DEC_BATCH, 1024, 1
        row0 = T_P
    else:
        seq, nb, tq, n_sub = SEQ, BATCH, SEQ, 4
        row0 = 0
    nqt = seq // tq
    q_rows, kv_rows = n_sub * tq, n_sub * seq
    q_spec = pl.BlockSpec((q_rows, gw), lambda b, g, t: (row0 // q_rows + b * nqt + t, g))
    o_spec = pl.BlockSpec((q_rows, gw), lambda b, g, t: (b * nqt + t, g))
    kv_spec = pl.BlockSpec((kv_rows, HEAD_DIM), lambda b, g, t: (row0 // kv_rows + b, g))
    in_specs = [q_spec, kv_spec, kv_spec]
    args = [q_att, k_att, v_att]
    n_keys = seq
    if latent:
        c_spec = pl.BlockSpec((None, None, PAST_LEN, N_KV_HEADS, HEAD_DIM),
                              lambda b, g, t: (b, 0, 0, 0, 0))
        in_specs += [c_spec, c_spec]
        args += [ctx_k, ctx_v]
        n_keys += PAST_LEN
    est = 4 * _nbytes((q_rows, gw), BF16) + 4 * _nbytes((kv_rows, HEAD_DIM), BF16)
    est += 4 * _nbytes((PAST_LEN, 8, HEAD_DIM), F32)
    est += 3 * GQA_GROUP * n_sub * _nbytes((tq, n_keys), F32)
    return pl.pallas_call(
        functools.partial(_attn_body, has_ctx=latent, seq=seq, tq=tq, n_sub=n_sub),
        grid=(nb // n_sub, N_KV_HEADS, nqt),
        in_specs=in_specs,
        out_specs=o_spec,
        out_shape=jax.ShapeDtypeStruct((nb * seq, ATTN_Q), BF16),
        compiler_params=_params(est, 3),
        name="attn_latent" if latent else "attn_prompt",
    )(*args)


_TN = (((0,), (0,)), ((), ()))


def _ret_body(*refs, seq, chunk, tq, latent, n_sub):
    if latent:
        (lgf_ref, lgb_ref, q_ref, k_ref, v_ref, sg_ref, rn_ref, s0f_ref, s0b_ref,
         o_ref, d_ref, zf_ref, zb_ref, xif_ref, xib_ref) = refs
    else:
        (lgf_ref, lgb_ref, q_ref, k_ref, v_ref, sg_ref, rn_ref,
         o_ref, sf_ref, sb_ref, d_ref, zf_ref, zb_ref) = refs
    h = pl.program_id(0)
    lgf = lgf_ref[h]
    lgb = lgb_ref[h]
    n_c = seq // chunk

    @pl.when(pl.program_id(1) == 0)
    def _():
        for t in range(chunk // tq):
            i = lax.broadcasted_iota(jnp.int32, (tq, chunk), 0) + t * tq
            j = lax.broadcasted_iota(jnp.int32, (tq, chunk), 1)
            diff = (i - j).astype(F32)
            arg = jnp.where(diff >= 0, diff * lgf, -diff * lgb)
            d_ref[t * tq:(t + 1) * tq, :] = jnp.exp(arg) * jnp.where(diff == 0, 2.0, 1.0)
        loc_k = lax.broadcasted_iota(jnp.int32, (chunk, RET_KEY_DIM), 0).astype(F32)
        zf_ref[...] = jnp.exp((chunk - 1.0 - loc_k) * lgf)
        zb_ref[...] = jnp.exp(loc_k * lgb)
        if latent:
            loc_v = lax.broadcasted_iota(jnp.int32, (chunk, RET_VAL_DIM), 0).astype(F32)
            xif_ref[...] = jnp.exp((loc_v + 1.0) * lgf)
            xib_ref[...] = jnp.exp((chunk - loc_v) * lgb)

    def chunk_state(k_f32, z_ref, v):
        return lax.dot_general((k_f32 * z_ref[...]).astype(BF16), v, _TN, preferred_element_type=F32)

    for i in range(n_sub):
        ks = [k_ref[i * seq + c * chunk:i * seq + (c + 1) * chunk, :] for c in range(n_c)]
        vs = [v_ref[i * seq + c * chunk:i * seq + (c + 1) * chunk, :] for c in range(n_c)]
        if latent:
            st_shape = (RET_KEY_DIM, RET_VAL_DIM)
            gch_f = jnp.exp(jnp.zeros(st_shape, F32) + chunk * lgf)
            gch_b = jnp.exp(jnp.zeros(st_shape, F32) + chunk * lgb)
            sf_in = [s0f_ref[i]]
            for c in range(1, n_c):
                sf_in.append(gch_f * sf_in[-1] + chunk_state(ks[c - 1], zf_ref, vs[c - 1]))
            sb_in = [s0b_ref[i]]
            for c in range(n_c - 2, -1, -1):
                sb_in.insert(0, gch_b * sb_in[0] + chunk_state(ks[c + 1], zb_ref, vs[c + 1]))
        for c in range(n_c):
            kb = ks[c].astype(BF16)
            if latent:
                sf_c = sf_in[c].astype(BF16)
                sb_c = sb_in[c].astype(BF16)
            for t in range(chunk // tq):
                tile = slice(t * tq, (t + 1) * tq)
                rows = slice(i * seq + c * chunk + t * tq, i * seq + c * chunk + (t + 1) * tq)
                q = q_ref[rows, :]
                raw = lax.dot_general(q, kb, _NT, preferred_element_type=F32)
                p = (raw * d_ref[tile, :]).astype(BF16)
                o = jnp.dot(p, vs[c], preferred_element_type=F32)
                if latent:
                    o = o + jnp.dot(q, sf_c, preferred_element_type=F32) * xif_ref[tile, :]
                    o = o + jnp.dot(q, sb_c, preferred_element_type=F32) * xib_ref[tile, :]
                mu = jnp.mean(o, axis=-1, keepdims=True)
                oc = o - mu
                var = jnp.mean(oc * oc, axis=-1, keepdims=True)
                y = (oc * lax.rsqrt(var + EPS)) * rn_ref[...]
                o_ref[rows, :] = (y * sg_ref[rows, :]).astype(BF16)
        if not latent:
            sf_ref[i] = chunk_state(ks[0], zf_ref, vs[0])
            sb_ref[i] = chunk_state(ks[0], zb_ref, vs[0])


def _ret_call(lgf, lgb, q_ret, k_ret, v_ret, sg, ret_norm, s0f, s0b, *, latent):
    if latent:
        seq, chunk, nb, row0, n_sub = DEC_SEQ, 256, DEC_BATCH, T_P, 2
    else:
        seq, chunk, nb, row0, n_sub = SEQ, SEQ, BATCH, 0, 8
    tq = 256
    rows = n_sub * seq
    rb = row0 // rows
    smem = pl.BlockSpec(memory_space=pltpu.SMEM)
    qk_spec = pl.BlockSpec((rows, RET_KEY_DIM), lambda h, b: (rb + b, h))
    v_spec = pl.BlockSpec((rows, RET_VAL_DIM), lambda h, b: (rb + b, h))
    o_spec = pl.BlockSpec((rows, RET_VAL_DIM), lambda h, b: (b, h))
    in_specs = [smem, smem, qk_spec, qk_spec, v_spec, v_spec,
                pl.BlockSpec((1, RET_VAL_DIM), lambda h, b: (0, h))]
    args = [lgf, lgb, q_ret, k_ret, v_ret, sg, ret_norm]
    o_shape = jax.ShapeDtypeStruct((nb * seq, RET_V), BF16)
    st_spec = pl.BlockSpec((n_sub, None, None, RET_KEY_DIM, RET_VAL_DIM),
                           lambda h, b: (b, 0, h, 0, 0))
    if latent:
        in_specs += [st_spec, st_spec]
        args += [s0f, s0b]
        out_specs = o_spec
        out_shape = o_shape
    else:
        st_shape = jax.ShapeDtypeStruct((BATCH, 1, N_RET_HEADS, RET_KEY_DIM, RET_VAL_DIM), F32)
        out_specs = [o_spec, st_spec, st_spec]
        out_shape = [o_shape, st_shape, st_shape]
    est = _nbytes((chunk, chunk), F32) + 4 * n_sub * _nbytes((tq, chunk), F32)
    est += 2 * _nbytes((chunk, RET_KEY_DIM), F32) + 2 * _nbytes((chunk, RET_VAL_DIM), F32)
    est += 2 * (_nbytes((rows, RET_KEY_DIM), BF16) + _nbytes((rows, RET_KEY_DIM), F32))
    est += 2 * (2 * _nbytes((rows, RET_VAL_DIM), BF16) + _nbytes((rows, RET_VAL_DIM), F32))
    est += 8 * n_sub * _nbytes((RET_KEY_DIM, RET_VAL_DIM), F32)
    return pl.pallas_call(
        functools.partial(_ret_body, seq=seq, chunk=chunk, tq=tq, latent=latent, n_sub=n_sub),
        grid=(N_RET_HEADS, nb // n_sub),
        in_specs=in_specs,
        out_specs=out_specs,
        out_shape=out_shape,
        scratch_shapes=[pltpu.VMEM((chunk, chunk), F32)]
        + [pltpu.VMEM((chunk, RET_KEY_DIM), F32)] * 2
        + ([pltpu.VMEM((chunk, RET_VAL_DIM), F32)] * 2 if latent else []),
        compiler_params=_params(est, 2),
        name="ret_latent" if latent else "ret_prompt",
    )(*args)


def _final_body(x_ref, g_ref, yp_ref, ys_ref, *, n_p):
    x = x_ref[...]
    y = x * lax.rsqrt(jnp.mean(x * x, axis=-1, keepdims=True) + EPS) * g_ref[...]
    m = pl.program_id(0)

    @pl.when(m < n_p)
    def _():
        yp_ref[...] = y

    @pl.when(m >= n_p)
    def _():
        ys_ref[...] = y


def _final_call(x, g):
    tm = 1024
    n_p = T_P // tm
    blk = (tm, D_MODEL)
    est = 6 * _nbytes(blk, F32) + _nbytes(blk, F32)
    return pl.pallas_call(
        functools.partial(_final_body, n_p=n_p),
        grid=(T // tm,),
        in_specs=[pl.BlockSpec(blk, lambda m: (m, 0)), pl.BlockSpec((1, D_MODEL), lambda m: (0, 0))],
        out_specs=[
            pl.BlockSpec(blk, lambda m: (jnp.minimum(m, n_p - 1), 0)),
            pl.BlockSpec(blk, lambda m: (jnp.maximum(m - n_p, 0), 0)),
        ],
        out_shape=[jax.ShapeDtypeStruct((T_P, D_MODEL), F32), jax.ShapeDtypeStruct((T_S, D_MODEL), F32)],
        compiler_params=_params(est, 1),
        name="final_norm",
    )(x, g)


def _rope_tables():
    rows = DEC_SEQ // GRID_W
    row = jnp.repeat(jnp.arange(rows, dtype=F32), GRID_W)
    col = jnp.tile(jnp.arange(GRID_W, dtype=F32), rows)
    quarter = HEAD_DIM // 4
    inv_freq = ROPE_THETA ** (-jnp.arange(quarter, dtype=F32) / quarter)
    ang_r = row[:, None] * inv_freq[None, :]
    ang_c = col[:, None] * inv_freq[None, :]
    cr, sr, cc, sc = jnp.cos(ang_r), jnp.sin(ang_r), jnp.cos(ang_c), jnp.sin(ang_c)
    zero = jnp.zeros_like(sr)
    cos = jnp.concatenate([cr, cr, cc, cc], axis=-1)
    s_lo = jnp.concatenate([-sr, zero, -sc, zero], axis=-1)
    s_hi = jnp.concatenate([zero, sr, zero, sc], axis=-1)
    ident = jnp.zeros((DEC_SEQ, HEAD_DIM), F32)
    return (jnp.concatenate([ident + 1.0, cos], axis=0),
            jnp.concatenate([ident, s_lo], axis=0),
            jnp.concatenate([ident, s_hi], axis=0))


def kernel(x_prompt, x_sample, cache_attn_k, cache_attn_v, state_ret_fwd, state_ret_bwd, c, c_ctx,
           norm_attn, norm_ffn, w_mod, b_mod, w_in, q_norm, k_norm, ret_decay_fwd, ret_decay_bwd,
           ret_norm, w_branch_attn, w_branch_ret, w_out, w_ffn_gate, w_ffn_up, w_ffn_down, final_norm):
    xp = x_prompt.reshape(T_P, D_MODEL)
    xs = x_sample.reshape(T_S, D_MODEL)
    lgf = jax.nn.log_sigmoid(ret_decay_fwd[0].astype(F32))
    lgb = jax.nn.log_sigmoid(ret_decay_bwd[0].astype(F32))
    cos, s_lo, s_hi = _rope_tables()

    c_all = jnp.concatenate(
        [c_ctx[None, :], c, jnp.zeros((MOD_ROWS - 1 - DEC_BATCH, D_MODEL), F32)], axis=0)
    mod = _mod_call(c_all, w_mod[0], b_mod)
    mod3 = mod.reshape(MOD_ROWS, 1, 6 * D_MODEL)
    h = _prenorm_call(xp, xs, norm_attn, mod3, 0, 1, "prenorm_attn")

    w_in0 = w_in[0]
    tm = 1024
    tbl = pl.BlockSpec((DEC_SEQ, HEAD_DIM), lambda n, m: (jnp.where(m < T_P // tm, 0, 1), 0))
    hd = pl.BlockSpec((1, HEAD_DIM), lambda n, m: (0, 0))
    rope_extras = [(cos, tbl), (s_lo, tbl), (s_hi, tbl)]

    def proj(col0, n_cols, tn, epilogue, extras, outs, name):
        return _matmul_call([h], [(w_in0, col0)], [(0, 0)], n_cols, tm, tn, epilogue, extras, outs, name)

    def full(n_cols, dtype):
        return jax.ShapeDtypeStruct((T, n_cols), dtype)

    (q_att,) = proj(COL_Q, ATTN_Q, 1024, functools.partial(_epi_q, tn=1024),
                    [(q_norm, hd)] + rope_extras,
                    [(full(ATTN_Q, BF16), _tile_spec(tm, 1024))], "proj_q")
    prompt_kv = jax.ShapeDtypeStruct((T_P + tm, ATTN_KV), F32)
    k_att, k_new = proj(COL_K, ATTN_KV, ATTN_KV, functools.partial(_epi_k, tn=ATTN_KV),
                        [(k_norm, hd)] + rope_extras,
                        [(full(ATTN_KV, BF16), _tile_spec(tm, ATTN_KV)),
                         (prompt_kv, _prompt_tile_spec(tm, ATTN_KV))], "proj_k")
    v_att, v_new = proj(COL_V, ATTN_KV, ATTN_KV, _epi_v, [],
                        [(full(ATTN_KV, BF16), _tile_spec(tm, ATTN_KV)),
                         (prompt_kv, _prompt_tile_spec(tm, ATTN_KV))], "proj_v")

    def proj_tall(col0, n_cols, epilogue, dtype, name):
        return _matmul_call([h], [(w_in0, col0)], [(0, 0)], n_cols, 2048, 1024, epilogue, [],
                            [(full(n_cols, dtype), _tile_spec(2048, 1024))], name,
                            stream=True, kc=512)

    (q_ret,) = proj(COL_QR, RET_QK, 1024, _epi_cast, [],
                    [(full(RET_QK, BF16), _tile_spec(tm, 1024))], "proj_qr")
    (k_ret,) = proj(COL_KR, RET_QK, 1024, functools.partial(_epi_scale, scale=RET_KEY_DIM ** -0.5), [],
                    [(full(RET_QK, F32), _tile_spec(tm, 1024))], "proj_kr")
    (v_ret,) = proj_tall(COL_VR, RET_V, _epi_cast, BF16, "proj_vr")
    (sg,) = proj_tall(COL_GR, RET_V, _epi_silu, F32, "proj_gr")
    (gates,) = proj_tall(COL_GATES, 2 * D_MODEL, _epi_sigmoid, F32, "proj_gates")

    o_a_p = _attn_call(q_att, k_att, v_att, None, None, latent=False)
    o_a_s = _attn_call(q_att, k_att, v_att, cache_attn_k, cache_attn_v, latent=True)
    o_r_p, new_sf, new_sb = _ret_call(lgf, lgb, q_ret, k_ret, v_ret, sg, ret_norm,
                                      None, None, latent=False)
    o_r_s = _ret_call(lgf, lgb, q_ret, k_ret, v_ret, sg, ret_norm,
                      state_ret_fwd, state_ret_bwd, latent=True)

    tm2, tn2 = 512, 1024
    (merged,) = _matmul_call(
        [(o_a_p, o_a_s), (o_r_p, o_r_s)], [(w_branch_attn[0], 0), (w_branch_ret[0], 0)],
        [(0, 0), (1, 1)], D_MODEL, tm2, tn2, _epi_merge,
        [(gates, _tile_spec(tm2, tn2)), (gates, _tile_spec(tm2, tn2, D_MODEL // tn2))],
        [(full(D_MODEL, BF16), _tile_spec(tm2, tn2))], "merge", stream=True)

    tm3 = 512
    n_p3 = T_P // tm3
    row_vec = pl.BlockSpec((1, D_MODEL), lambda n, m: (0, 0))
    x1, h2 = _matmul_call(
        [merged], [(w_out[0], 0)], [(0, 0)], D_MODEL, tm3, D_MODEL,
        functools.partial(_epi_residual_norm, tm=tm3),
        [(xp, pl.BlockSpec((tm3, D_MODEL), lambda n, m: (jnp.minimum(m, n_p3 - 1), 0))),
         (xs, pl.BlockSpec((tm3, D_MODEL), lambda n, m: (jnp.maximum(m - n_p3, 0), 0))),
         (mod3, _mod_spec(tm3, D_MODEL, lambda n: 2, 2)),
         (norm_ffn, row_vec),
         (mod3, _mod_spec(tm3, D_MODEL, lambda n: 4, 2)),
         (mod3, _mod_spec(tm3, D_MODEL, lambda n: 3, 2))],
        [(full(D_MODEL, F32), _tile_spec(tm3, D_MODEL)),
         (full(D_MODEL, BF16), _tile_spec(tm3, D_MODEL))], "out_proj", stream=True)

    (act,) = _matmul_call(
        [h2], [(w_ffn_gate[0], 0), (w_ffn_up[0], 0)], [(0, 0), (0, 1)],
        D_FF, 2048, 512, _epi_swiglu, [],
        [(full(D_FF, BF16), _tile_spec(2048, 512))], "ffn_up", stream=True, kc=512)

    tm4, tn4 = 512, 1024
    (y_pre,) = _matmul_call(
        [act], [(w_ffn_down[0], 0)], [(0, 0)], D_MODEL, tm4, tn4, _epi_residual,
        [(x1, _tile_spec(tm4, tn4)),
         (mod3, _mod_spec(tm4, tn4, lambda n: 5 * (D_MODEL // tn4) + n, 2))],
        [(full(D_MODEL, F32), _tile_spec(tm4, tn4))], "ffn_down", stream=True, kc=512)

    y_p, y_s = _final_call(y_pre, final_norm[None, :])

    return (y_p.reshape(BATCH, SEQ, D_MODEL),
            y_s.reshape(DEC_BATCH, DEC_SEQ, D_MODEL),
            k_new[:T_P].reshape(BATCH, 1, SEQ, N_KV_HEADS, HEAD_DIM),
            v_new[:T_P].reshape(BATCH, 1, SEQ, N_KV_HEADS, HEAD_DIM),
            new_sf, new_sb)
```

```python
import functools

import jax
import jax.numpy as jnp
from jax import lax
from jax.experimental import pallas as pl
from jax.experimental.pallas import tpu as pltpu

D_MODEL = 2048
BATCH = 16
SEQ = 256
DEC_BATCH = 8
DEC_SEQ = 1024
PAST_LEN = 512
GRID_W = 64
N_HEADS = 16
N_KV_HEADS = 4
HEAD_DIM = 128
ROPE_THETA = 10000.0
N_RET_HEADS = 8
RET_KEY_DIM = 128
RET_VAL_DIM = 256
D_FF = 5632
EPS = 1e-6

ATTN_Q = N_HEADS * HEAD_DIM
ATTN_KV = N_KV_HEADS * HEAD_DIM
RET_QK = N_RET_HEADS * RET_KEY_DIM
RET_V = N_RET_HEADS * RET_VAL_DIM
GQA_GROUP = N_HEADS // N_KV_HEADS

COL_Q = 0
COL_K = COL_Q + ATTN_Q
COL_V = COL_K + ATTN_KV
COL_QR = COL_V + ATTN_KV
COL_KR = COL_QR + RET_QK
COL_VR = COL_KR + RET_QK
COL_GR = COL_VR + RET_V
COL_GATES = COL_GR + RET_V

T_P = BATCH * SEQ
T_S = DEC_BATCH * DEC_SEQ
T = T_P + T_S
MOD_ROWS = 16

V7X_LANES = 128
V7X_VMEM_BYTES = 64 * 1024 * 1024
VMEM_CAP = V7X_VMEM_BYTES - 6 * 1024 * 1024

BF16 = jnp.bfloat16
F32 = jnp.float32


def _nbytes(shape, dtype):
    n = 1
    for s in shape:
        if s is not None:
            n *= s
    return n * jnp.dtype(dtype).itemsize


def _params(vmem_estimate, n_grid):
    limit = min(VMEM_CAP, int(vmem_estimate) + 8 * 1024 * 1024)
    return pltpu.CompilerParams(
        dimension_semantics=("arbitrary",) * n_grid, vmem_limit_bytes=limit)


def _mod_row(m, tm):
    n_p = T_P // tm
    return jnp.where(m < n_p, 0, 1 + (m - n_p) // (DEC_SEQ // tm))


def _mod_spec(tm, width, col_block, grid_rank):
    if grid_rank == 1:
        return pl.BlockSpec((None, 1, width), lambda m: (_mod_row(m, tm), 0, col_block(0)))
    return pl.BlockSpec((None, 1, width), lambda n, m: (_mod_row(m, tm), 0, col_block(n)))


def _mod_body(c_ref, w_ref, b_ref, o_ref):
    a = jax.nn.silu(c_ref[...]).astype(BF16)
    w = w_ref[...].astype(BF16)
    o_ref[...] = jnp.dot(a, w, preferred_element_type=F32) + b_ref[...]


def _mod_call(c_all, w_mod, b_mod):
    tn = 1024
    n_out = 6 * D_MODEL
    est = 2 * _nbytes((D_MODEL, tn), F32) + _nbytes((D_MODEL, tn), BF16)
    return pl.pallas_call(
        _mod_body,
        grid=(n_out // tn,),
        in_specs=[
            pl.BlockSpec((MOD_ROWS, D_MODEL), lambda n: (0, 0)),
            pl.BlockSpec((D_MODEL, tn), lambda n: (0, n)),
            pl.BlockSpec((1, tn), lambda n: (0, n)),
        ],
        out_specs=pl.BlockSpec((MOD_ROWS, tn), lambda n: (0, n)),
        out_shape=jax.ShapeDtypeStruct((MOD_ROWS, n_out), F32),
        compiler_params=_params(est, 1),
        name="mod_table",
    )(c_all, w_mod, b_mod)


def _modulated_norm(x, g, sc, sh):
    y = x * lax.rsqrt(jnp.mean(x * x, axis=-1, keepdims=True) + EPS)
    return (y * g) * (1.0 + sc) + sh


def _prenorm_body(xp_ref, xs_ref, g_ref, sc_ref, sh_ref, o_ref, *, n_p):
    m = pl.program_id(0)

    @pl.when(m < n_p)
    def _():
        o_ref[...] = _modulated_norm(xp_ref[...], g_ref[...], sc_ref[...], sh_ref[...]).astype(BF16)

    @pl.when(m >= n_p)
    def _():
        o_ref[...] = _modulated_norm(xs_ref[...], g_ref[...], sc_ref[...], sh_ref[...]).astype(BF16)


def _prenorm_call(xp, xs, norm_w, mod3, sh_blk, sc_blk, name):
    tm = 1024
    n_p = T_P // tm
    blk = (tm, D_MODEL)
    in_specs = [
        pl.BlockSpec(blk, lambda m: (jnp.minimum(m, n_p - 1), 0)),
        pl.BlockSpec(blk, lambda m: (jnp.maximum(m - n_p, 0), 0)),
        pl.BlockSpec((1, D_MODEL), lambda m: (0, 0)),
        _mod_spec(tm, D_MODEL, lambda n: sc_blk, 1),
        _mod_spec(tm, D_MODEL, lambda n: sh_blk, 1),
    ]
    est = 4 * _nbytes(blk, F32) + 2 * _nbytes(blk, BF16) + _nbytes(blk, F32)
    return pl.pallas_call(
        functools.partial(_prenorm_body, n_p=n_p),
        grid=(T // tm,),
        in_specs=in_specs,
        out_specs=pl.BlockSpec(blk, lambda m: (m, 0)),
        out_shape=jax.ShapeDtypeStruct((T, D_MODEL), BF16),
        compiler_params=_params(est, 1),
        name=name,
    )(xp, xs, norm_w, mod3, mod3)


def _matmul_body(*refs, lhs_split, n_rhs, col0s, pairs, n_extra, n_out, epilogue,
                 tm, tn, kc, row_chunk, stream, single_tile):
    n_lhs_refs = sum(2 if s else 1 for s in lhs_split)
    lhs_refs = refs[:n_lhs_refs]
    pos = n_lhs_refs
    rhs = refs[pos:pos + n_rhs]
    pos += n_rhs
    extras = refs[pos:pos + n_extra]
    pos += n_extra
    outs = refs[pos:pos + n_out]
    pos += n_out
    wb = refs[pos:pos + n_rhs]
    n, m = pl.program_id(0), pl.program_id(1)
    is_prompt = m < T_P // tm

    def lhs_rows(i, rows):
        first = sum(2 if s else 1 for s in lhs_split[:i])
        if lhs_split[i]:
            return jnp.where(is_prompt, lhs_refs[first][rows, :], lhs_refs[first + 1][rows, :])
        return lhs_refs[first][rows, :]

    def compute(weight):
        for r in range(tm // row_chunk):
            rows = slice(r * row_chunk, (r + 1) * row_chunk)
            zs = [jnp.dot(lhs_rows(i, rows), weight(j), preferred_element_type=F32) for i, j in pairs]
            epilogue(zs, extras, outs, rows)

    if not stream:
        @pl.when(m == 0)
        def _():
            for w_ref, wb_ref in zip(rhs, wb):
                wb_ref[...] = w_ref[...].astype(BF16)

        compute(lambda j: wb[j][...])
        return

    stage = refs[pos + n_rhs:pos + 2 * n_rhs]
    sems = refs[pos + 2 * n_rhs:pos + 3 * n_rhs]
    n_tiles = pl.num_programs(0)
    n_chunks = wb[0].shape[1] // kc

    def chunk_copy(j, tile, chunk, slot):
        col = pl.multiple_of(col0s[j] + tile * tn, V7X_LANES)
        row = pl.multiple_of(chunk * kc, kc)
        return pltpu.make_async_copy(
            rhs[j].at[pl.ds(row, kc), pl.ds(col, tn)], stage[j].at[slot], sems[j].at[slot])

    nxt = jnp.minimum(n + 1, n_tiles - 1)
    chunk = jnp.minimum(m, n_chunks - 1)
    first = jnp.logical_and(n == 0, m == 0)

    def start_prefetch():
        for j in range(n_rhs):
            chunk_copy(j, nxt, chunk, 0).start()

    @pl.when(first)
    def _():
        for j in range(n_rhs):
            chunk_copy(j, 0, 0, 0).start()
            for c in range(n_chunks):
                if c + 1 < n_chunks:
                    chunk_copy(j, 0, c + 1, (c + 1) % 2).start()
                chunk_copy(j, 0, c, c % 2).wait()
                wb[j][0, c * kc:(c + 1) * kc, :] = stage[j][c % 2].astype(BF16)
        if not single_tile:
            start_prefetch()

    if single_tile:
        compute(lambda j: wb[j][0])
        return

    pl.when(jnp.logical_not(first))(start_prefetch)

    for cur in range(2):
        @pl.when(n % 2 == cur)
        def _(cur=cur):
            compute(lambda j: wb[j][cur])
            for j in range(n_rhs):
                chunk_copy(j, nxt, chunk, 0).wait()
                wb[j][1 - cur, pl.ds(pl.multiple_of(chunk * kc, kc), kc), :] = stage[j][0].astype(BF16)


def _matmul_call(lhs_list, rhs_list, pairs, n_cols, tm, tn, epilogue, extras, outs, name,
                 row_chunk=256, stream=False, kc=256):
    lhs_split = tuple(isinstance(a, tuple) for a in lhs_list)
    k_dim = (lhs_list[0][0] if lhs_split[0] else lhs_list[0]).shape[1]
    n_p = T_P // tm
    grid = (n_cols // tn, T // tm)
    in_specs, lhs_args = [], []
    for a in lhs_list:
        if isinstance(a, tuple):
            in_specs.append(pl.BlockSpec((tm, k_dim), lambda n, m: (jnp.minimum(m, n_p - 1), 0)))
            in_specs.append(pl.BlockSpec((tm, k_dim), lambda n, m: (jnp.maximum(m - n_p, 0), 0)))
            lhs_args += list(a)
        else:
            in_specs.append(pl.BlockSpec((tm, k_dim), lambda n, m: (m, 0)))
            lhs_args.append(a)
    est = 2 * len(lhs_args) * _nbytes((tm, k_dim), BF16)
    single_tile = stream and grid[0] == 1
    if stream:
        assert k_dim % kc == 0 and k_dim // kc <= grid[1], (k_dim, kc, grid)
        n_buf = 1 if single_tile else 2
        in_specs += [pl.BlockSpec(memory_space=pl.ANY) for _ in rhs_list]
        est += len(rhs_list) * (n_buf * _nbytes((k_dim, tn), BF16) + 2 * _nbytes((kc, tn), F32))
        scratch = [pltpu.VMEM((n_buf, k_dim, tn), BF16) for _ in rhs_list]
        scratch += [pltpu.VMEM((2, kc, tn), F32) for _ in rhs_list]
        scratch += [pltpu.SemaphoreType.DMA((2,)) for _ in rhs_list]
    else:
        for _, col0 in rhs_list:
            in_specs.append(pl.BlockSpec((k_dim, tn), lambda n, m, c=col0 // tn: (0, c + n)))
        est += len(rhs_list) * (2 * _nbytes((k_dim, tn), F32) + _nbytes((k_dim, tn), BF16))
        scratch = [pltpu.VMEM((k_dim, tn), BF16) for _ in rhs_list]
    in_specs += [spec for _, spec in extras]
    for arr, spec in list(extras) + list(outs):
        est += 2 * _nbytes(spec.block_shape, arr.dtype)
    est += 3 * len(pairs) * _nbytes((row_chunk, tn), F32)
    body = functools.partial(
        _matmul_body, lhs_split=lhs_split, n_rhs=len(rhs_list),
        col0s=tuple(c for _, c in rhs_list), pairs=tuple(pairs),
        n_extra=len(extras), n_out=len(outs), epilogue=epilogue,
        tm=tm, tn=tn, kc=kc, row_chunk=row_chunk, stream=stream, single_tile=single_tile)
    res = pl.pallas_call(
        body,
        grid=grid,
        in_specs=in_specs,
        out_specs=[spec for _, spec in outs],
        out_shape=[s for s, _ in outs],
        scratch_shapes=scratch,
        compiler_params=_params(est, 2),
        name=name,
    )(*lhs_args, *[w for w, _ in rhs_list], *[a for a, _ in extras])
    return res


def _tile_spec(tm, tn, col_block0=0):
    return pl.BlockSpec((tm, tn), lambda n, m, c=col_block0: (m, c + n))


def _prompt_tile_spec(tm, tn):
    n_p = T_P // tm
    return pl.BlockSpec((tm, tn), lambda n, m: (jnp.minimum(m, n_p), n))


def _head_rms(x, g):
    return x * lax.rsqrt(jnp.mean(x * x, axis=-1, keepdims=True) + EPS) * g


def _rope(y, c, s_lo, s_hi):
    return y * c + pltpu.roll(y, HEAD_DIM - 32, axis=1) * s_lo + pltpu.roll(y, 32, axis=1) * s_hi


def _epi_q(zs, extras, outs, rows, *, tn):
    g_ref, c_ref, slo_ref, shi_ref = extras
    (q_ref,) = outs
    (z,) = zs
    for h in range(tn // HEAD_DIM):
        sl = slice(h * HEAD_DIM, (h + 1) * HEAD_DIM)
        y = _head_rms(z[:, sl], g_ref[...])
        q_ref[rows, sl] = _rope(y, c_ref[rows, :], slo_ref[rows, :], shi_ref[rows, :]).astype(BF16)


def _epi_k(zs, extras, outs, rows, *, tn):
    g_ref, c_ref, slo_ref, shi_ref = extras
    k_att_ref, k_new_ref = outs
    (z,) = zs
    for h in range(tn // HEAD_DIM):
        sl = slice(h * HEAD_DIM, (h + 1) * HEAD_DIM)
        y = _head_rms(z[:, sl], g_ref[...])
        k_new_ref[rows, sl] = y
        k_att_ref[rows, sl] = _rope(y, c_ref[rows, :], slo_ref[rows, :], shi_ref[rows, :]).astype(BF16)


def _epi_v(zs, extras, outs, rows):
    v_att_ref, v_new_ref = outs
    v_att_ref[rows, :] = zs[0].astype(BF16)
    v_new_ref[rows, :] = zs[0]


def _epi_cast(zs, extras, outs, rows):
    outs[0][rows, :] = zs[0].astype(outs[0].dtype)


def _epi_scale(zs, extras, outs, rows, *, scale):
    outs[0][rows, :] = zs[0] * scale


def _epi_silu(zs, extras, outs, rows):
    outs[0][rows, :] = jax.nn.silu(zs[0])


def _epi_sigmoid(zs, extras, outs, rows):
    outs[0][rows, :] = jax.nn.sigmoid(zs[0])


def _epi_merge(zs, extras, outs, rows):
    ga_ref, gr_ref = extras
    outs[0][rows, :] = (ga_ref[rows, :] * zs[0] + gr_ref[rows, :] * zs[1]).astype(BF16)


def _epi_residual_norm(zs, extras, outs, rows, *, tm):
    xp_ref, xs_ref, gate_ref, g_ref, sc_ref, sh_ref = extras
    x1_ref, h_ref = outs
    x = jnp.where(pl.program_id(1) < T_P // tm, xp_ref[rows, :], xs_ref[rows, :])
    x1 = x + gate_ref[...] * zs[0]
    x1_ref[rows, :] = x1
    h_ref[rows, :] = _modulated_norm(x1, g_ref[...], sc_ref[...], sh_ref[...]).astype(BF16)


def _epi_swiglu(zs, extras, outs, rows):
    outs[0][rows, :] = (jax.nn.silu(zs[0]) * zs[1]).astype(BF16)


def _epi_residual(zs, extras, outs, rows):
    x_ref, gate_ref = extras
    outs[0][rows, :] = x_ref[rows, :] + gate_ref[...] * zs[0]


_NT = (((1,), (1,)), ((), ()))


_SOFTMAX_EXP2_SCALE = HEAD_DIM ** -0.5 * 1.4426950408889634


def _attn_body(*refs, has_ctx, seq, tq, n_sub):
    if has_ctx:
        q_ref, k_ref, v_ref, ck_ref, cv_ref, o_ref = refs
        g = pl.program_id(1)
        ck = ck_ref[:, g, :].astype(BF16)
        cv = cv_ref[:, g, :].astype(BF16)
    else:
        q_ref, k_ref, v_ref, o_ref = refs
    chain_rows = tq
    for i, c in [(i, c) for i in range(n_sub) for c in range(tq // chain_rows)]:
        k = k_ref[i * seq:(i + 1) * seq, :]
        v = v_ref[i * seq:(i + 1) * seq, :]
        rows = slice(i * tq + c * chain_rows, i * tq + (c + 1) * chain_rows)
        for h in range(GQA_GROUP):
            sl = slice(h * HEAD_DIM, (h + 1) * HEAD_DIM)
            q = q_ref[rows, sl]
            s_new = lax.dot_general(q, k, _NT, preferred_element_type=F32)
            mx = jnp.max(s_new, axis=-1, keepdims=True)
            if has_ctx:
                s_ctx = lax.dot_general(q, ck, _NT, preferred_element_type=F32)
                mx = jnp.maximum(mx, jnp.max(s_ctx, axis=-1, keepdims=True))
                p_ctx = jnp.exp2((s_ctx - mx) * _SOFTMAX_EXP2_SCALE)
                p_new = jnp.exp2((s_new - mx) * _SOFTMAX_EXP2_SCALE)
                den = jnp.sum(p_ctx, axis=-1, keepdims=True) + jnp.sum(p_new, axis=-1, keepdims=True)
                o = jnp.dot(p_ctx.astype(BF16), cv, preferred_element_type=F32)
                o = o + jnp.dot(p_new.astype(BF16), v, preferred_element_type=F32)
            else:
                p_new = jnp.exp2((s_new - mx) * _SOFTMAX_EXP2_SCALE)
                den = jnp.sum(p_new, axis=-1, keepdims=True)
                o = jnp.dot(p_new.astype(BF16), v, preferred_element_type=F32)
            o_ref[rows, sl] = (o * (1.0 / den)).astype(BF16)


def _attn_call(q_att, k_att, v_att, ctx_k, ctx_v, *, latent):
    gw = GQA_GROUP * HEAD_DIM
    if latent:
        seq, nb, tq, n_sub = DEC_SEQ, DEC_BATCH, 1024, 1
        row0 = T_P
    else:
        seq, nb, tq, n_sub = SEQ, BATCH, SEQ, 8
        row0 = 0
    nqt = seq // tq
    q_rows, kv_rows = n_sub * tq, n_sub * seq
    q_spec = pl.BlockSpec((q_rows, gw), lambda b, g, t: (row0 // q_rows + b * nqt + t, g))
    o_spec = pl.BlockSpec((q_rows, gw), lambda b, g, t: (b * nqt + t, g))
    kv_spec = pl.BlockSpec((kv_rows, HEAD_DIM), lambda b, g, t: (row0 // kv_rows + b, g))
    in_specs = [q_spec, kv_spec, kv_spec]
    args = [q_att, k_att, v_att]
    n_keys = seq
    if latent:
        c_spec = pl.BlockSpec((None, None, PAST_LEN, N_KV_HEADS, HEAD_DIM),
                              lambda b, g, t: (b, 0, 0, 0, 0))
        in_specs += [c_spec, c_spec]
        args += [ctx_k, ctx_v]
        n_keys += PAST_LEN
    est = 4 * _nbytes((q_rows, gw), BF16) + 4 * _nbytes((kv_rows, HEAD_DIM), BF16)
    est += 4 * _nbytes((PAST_LEN, 8, HEAD_DIM), F32)
    est += 3 * GQA_GROUP * n_sub * _nbytes((tq, n_keys), F32)
    return pl.pallas_call(
        functools.partial(_attn_body, has_ctx=latent, seq=seq, tq=tq, n_sub=n_sub),
        grid=(nb // n_sub, N_KV_HEADS, nqt),
        in_specs=in_specs,
        out_specs=o_spec,
        out_shape=jax.ShapeDtypeStruct((nb * seq, ATTN_Q), BF16),
        compiler_params=_params(est, 3),
        name="attn_latent" if latent else "attn_prompt",
    )(*args)


_TN = (((0,), (0,)), ((), ()))


def _ret_body(*refs, seq, chunk, tq, latent, n_sub):
    if latent:
        (lgf_ref, lgb_ref, q_ref, k_ref, v_ref, sg_ref, rn_ref, s0f_ref, s0b_ref,
         o_ref, d_ref, zf_ref, zb_ref, xif_ref, xib_ref) = refs
    else:
        (lgf_ref, lgb_ref, q_ref, k_ref, v_ref, sg_ref, rn_ref,
         o_ref, sf_ref, sb_ref, d_ref, zf_ref, zb_ref) = refs
    h = pl.program_id(0)
    lgf = lgf_ref[h]
    lgb = lgb_ref[h]
    n_c = seq // chunk

    @pl.when(pl.program_id(1) == 0)
    def _():
        for t in range(chunk // tq):
            i = lax.broadcasted_iota(jnp.int32, (tq, chunk), 0) + t * tq
            j = lax.broadcasted_iota(jnp.int32, (tq, chunk), 1)
            diff = (i - j).astype(F32)
            arg = jnp.where(diff >= 0, diff * lgf, -diff * lgb)
            d_ref[t * tq:(t + 1) * tq, :] = jnp.exp(arg) * jnp.where(diff == 0, 2.0, 1.0)
        loc_k = lax.broadcasted_iota(jnp.int32, (chunk, RET_KEY_DIM), 0).astype(F32)
        zf_ref[...] = jnp.exp((chunk - 1.0 - loc_k) * lgf)
        zb_ref[...] = jnp.exp(loc_k * lgb)
        if latent:
            loc_v = lax.broadcasted_iota(jnp.int32, (chunk, RET_VAL_DIM), 0).astype(F32)
            xif_ref[...] = jnp.exp((loc_v + 1.0) * lgf)
            xib_ref[...] = jnp.exp((chunk - loc_v) * lgb)

    def chunk_state(k_f32, z_ref, v):
        return lax.dot_general((k_f32 * z_ref[...]).astype(BF16), v, _TN, preferred_element_type=F32)

    for i in range(n_sub):
        ks = [k_ref[i * seq + c * chunk:i * seq + (c + 1) * chunk, :] for c in range(n_c)]
        vs = [v_ref[i * seq + c * chunk:i * seq + (c + 1) * chunk, :] for c in range(n_c)]
        if latent:
            st_shape = (RET_KEY_DIM, RET_VAL_DIM)
            gch_f = jnp.exp(jnp.zeros(st_shape, F32) + chunk * lgf)
            gch_b = jnp.exp(jnp.zeros(st_shape, F32) + chunk * lgb)
            sf_in = [s0f_ref[i]]
            for c in range(1, n_c):
                sf_in.append(gch_f * sf_in[-1] + chunk_state(ks[c - 1], zf_ref, vs[c - 1]))
            sb_in = [s0b_ref[i]]
            for c in range(n_c - 2, -1, -1):
                sb_in.insert(0, gch_b * sb_in[0] + chunk_state(ks[c + 1], zb_ref, vs[c + 1]))
        for c in range(n_c):
            kb = ks[c].astype(BF16)
            if latent:
                sf_c = sf_in[c].astype(BF16)
                sb_c = sb_in[c].astype(BF16)
            for t in range(chunk // tq):
                tile = slice(t * tq, (t + 1) * tq)
                rows = slice(i * seq + c * chunk + t * tq, i * seq + c * chunk + (t + 1) * tq)
                q = q_ref[rows, :]
                raw = lax.dot_general(q, kb, _NT, preferred_element_type=F32)
                p = (raw * d_ref[tile, :]).astype(BF16)
                o = jnp.dot(p, vs[c], preferred_element_type=F32)
                if latent:
                    o = o + jnp.dot(q, sf_c, preferred_element_type=F32) * xif_ref[tile, :]
                    o = o + jnp.dot(q, sb_c, preferred_element_type=F32) * xib_ref[tile, :]
                mu = jnp.mean(o, axis=-1, keepdims=True)
                oc = o - mu
                var = jnp.mean(oc * oc, axis=-1, keepdims=True)
                y = (oc * lax.rsqrt(var + EPS)) * rn_ref[...]
                o_ref[rows, :] = (y * sg_ref[rows, :]).astype(BF16)
        if not latent:
            sf_ref[i] = chunk_state(ks[0], zf_ref, vs[0])
            sb_ref[i] = chunk_state(ks[0], zb_ref, vs[0])


def _ret_call(lgf, lgb, q_ret, k_ret, v_ret, sg, ret_norm, s0f, s0b, *, latent):
    if latent:
        seq, chunk, nb, row0, n_sub = DEC_SEQ, 256, DEC_BATCH, T_P, 2
    else:
        seq, chunk, nb, row0, n_sub = SEQ, SEQ, BATCH, 0, 8
    tq = 256
    rows = n_sub * seq
    rb = row0 // rows
    smem = pl.BlockSpec(memory_space=pltpu.SMEM)
    qk_spec = pl.BlockSpec((rows, RET_KEY_DIM), lambda h, b: (rb + b, h))
    v_spec = pl.BlockSpec((rows, RET_VAL_DIM), lambda h, b: (rb + b, h))
    o_spec = pl.BlockSpec((rows, RET_VAL_DIM), lambda h, b: (b, h))
    in_specs = [smem, smem, qk_spec, qk_spec, v_spec, v_spec,
                pl.BlockSpec((1, RET_VAL_DIM), lambda h, b: (0, h))]
    args = [lgf, lgb, q_ret, k_ret, v_ret, sg, ret_norm]
    o_shape = jax.ShapeDtypeStruct((nb * seq, RET_V), BF16)
    st_spec = pl.BlockSpec((n_sub, None, None, RET_KEY_DIM, RET_VAL_DIM),
                           lambda h, b: (b, 0, h, 0, 0))
    if latent:
        in_specs += [st_spec, st_spec]
        args += [s0f, s0b]
        out_specs = o_spec
        out_shape = o_shape
    else:
        st_shape = jax.ShapeDtypeStruct((BATCH, 1, N_RET_HEADS, RET_KEY_DIM, RET_VAL_DIM), F32)
        out_specs = [o_spec, st_spec, st_spec]
        out_shape = [o_shape, st_shape, st_shape]
    est = _nbytes((chunk, chunk), F32) + 4 * n_sub * _nbytes((tq, chunk), F32)
    est += 2 * _nbytes((chunk, RET_KEY_DIM), F32) + 2 * _nbytes((chunk, RET_VAL_DIM), F32)
    est += 2 * (_nbytes((rows, RET_KEY_DIM), BF16) + _nbytes((rows, RET_KEY_DIM), F32))
    est += 2 * (2 * _nbytes((rows, RET_VAL_DIM), BF16) + _nbytes((rows, RET_VAL_DIM), F32))
    est += 8 * n_sub * _nbytes((RET_KEY_DIM, RET_VAL_DIM), F32)
    return pl.pallas_call(
        functools.partial(_ret_body, seq=seq, chunk=chunk, tq=tq, latent=latent, n_sub=n_sub),
        grid=(N_RET_HEADS, nb // n_sub),
        in_specs=in_specs,
        out_specs=out_specs,
        out_shape=out_shape,
        scratch_shapes=[pltpu.VMEM((chunk, chunk), F32)]
        + [pltpu.VMEM((chunk, RET_KEY_DIM), F32)] * 2
        + ([pltpu.VMEM((chunk, RET_VAL_DIM), F32)] * 2 if latent else []),
        compiler_params=_params(est, 2),
        name="ret_latent" if latent else "ret_prompt",
    )(*args)


def _final_body(x_ref, g_ref, yp_ref, ys_ref, *, n_p):
    x = x_ref[...]
    y = x * lax.rsqrt(jnp.mean(x * x, axis=-1, keepdims=True) + EPS) * g_ref[...]
    m = pl.program_id(0)

    @pl.when(m < n_p)
    def _():
        yp_ref[...] = y

    @pl.when(m >= n_p)
    def _():
        ys_ref[...] = y


def _final_call(x, g):
    tm = 1024
    n_p = T_P // tm
    blk = (tm, D_MODEL)
    est = 6 * _nbytes(blk, F32) + _nbytes(blk, F32)
    return pl.pallas_call(
        functools.partial(_final_body, n_p=n_p),
        grid=(T // tm,),
        in_specs=[pl.BlockSpec(blk, lambda m: (m, 0)), pl.BlockSpec((1, D_MODEL), lambda m: (0, 0))],
        out_specs=[
            pl.BlockSpec(blk, lambda m: (jnp.minimum(m, n_p - 1), 0)),
            pl.BlockSpec(blk, lambda m: (jnp.maximum(m - n_p, 0), 0)),
        ],
        out_shape=[jax.ShapeDtypeStruct((T_P, D_MODEL), F32), jax.ShapeDtypeStruct((T_S, D_MODEL), F32)],
        compiler_params=_params(est, 1),
        name="final_norm",
    )(x, g)


def _rope_tables():
    rows = DEC_SEQ // GRID_W
    row = jnp.repeat(jnp.arange(rows, dtype=F32), GRID_W)
    col = jnp.tile(jnp.arange(GRID_W, dtype=F32), rows)
    quarter = HEAD_DIM // 4
    inv_freq = ROPE_THETA ** (-jnp.arange(quarter, dtype=F32) / quarter)
    ang_r = row[:, None] * inv_freq[None, :]
    ang_c = col[:, None] * inv_freq[None, :]
    cr, sr, cc, sc = jnp.cos(ang_r), jnp.sin(ang_r), jnp.cos(ang_c), jnp.sin(ang_c)
    zero = jnp.zeros_like(sr)
    cos = jnp.concatenate([cr, cr, cc, cc], axis=-1)
    s_lo = jnp.concatenate([-sr, zero, -sc, zero], axis=-1)
    s_hi = jnp.concatenate([zero, sr, zero, sc], axis=-1)
    ident = jnp.zeros((DEC_SEQ, HEAD_DIM), F32)
    return (jnp.concatenate([ident + 1.0, cos], axis=0),
            jnp.concatenate([ident, s_lo], axis=0),
            jnp.concatenate([ident, s_hi], axis=0))


def kernel(x_prompt, x_sample, cache_attn_k, cache_attn_v, state_ret_fwd, state_ret_bwd, c, c_ctx,
           norm_attn, norm_ffn, w_mod, b_mod, w_in, q_norm, k_norm, ret_decay_fwd, ret_decay_bwd,
           ret_norm, w_branch_attn, w_branch_ret, w_out, w_ffn_gate, w_ffn_up, w_ffn_down, final_norm):
    xp = x_prompt.reshape(T_P, D_MODEL)
    xs = x_sample.reshape(T_S, D_MODEL)
    lgf = jax.nn.log_sigmoid(ret_decay_fwd[0].astype(F32))
    lgb = jax.nn.log_sigmoid(ret_decay_bwd[0].astype(F32))
    cos, s_lo, s_hi = _rope_tables()

    c_all = jnp.concatenate(
        [c_ctx[None, :], c, jnp.zeros((MOD_ROWS - 1 - DEC_BATCH, D_MODEL), F32)], axis=0)
    mod = _mod_call(c_all, w_mod[0], b_mod)
    mod3 = mod.reshape(MOD_ROWS, 1, 6 * D_MODEL)
    h = _prenorm_call(xp, xs, norm_attn, mod3, 0, 1, "prenorm_attn")

    w_in0 = w_in[0]
    tm = 1024
    tbl = pl.BlockSpec((DEC_SEQ, HEAD_DIM), lambda n, m: (jnp.where(m < T_P // tm, 0, 1), 0))
    hd = pl.BlockSpec((1, HEAD_DIM), lambda n, m: (0, 0))
    rope_extras = [(cos, tbl), (s_lo, tbl), (s_hi, tbl)]

    def proj(col0, n_cols, tn, epilogue, extras, outs, name):
        return _matmul_call([h], [(w_in0, col0)], [(0, 0)], n_cols, tm, tn, epilogue, extras, outs, name)

    def full(n_cols, dtype):
        return jax.ShapeDtypeStruct((T, n_cols), dtype)

    (q_att,) = proj(COL_Q, ATTN_Q, 1024, functools.partial(_epi_q, tn=1024),
                    [(q_norm, hd)] + rope_extras,
                    [(full(ATTN_Q, BF16), _tile_spec(tm, 1024))], "proj_q")
    prompt_kv = jax.ShapeDtypeStruct((T_P + tm, ATTN_KV), F32)
    k_att, k_new = proj(COL_K, ATTN_KV, ATTN_KV, functools.partial(_epi_k, tn=ATTN_KV),
                        [(k_norm, hd)] + rope_extras,
                        [(full(ATTN_KV, BF16), _tile_spec(tm, ATTN_KV)),
                         (prompt_kv, _prompt_tile_spec(tm, ATTN_KV))], "proj_k")
    v_att, v_new = proj(COL_V, ATTN_KV, ATTN_KV, _epi_v, [],
                        [(full(ATTN_KV, BF16), _tile_spec(tm, ATTN_KV)),
                         (prompt_kv, _prompt_tile_spec(tm, ATTN_KV))], "proj_v")

    def proj_tall(col0, n_cols, epilogue, dtype, name):
        return _matmul_call([h], [(w_in0, col0)], [(0, 0)], n_cols, 2048, 1024, epilogue, [],
                            [(full(n_cols, dtype), _tile_spec(2048, 1024))], name,
                            stream=True, kc=512)

    (q_ret,) = proj(COL_QR, RET_QK, 1024, _epi_cast, [],
                    [(full(RET_QK, BF16), _tile_spec(tm, 1024))], "proj_qr")
    (k_ret,) = proj(COL_KR, RET_QK, 1024, functools.partial(_epi_scale, scale=RET_KEY_DIM ** -0.5), [],
                    [(full(RET_QK, F32), _tile_spec(tm, 1024))], "proj_kr")
    (v_ret,) = proj_tall(COL_VR, RET_V, _epi_cast, BF16, "proj_vr")
    (sg,) = proj_tall(COL_GR, RET_V, _epi_silu, F32, "proj_gr")
    (gates,) = _matmul_call([h], [(w_in0, COL_GATES)], [(0, 0)], 2 * D_MODEL, tm, D_MODEL,
                            _epi_sigmoid, [], [(full(2 * D_MODEL, F32), _tile_spec(tm, D_MODEL))],
                            "proj_gates", stream=True)

    o_a_p = _attn_call(q_att, k_att, v_att, None, None, latent=False)
    o_a_s = _attn_call(q_att, k_att, v_att, cache_attn_k, cache_attn_v, latent=True)
    o_r_p, new_sf, new_sb = _ret_call(lgf, lgb, q_ret, k_ret, v_ret, sg, ret_norm,
                                      None, None, latent=False)
    o_r_s = _ret_call(lgf, lgb, q_ret, k_ret, v_ret, sg, ret_norm,
                      state_ret_fwd, state_ret_bwd, latent=True)

    tm2, tn2 = 512, 1024
    (merged,) = _matmul_call(
        [(o_a_p, o_a_s), (o_r_p, o_r_s)], [(w_branch_attn[0], 0), (w_branch_ret[0], 0)],
        [(0, 0), (1, 1)], D_MODEL, tm2, tn2, _epi_merge,
        [(gates, _tile_spec(tm2, tn2)), (gates, _tile_spec(tm2, tn2, D_MODEL // tn2))],
        [(full(D_MODEL, BF16), _tile_spec(tm2, tn2))], "merge", stream=True)

    tm3 = 512
    n_p3 = T_P // tm3
    row_vec = pl.BlockSpec((1, D_MODEL), lambda n, m: (0, 0))
    x1, h2 = _matmul_call(
        [merged], [(w_out[0], 0)], [(0, 0)], D_MODEL, tm3, D_MODEL,
        functools.partial(_epi_residual_norm, tm=tm3),
        [(xp, pl.BlockSpec((tm3, D_MODEL), lambda n, m: (jnp.minimum(m, n_p3 - 1), 0))),
         (xs, pl.BlockSpec((tm3, D_MODEL), lambda n, m: (jnp.maximum(m - n_p3, 0), 0))),
         (mod3, _mod_spec(tm3, D_MODEL, lambda n: 2, 2)),
         (norm_ffn, row_vec),
         (mod3, _mod_spec(tm3, D_MODEL, lambda n: 4, 2)),
         (mod3, _mod_spec(tm3, D_MODEL, lambda n: 3, 2))],
        [(full(D_MODEL, F32), _tile_spec(tm3, D_MODEL)),
         (full(D_MODEL, BF16), _tile_spec(tm3, D_MODEL))], "out_proj", stream=True)

    (act,) = _matmul_call(
        [h2], [(w_ffn_gate[0], 0), (w_ffn_up[0], 0)], [(0, 0), (0, 1)],
        D_FF, 2048, 512, _epi_swiglu, [],
        [(full(D_FF, BF16), _tile_spec(2048, 512))], "ffn_up", stream=True, kc=512)

    tm4, tn4 = 512, 1024
    (y_pre,) = _matmul_call(
        [act], [(w_ffn_down[0], 0)], [(0, 0)], D_MODEL, tm4, tn4, _epi_residual,
        [(x1, _tile_spec(tm4, tn4)),
         (mod3, _mod_spec(tm4, tn4, lambda n: 5 * (D_MODEL // tn4) + n, 2))],
        [(full(D_MODEL, F32), _tile_spec(tm4, tn4))], "ffn_down", stream=True, kc=512)

    y_p, y_s = _final_call(y_pre, final_norm[None, :])

    return (y_p.reshape(BATCH, SEQ, D_MODEL),
            y_s.reshape(DEC_BATCH, DEC_SEQ, D_MODEL),
            k_new[:T_P].reshape(BATCH, 1, SEQ, N_KV_HEADS, HEAD_DIM),
            v_new[:T_P].reshape(BATCH, 1, SEQ, N_KV_HEADS, HEAD_DIM),
            new_sf, new_sb)
```

```python
import functools

import jax
import jax.numpy as jnp
from jax import lax
from jax.experimental import pallas as pl
from jax.experimental.pallas import tpu as pltpu

D_MODEL = 2048
BATCH = 16
SEQ = 256
DEC_BATCH = 8
DEC_SEQ = 1024
PAST_LEN = 512
GRID_W = 64
N_HEADS = 16
N_KV_HEADS = 4
HEAD_DIM = 128
ROPE_THETA = 10000.0
N_RET_HEADS = 8
RET_KEY_DIM = 128
RET_VAL_DIM = 256
D_FF = 5632
EPS = 1e-6

ATTN_Q = N_HEADS * HEAD_DIM
ATTN_KV = N_KV_HEADS * HEAD_DIM
RET_QK = N_RET_HEADS * RET_KEY_DIM
RET_V = N_RET_HEADS * RET_VAL_DIM
GQA_GROUP = N_HEADS // N_KV_HEADS

COL_Q = 0
COL_K = COL_Q + ATTN_Q
COL_V = COL_K + ATTN_KV
COL_QR = COL_V + ATTN_KV
COL_KR = COL_QR + RET_QK
COL_VR = COL_KR + RET_QK
COL_GR = COL_VR + RET_V
COL_GATES = COL_GR + RET_V

T_P = BATCH * SEQ
T_S = DEC_BATCH * DEC_SEQ
T = T_P + T_S
MOD_ROWS = 16

V7X_LANES = 128
V7X_VMEM_BYTES = 64 * 1024 * 1024
VMEM_CAP = V7X_VMEM_BYTES - 6 * 1024 * 1024

BF16 = jnp.bfloat16
F32 = jnp.float32


def _nbytes(shape, dtype):
    n = 1
    for s in shape:
        if s is not None:
            n *= s
    return n * jnp.dtype(dtype).itemsize


def _params(vmem_estimate, n_grid):
    limit = min(VMEM_CAP, int(vmem_estimate) + 8 * 1024 * 1024)
    return pltpu.CompilerParams(
        dimension_semantics=("arbitrary",) * n_grid, vmem_limit_bytes=limit)


def _mod_row(m, tm):
    n_p = T_P // tm
    return jnp.where(m < n_p, 0, 1 + (m - n_p) // (DEC_SEQ // tm))


def _mod_spec(tm, width, col_block, grid_rank):
    if grid_rank == 1:
        return pl.BlockSpec((None, 1, width), lambda m: (_mod_row(m, tm), 0, col_block(0)))
    return pl.BlockSpec((None, 1, width), lambda n, m: (_mod_row(m, tm), 0, col_block(n)))


def _mod_body(c_ref, w_ref, b_ref, o_ref):
    a = jax.nn.silu(c_ref[...]).astype(BF16)
    w = w_ref[...].astype(BF16)
    o_ref[...] = jnp.dot(a, w, preferred_element_type=F32) + b_ref[...]


def _mod_call(c_all, w_mod, b_mod):
    tn = 1024
    n_out = 6 * D_MODEL
    est = 2 * _nbytes((D_MODEL, tn), F32) + _nbytes((D_MODEL, tn), BF16)
    return pl.pallas_call(
        _mod_body,
        grid=(n_out // tn,),
        in_specs=[
            pl.BlockSpec((MOD_ROWS, D_MODEL), lambda n: (0, 0)),
            pl.BlockSpec((D_MODEL, tn), lambda n: (0, n)),
            pl.BlockSpec((1, tn), lambda n: (0, n)),
        ],
        out_specs=pl.BlockSpec((MOD_ROWS, tn), lambda n: (0, n)),
        out_shape=jax.ShapeDtypeStruct((MOD_ROWS, n_out), F32),
        compiler_params=_params(est, 1),
        name="mod_table",
    )(c_all, w_mod, b_mod)


def _modulated_norm(x, g, sc, sh):
    y = x * lax.rsqrt(jnp.mean(x * x, axis=-1, keepdims=True) + EPS)
    return (y * g) * (1.0 + sc) + sh


def _prenorm_body(xp_ref, xs_ref, g_ref, sc_ref, sh_ref, o_ref, *, n_p):
    m = pl.program_id(0)

    @pl.when(m < n_p)
    def _():
        o_ref[...] = _modulated_norm(xp_ref[...], g_ref[...], sc_ref[...], sh_ref[...]).astype(BF16)

    @pl.when(m >= n_p)
    def _():
        o_ref[...] = _modulated_norm(xs_ref[...], g_ref[...], sc_ref[...], sh_ref[...]).astype(BF16)


def _prenorm_call(xp, xs, norm_w, mod3, sh_blk, sc_blk, name):
    tm = 1024
    n_p = T_P // tm
    blk = (tm, D_MODEL)
    in_specs = [
        pl.BlockSpec(blk, lambda m: (jnp.minimum(m, n_p - 1), 0)),
        pl.BlockSpec(blk, lambda m: (jnp.maximum(m - n_p, 0), 0)),
        pl.BlockSpec((1, D_MODEL), lambda m: (0, 0)),
        _mod_spec(tm, D_MODEL, lambda n: sc_blk, 1),
        _mod_spec(tm, D_MODEL, lambda n: sh_blk, 1),
    ]
    est = 4 * _nbytes(blk, F32) + 2 * _nbytes(blk, BF16) + _nbytes(blk, F32)
    return pl.pallas_call(
        functools.partial(_prenorm_body, n_p=n_p),
        grid=(T // tm,),
        in_specs=in_specs,
        out_specs=pl.BlockSpec(blk, lambda m: (m, 0)),
        out_shape=jax.ShapeDtypeStruct((T, D_MODEL), BF16),
        compiler_params=_params(est, 1),
        name=name,
    )(xp, xs, norm_w, mod3, mod3)


def _matmul_body(*refs, lhs_split, n_rhs, col0s, pairs, n_extra, n_out, epilogue,
                 tm, tn, kc, row_chunk, stream, single_tile):
    n_lhs_refs = sum(2 if s else 1 for s in lhs_split)
    lhs_refs = refs[:n_lhs_refs]
    pos = n_lhs_refs
    rhs = refs[pos:pos + n_rhs]
    pos += n_rhs
    extras = refs[pos:pos + n_extra]
    pos += n_extra
    outs = refs[pos:pos + n_out]
    pos += n_out
    wb = refs[pos:pos + n_rhs]
    n, m = pl.program_id(0), pl.program_id(1)
    is_prompt = m < T_P // tm

    def lhs_rows(i, rows):
        first = sum(2 if s else 1 for s in lhs_split[:i])
        if lhs_split[i]:
            return jnp.where(is_prompt, lhs_refs[first][rows, :], lhs_refs[first + 1][rows, :])
        return lhs_refs[first][rows, :]

    def compute(weight):
        for r in range(tm // row_chunk):
            rows = slice(r * row_chunk, (r + 1) * row_chunk)
            zs = [jnp.dot(lhs_rows(i, rows), weight(j), preferred_element_type=F32) for i, j in pairs]
            epilogue(zs, extras, outs, rows)

    if not stream:
        @pl.when(m == 0)
        def _():
            for w_ref, wb_ref in zip(rhs, wb):
                wb_ref[...] = w_ref[...].astype(BF16)

        compute(lambda j: wb[j][...])
        return

    stage = refs[pos + n_rhs:pos + 2 * n_rhs]
    sems = refs[pos + 2 * n_rhs:pos + 3 * n_rhs]
    n_tiles = pl.num_programs(0)
    n_chunks = wb[0].shape[1] // kc

    def chunk_copy(j, tile, chunk, slot):
        col = pl.multiple_of(col0s[j] + tile * tn, V7X_LANES)
        row = pl.multiple_of(chunk * kc, kc)
        return pltpu.make_async_copy(
            rhs[j].at[pl.ds(row, kc), pl.ds(col, tn)], stage[j].at[slot], sems[j].at[slot])

    nxt = jnp.minimum(n + 1, n_tiles - 1)
    chunk = jnp.minimum(m, n_chunks - 1)
    first = jnp.logical_and(n == 0, m == 0)

    def start_prefetch():
        for j in range(n_rhs):
            chunk_copy(j, nxt, chunk, 0).start()

    @pl.when(first)
    def _():
        for j in range(n_rhs):
            chunk_copy(j, 0, 0, 0).start()
            for c in range(n_chunks):
                if c + 1 < n_chunks:
                    chunk_copy(j, 0, c + 1, (c + 1) % 2).start()
                chunk_copy(j, 0, c, c % 2).wait()
                wb[j][0, c * kc:(c + 1) * kc, :] = stage[j][c % 2].astype(BF16)
        if not single_tile:
            start_prefetch()

    if single_tile:
        compute(lambda j: wb[j][0])
        return

    pl.when(jnp.logical_not(first))(start_prefetch)

    for cur in range(2):
        @pl.when(n % 2 == cur)
        def _(cur=cur):
            compute(lambda j: wb[j][cur])
            for j in range(n_rhs):
                chunk_copy(j, nxt, chunk, 0).wait()
                wb[j][1 - cur, pl.ds(pl.multiple_of(chunk * kc, kc), kc), :] = stage[j][0].astype(BF16)


def _matmul_call(lhs_list, rhs_list, pairs, n_cols, tm, tn, epilogue, extras, outs, name,
                 row_chunk=256, stream=False, kc=256):
    lhs_split = tuple(isinstance(a, tuple) for a in lhs_list)
    k_dim = (lhs_list[0][0] if lhs_split[0] else lhs_list[0]).shape[1]
    n_p = T_P // tm
    grid = (n_cols // tn, T // tm)
    in_specs, lhs_args = [], []
    for a in lhs_list:
        if isinstance(a, tuple):
            in_specs.append(pl.BlockSpec((tm, k_dim), lambda n, m: (jnp.minimum(m, n_p - 1), 0)))
            in_specs.append(pl.BlockSpec((tm, k_dim), lambda n, m: (jnp.maximum(m - n_p, 0), 0)))
            lhs_args += list(a)
        else:
            in_specs.append(pl.BlockSpec((tm, k_dim), lambda n, m: (m, 0)))
            lhs_args.append(a)
    est = 2 * len(lhs_args) * _nbytes((tm, k_dim), BF16)
    single_tile = stream and grid[0] == 1
    if stream:
        assert k_dim % kc == 0 and k_dim // kc <= grid[1], (k_dim, kc, grid)
        n_buf = 1 if single_tile else 2
        in_specs += [pl.BlockSpec(memory_space=pl.ANY) for _ in rhs_list]
        est += len(rhs_list) * (n_buf * _nbytes((k_dim, tn), BF16) + 2 * _nbytes((kc, tn), F32))
        scratch = [pltpu.VMEM((n_buf, k_dim, tn), BF16) for _ in rhs_list]
        scratch += [pltpu.VMEM((2, kc, tn), F32) for _ in rhs_list]
        scratch += [pltpu.SemaphoreType.DMA((2,)) for _ in rhs_list]
    else:
        for _, col0 in rhs_list:
            in_specs.append(pl.BlockSpec((k_dim, tn), lambda n, m, c=col0 // tn: (0, c + n)))
        est += len(rhs_list) * (2 * _nbytes((k_dim, tn), F32) + _nbytes((k_dim, tn), BF16))
        scratch = [pltpu.VMEM((k_dim, tn), BF16) for _ in rhs_list]
    in_specs += [spec for _, spec in extras]
    for arr, spec in list(extras) + list(outs):
        est += 2 * _nbytes(spec.block_shape, arr.dtype)
    est += 3 * len(pairs) * _nbytes((row_chunk, tn), F32)
    body = functools.partial(
        _matmul_body, lhs_split=lhs_split, n_rhs=len(rhs_list),
        col0s=tuple(c for _, c in rhs_list), pairs=tuple(pairs),
        n_extra=len(extras), n_out=len(outs), epilogue=epilogue,
        tm=tm, tn=tn, kc=kc, row_chunk=row_chunk, stream=stream, single_tile=single_tile)
    res = pl.pallas_call(
        body,
        grid=grid,
        in_specs=in_specs,
        out_specs=[spec for _, spec in outs],
        out_shape=[s for s, _ in outs],
        scratch_shapes=scratch,
        compiler_params=_params(est, 2),
        name=name,
    )(*lhs_args, *[w for w, _ in rhs_list], *[a for a, _ in extras])
    return res


def _tile_spec(tm, tn, col_block0=0):
    return pl.BlockSpec((tm, tn), lambda n, m, c=col_block0: (m, c + n))


def _prompt_tile_spec(tm, tn):
    n_p = T_P // tm
    return pl.BlockSpec((tm, tn), lambda n, m: (jnp.minimum(m, n_p), n))


def _head_rms(x, g):
    return x * lax.rsqrt(jnp.mean(x * x, axis=-1, keepdims=True) + EPS) * g


def _rope(y, c, s_lo, s_hi):
    return y * c + pltpu.roll(y, HEAD_DIM - 32, axis=1) * s_lo + pltpu.roll(y, 32, axis=1) * s_hi


def _epi_q(zs, extras, outs, rows, *, tn):
    g_ref, c_ref, slo_ref, shi_ref = extras
    (q_ref,) = outs
    (z,) = zs
    for h in range(tn // HEAD_DIM):
        sl = slice(h * HEAD_DIM, (h + 1) * HEAD_DIM)
        y = _head_rms(z[:, sl], g_ref[...])
        q_ref[rows, sl] = _rope(y, c_ref[rows, :], slo_ref[rows, :], shi_ref[rows, :]).astype(BF16)


def _epi_k(zs, extras, outs, rows, *, tn):
    g_ref, c_ref, slo_ref, shi_ref = extras
    k_att_ref, k_new_ref = outs
    (z,) = zs
    for h in range(tn // HEAD_DIM):
        sl = slice(h * HEAD_DIM, (h + 1) * HEAD_DIM)
        y = _head_rms(z[:, sl], g_ref[...])
        k_new_ref[rows, sl] = y
        k_att_ref[rows, sl] = _rope(y, c_ref[rows, :], slo_ref[rows, :], shi_ref[rows, :]).astype(BF16)


def _epi_v(zs, extras, outs, rows):
    v_att_ref, v_new_ref = outs
    v_att_ref[rows, :] = zs[0].astype(BF16)
    v_new_ref[rows, :] = zs[0]


def _epi_cast(zs, extras, outs, rows):
    outs[0][rows, :] = zs[0].astype(outs[0].dtype)


def _epi_scale(zs, extras, outs, rows, *, scale):
    outs[0][rows, :] = zs[0] * scale


def _epi_silu(zs, extras, outs, rows):
    outs[0][rows, :] = jax.nn.silu(zs[0])


def _epi_sigmoid(zs, extras, outs, rows):
    outs[0][rows, :] = jax.nn.sigmoid(zs[0])


def _epi_merge(zs, extras, outs, rows):
    ga_ref, gr_ref = extras
    outs[0][rows, :] = (ga_ref[rows, :] * zs[0] + gr_ref[rows, :] * zs[1]).astype(BF16)


def _epi_residual_norm(zs, extras, outs, rows, *, tm):
    xp_ref, xs_ref, gate_ref, g_ref, sc_ref, sh_ref = extras
    x1_ref, h_ref = outs
    x = jnp.where(pl.program_id(1) < T_P // tm, xp_ref[rows, :], xs_ref[rows, :])
    x1 = x + gate_ref[...] * zs[0]
    x1_ref[rows, :] = x1
    h_ref[rows, :] = _modulated_norm(x1, g_ref[...], sc_ref[...], sh_ref[...]).astype(BF16)


def _epi_swiglu(zs, extras, outs, rows):
    outs[0][rows, :] = (jax.nn.silu(zs[0]) * zs[1]).astype(BF16)


def _epi_residual(zs, extras, outs, rows):
    x_ref, gate_ref = extras
    outs[0][rows, :] = x_ref[rows, :] + gate_ref[...] * zs[0]


_NT = (((1,), (1,)), ((), ()))


_SOFTMAX_EXP2_SCALE = HEAD_DIM ** -0.5 * 1.4426950408889634


def _attn_body(*refs, has_ctx, seq, tq, n_sub):
    if has_ctx:
        q_ref, k_ref, v_ref, ck_ref, cv_ref, o_ref = refs
        g = pl.program_id(1)
        ck = ck_ref[:, g, :].astype(BF16)
        cv = cv_ref[:, g, :].astype(BF16)
    else:
        q_ref, k_ref, v_ref, o_ref = refs
    chain_rows = tq
    for i, c in [(i, c) for i in range(n_sub) for c in range(tq // chain_rows)]:
        k = k_ref[i * seq:(i + 1) * seq, :]
        v = v_ref[i * seq:(i + 1) * seq, :]
        rows = slice(i * tq + c * chain_rows, i * tq + (c + 1) * chain_rows)
        for h in range(GQA_GROUP):
            sl = slice(h * HEAD_DIM, (h + 1) * HEAD_DIM)
            q = q_ref[rows, sl]
            s_new = lax.dot_general(q, k, _NT, preferred_element_type=F32)
            mx = jnp.max(s_new, axis=-1, keepdims=True)
            if has_ctx:
                s_ctx = lax.dot_general(q, ck, _NT, preferred_element_type=F32)
                mx = jnp.maximum(mx, jnp.max(s_ctx, axis=-1, keepdims=True))
                p_ctx = jnp.exp2((s_ctx - mx) * _SOFTMAX_EXP2_SCALE)
                p_new = jnp.exp2((s_new - mx) * _SOFTMAX_EXP2_SCALE)
                den = jnp.sum(p_ctx, axis=-1, keepdims=True) + jnp.sum(p_new, axis=-1, keepdims=True)
                o = jnp.dot(p_ctx.astype(BF16), cv, preferred_element_type=F32)
                o = o + jnp.dot(p_new.astype(BF16), v, preferred_element_type=F32)
            else:
                p_new = jnp.exp2((s_new - mx) * _SOFTMAX_EXP2_SCALE)
                den = jnp.sum(p_new, axis=-1, keepdims=True)
                o = jnp.dot(p_new.astype(BF16), v, preferred_element_type=F32)
            o_ref[rows, sl] = (o * (1.0 / den)).astype(BF16)


def _attn_call(q_att, k_att, v_att, ctx_k, ctx_v, *, latent):
    gw = GQA_GROUP * HEAD_DIM
    if latent:
        seq, nb, tq, n_sub = DEC_SEQ, DEC_BATCH, 1024, 1
        row0 = T_P
    else:
        seq, nb, tq, n_sub = SEQ, BATCH, SEQ, 8
        row0 = 0
    nqt = seq // tq
    q_rows, kv_rows = n_sub * tq, n_sub * seq
    q_spec = pl.BlockSpec((q_rows, gw), lambda b, g, t: (row0 // q_rows + b * nqt + t, g))
    o_spec = pl.BlockSpec((q_rows, gw), lambda b, g, t: (b * nqt + t, g))
    kv_spec = pl.BlockSpec((kv_rows, HEAD_DIM), lambda b, g, t: (row0 // kv_rows + b, g))
    in_specs = [q_spec, kv_spec, kv_spec]
    args = [q_att, k_att, v_att]
    n_keys = seq
    if latent:
        c_spec = pl.BlockSpec((None, None, PAST_LEN, N_KV_HEADS, HEAD_DIM),
                              lambda b, g, t: (b, 0, 0, 0, 0))
        in_specs += [c_spec, c_spec]
        args += [ctx_k, ctx_v]
        n_keys += PAST_LEN
    est = 4 * _nbytes((q_rows, gw), BF16) + 4 * _nbytes((kv_rows, HEAD_DIM), BF16)
    est += 4 * _nbytes((PAST_LEN, 8, HEAD_DIM), F32)
    est += 3 * GQA_GROUP * n_sub * _nbytes((tq, n_keys), F32)
    return pl.pallas_call(
        functools.partial(_attn_body, has_ctx=latent, seq=seq, tq=tq, n_sub=n_sub),
        grid=(nb // n_sub, N_KV_HEADS, nqt),
        in_specs=in_specs,
        out_specs=o_spec,
        out_shape=jax.ShapeDtypeStruct((nb * seq, ATTN_Q), BF16),
        compiler_params=_params(est, 3),
        name="attn_latent" if latent else "attn_prompt",
    )(*args)


_TN = (((0,), (0,)), ((), ()))


def _ret_body(*refs, seq, chunk, tq, latent, n_sub):
    if latent:
        (lgf_ref, lgb_ref, q_ref, k_ref, v_ref, sg_ref, rn_ref, s0f_ref, s0b_ref,
         o_ref, d_ref, zf_ref, zb_ref, xif_ref, xib_ref) = refs
    else:
        (lgf_ref, lgb_ref, q_ref, k_ref, v_ref, sg_ref, rn_ref,
         o_ref, sf_ref, sb_ref, d_ref, zf_ref, zb_ref) = refs
    h = pl.program_id(0)
    lgf = lgf_ref[h]
    lgb = lgb_ref[h]
    n_c = seq // chunk

    @pl.when(pl.program_id(1) == 0)
    def _():
        for t in range(chunk // tq):
            i = lax.broadcasted_iota(jnp.int32, (tq, chunk), 0) + t * tq
            j = lax.broadcasted_iota(jnp.int32, (tq, chunk), 1)
            diff = (i - j).astype(F32)
            arg = jnp.where(diff >= 0, diff * lgf, -diff * lgb)
            d_ref[t * tq:(t + 1) * tq, :] = jnp.exp(arg) * jnp.where(diff == 0, 2.0, 1.0)
        loc_k = lax.broadcasted_iota(jnp.int32, (chunk, RET_KEY_DIM), 0).astype(F32)
        zf_ref[...] = jnp.exp((chunk - 1.0 - loc_k) * lgf)
        zb_ref[...] = jnp.exp(loc_k * lgb)
        if latent:
            loc_v = lax.broadcasted_iota(jnp.int32, (chunk, RET_VAL_DIM), 0).astype(F32)
            xif_ref[...] = jnp.exp((loc_v + 1.0) * lgf)
            xib_ref[...] = jnp.exp((chunk - loc_v) * lgb)

    def chunk_state(k_f32, z_ref, v):
        return lax.dot_general((k_f32 * z_ref[...]).astype(BF16), v, _TN, preferred_element_type=F32)

    for i in range(n_sub):
        ks = [k_ref[i * seq + c * chunk:i * seq + (c + 1) * chunk, :] for c in range(n_c)]
        vs = [v_ref[i * seq + c * chunk:i * seq + (c + 1) * chunk, :] for c in range(n_c)]
        if latent:
            st_shape = (RET_KEY_DIM, RET_VAL_DIM)
            gch_f = jnp.exp(jnp.zeros(st_shape, F32) + chunk * lgf)
            gch_b = jnp.exp(jnp.zeros(st_shape, F32) + chunk * lgb)
            sf_in = [s0f_ref[i]]
            for c in range(1, n_c):
                sf_in.append(gch_f * sf_in[-1] + chunk_state(ks[c - 1], zf_ref, vs[c - 1]))
            sb_in = [s0b_ref[i]]
            for c in range(n_c - 2, -1, -1):
                sb_in.insert(0, gch_b * sb_in[0] + chunk_state(ks[c + 1], zb_ref, vs[c + 1]))
        for c in range(n_c):
            kb = ks[c].astype(BF16)
            if latent:
                sf_c = sf_in[c].astype(BF16)
                sb_c = sb_in[c].astype(BF16)
            for t in range(chunk // tq):
                tile = slice(t * tq, (t + 1) * tq)
                rows = slice(i * seq + c * chunk + t * tq, i * seq + c * chunk + (t + 1) * tq)
                q = q_ref[rows, :]
                raw = lax.dot_general(q, kb, _NT, preferred_element_type=F32)
                p = (raw * d_ref[tile, :]).astype(BF16)
                o = jnp.dot(p, vs[c], preferred_element_type=F32)
                if latent:
                    o = o + jnp.dot(q, sf_c, preferred_element_type=F32) * xif_ref[tile, :]
                    o = o + jnp.dot(q, sb_c, preferred_element_type=F32) * xib_ref[tile, :]
                mu = jnp.mean(o, axis=-1, keepdims=True)
                oc = o - mu
                var = jnp.mean(oc * oc, axis=-1, keepdims=True)
                y = (oc * lax.rsqrt(var + EPS)) * rn_ref[...]
                o_ref[rows, :] = (y * sg_ref[rows, :]).astype(BF16)
        if not latent:
            sf_ref[i] = chunk_state(ks[0], zf_ref, vs[0])
            sb_ref[i] = chunk_state(ks[0], zb_ref, vs[0])


def _ret_call(lgf, lgb, q_ret, k_ret, v_ret, sg, ret_norm, s0f, s0b, *, latent):
    if latent:
        seq, chunk, nb, row0, n_sub = DEC_SEQ, 256, DEC_BATCH, T_P, 2
    else:
        seq, chunk, nb, row0, n_sub = SEQ, SEQ, BATCH, 0, 8
    tq = 256
    rows = n_sub * seq
    rb = row0 // rows
    smem = pl.BlockSpec(memory_space=pltpu.SMEM)
    qk_spec = pl.BlockSpec((rows, RET_KEY_DIM), lambda h, b: (rb + b, h))
    v_spec = pl.BlockSpec((rows, RET_VAL_DIM), lambda h, b: (rb + b, h))
    o_spec = pl.BlockSpec((rows, RET_VAL_DIM), lambda h, b: (b, h))
    in_specs = [smem, smem, qk_spec, qk_spec, v_spec, v_spec,
                pl.BlockSpec((1, RET_VAL_DIM), lambda h, b: (0, h))]
    args = [lgf, lgb, q_ret, k_ret, v_ret, sg, ret_norm]
    o_shape = jax.ShapeDtypeStruct((nb * seq, RET_V), BF16)
    st_spec = pl.BlockSpec((n_sub, None, None, RET_KEY_DIM, RET_VAL_DIM),
                           lambda h, b: (b, 0, h, 0, 0))
    if latent:
        in_specs += [st_spec, st_spec]
        args += [s0f, s0b]
        out_specs = o_spec
        out_shape = o_shape
    else:
        st_shape = jax.ShapeDtypeStruct((BATCH, 1, N_RET_HEADS, RET_KEY_DIM, RET_VAL_DIM), F32)
        out_specs = [o_spec, st_spec, st_spec]
        out_shape = [o_shape, st_shape, st_shape]
    est = _nbytes((chunk, chunk), F32) + 4 * n_sub * _nbytes((tq, chunk), F32)
    est += 2 * _nbytes((chunk, RET_KEY_DIM), F32) + 2 * _nbytes((chunk, RET_VAL_DIM), F32)
    est += 2 * (_nbytes((rows, RET_KEY_DIM), BF16) + _nbytes((rows, RET_KEY_DIM), F32))
    est += 2 * (2 * _nbytes((rows, RET_VAL_DIM), BF16) + _nbytes((rows, RET_VAL_DIM), F32))
    est += 8 * n_sub * _nbytes((RET_KEY_DIM, RET_VAL_DIM), F32)
    return pl.pallas_call(
        functools.partial(_ret_body, seq=seq, chunk=chunk, tq=tq, latent=latent, n_sub=n_sub),
        grid=(N_RET_HEADS, nb // n_sub),
        in_specs=in_specs,
        out_specs=out_specs,
        out_shape=out_shape,
        scratch_shapes=[pltpu.VMEM((chunk, chunk), F32)]
        + [pltpu.VMEM((chunk, RET_KEY_DIM), F32)] * 2
        + ([pltpu.VMEM((chunk, RET_VAL_DIM), F32)] * 2 if latent else []),
        compiler_params=_params(est, 2),
        name="ret_latent" if latent else "ret_prompt",
    )(*args)


def _final_body(x_ref, g_ref, yp_ref, ys_ref, *, n_p):
    x = x_ref[...]
    y = x * lax.rsqrt(jnp.mean(x * x, axis=-1, keepdims=True) + EPS) * g_ref[...]
    m = pl.program_id(0)

    @pl.when(m < n_p)
    def _():
        yp_ref[...] = y

    @pl.when(m >= n_p)
    def _():
        ys_ref[...] = y


def _final_call(x, g):
    tm = 1024
    n_p = T_P // tm
    blk = (tm, D_MODEL)
    est = 6 * _nbytes(blk, F32) + _nbytes(blk, F32)
    return pl.pallas_call(
        functools.partial(_final_body, n_p=n_p),
        grid=(T // tm,),
        in_specs=[pl.BlockSpec(blk, lambda m: (m, 0)), pl.BlockSpec((1, D_MODEL), lambda m: (0, 0))],
        out_specs=[
            pl.BlockSpec(blk, lambda m: (jnp.minimum(m, n_p - 1), 0)),
            pl.BlockSpec(blk, lambda m: (jnp.maximum(m - n_p, 0), 0)),
        ],
        out_shape=[jax.ShapeDtypeStruct((T_P, D_MODEL), F32), jax.ShapeDtypeStruct((T_S, D_MODEL), F32)],
        compiler_params=_params(est, 1),
        name="final_norm",
    )(x, g)


def _rope_tables():
    rows = DEC_SEQ // GRID_W
    row = jnp.repeat(jnp.arange(rows, dtype=F32), GRID_W)
    col = jnp.tile(jnp.arange(GRID_W, dtype=F32), rows)
    quarter = HEAD_DIM // 4
    inv_freq = ROPE_THETA ** (-jnp.arange(quarter, dtype=F32) / quarter)
    ang_r = row[:, None] * inv_freq[None, :]
    ang_c = col[:, None] * inv_freq[None, :]
    cr, sr, cc, sc = jnp.cos(ang_r), jnp.sin(ang_r), jnp.cos(ang_c), jnp.sin(ang_c)
    zero = jnp.zeros_like(sr)
    cos = jnp.concatenate([cr, cr, cc, cc], axis=-1)
    s_lo = jnp.concatenate([-sr, zero, -sc, zero], axis=-1)
    s_hi = jnp.concatenate([zero, sr, zero, sc], axis=-1)
    ident = jnp.zeros((DEC_SEQ, HEAD_DIM), F32)
    return (jnp.concatenate([ident + 1.0, cos], axis=0),
            jnp.concatenate([ident, s_lo], axis=0),
            jnp.concatenate([ident, s_hi], axis=0))


def kernel(x_prompt, x_sample, cache_attn_k, cache_attn_v, state_ret_fwd, state_ret_bwd, c, c_ctx,
           norm_attn, norm_ffn, w_mod, b_mod, w_in, q_norm, k_norm, ret_decay_fwd, ret_decay_bwd,
           ret_norm, w_branch_attn, w_branch_ret, w_out, w_ffn_gate, w_ffn_up, w_ffn_down, final_norm):
    xp = x_prompt.reshape(T_P, D_MODEL)
    xs = x_sample.reshape(T_S, D_MODEL)
    lgf = jax.nn.log_sigmoid(ret_decay_fwd[0].astype(F32))
    lgb = jax.nn.log_sigmoid(ret_decay_bwd[0].astype(F32))
    cos, s_lo, s_hi = _rope_tables()

    c_all = jnp.concatenate(
        [c_ctx[None, :], c, jnp.zeros((MOD_ROWS - 1 - DEC_BATCH, D_MODEL), F32)], axis=0)
    mod = _mod_call(c_all, w_mod[0], b_mod)
    mod3 = mod.reshape(MOD_ROWS, 1, 6 * D_MODEL)
    h = _prenorm_call(xp, xs, norm_attn, mod3, 0, 1, "prenorm_attn")

    w_in0 = w_in[0]
    tm = 1024
    tbl = pl.BlockSpec((DEC_SEQ, HEAD_DIM), lambda n, m: (jnp.where(m < T_P // tm, 0, 1), 0))
    hd = pl.BlockSpec((1, HEAD_DIM), lambda n, m: (0, 0))
    rope_extras = [(cos, tbl), (s_lo, tbl), (s_hi, tbl)]

    def proj(col0, n_cols, tn, epilogue, extras, outs, name):
        return _matmul_call([h], [(w_in0, col0)], [(0, 0)], n_cols, tm, tn, epilogue, extras, outs, name)

    def full(n_cols, dtype):
        return jax.ShapeDtypeStruct((T, n_cols), dtype)

    (q_att,) = proj(COL_Q, ATTN_Q, 1024, functools.partial(_epi_q, tn=1024),
                    [(q_norm, hd)] + rope_extras,
                    [(full(ATTN_Q, BF16), _tile_spec(tm, 1024))], "proj_q")

    def proj_tall(col0, n_cols, epilogue, dtype, name):
        return _matmul_call([h], [(w_in0, col0)], [(0, 0)], n_cols, 2048, 1024, epilogue, [],
                            [(full(n_cols, dtype), _tile_spec(2048, 1024))], name,
                            stream=True, kc=512)

    (q_ret,) = proj(COL_QR, RET_QK, 1024, _epi_cast, [],
                    [(full(RET_QK, BF16), _tile_spec(tm, 1024))], "proj_qr")
    (k_ret,) = proj(COL_KR, RET_QK, 1024, functools.partial(_epi_scale, scale=RET_KEY_DIM ** -0.5), [],
                    [(full(RET_QK, F32), _tile_spec(tm, 1024))], "proj_kr")
    (v_ret,) = proj_tall(COL_VR, RET_V, _epi_cast, BF16, "proj_vr")
    (sg,) = proj_tall(COL_GR, RET_V, _epi_silu, F32, "proj_gr")
    (gates,) = _matmul_call([h], [(w_in0, COL_GATES)], [(0, 0)], 2 * D_MODEL, tm, D_MODEL,
                            _epi_sigmoid, [], [(full(2 * D_MODEL, F32), _tile_spec(tm, D_MODEL))],
                            "proj_gates", stream=True)
    prompt_kv = jax.ShapeDtypeStruct((T_P + tm, ATTN_KV), F32)
    k_att, k_new = proj(COL_K, ATTN_KV, ATTN_KV, functools.partial(_epi_k, tn=ATTN_KV),
                        [(k_norm, hd)] + rope_extras,
                        [(full(ATTN_KV, BF16), _tile_spec(tm, ATTN_KV)),
                         (prompt_kv, _prompt_tile_spec(tm, ATTN_KV))], "proj_k")
    v_att, v_new = proj(COL_V, ATTN_KV, ATTN_KV, _epi_v, [],
                        [(full(ATTN_KV, BF16), _tile_spec(tm, ATTN_KV)),
                         (prompt_kv, _prompt_tile_spec(tm, ATTN_KV))], "proj_v")

    o_a_p = _attn_call(q_att, k_att, v_att, None, None, latent=False)
    o_a_s = _attn_call(q_att, k_att, v_att, cache_attn_k, cache_attn_v, latent=True)
    o_r_p, new_sf, new_sb = _ret_call(lgf, lgb, q_ret, k_ret, v_ret, sg, ret_norm,
                                      None, None, latent=False)
    o_r_s = _ret_call(lgf, lgb, q_ret, k_ret, v_ret, sg, ret_norm,
                      state_ret_fwd, state_ret_bwd, latent=True)

    tm2, tn2 = 512, 1024
    (merged,) = _matmul_call(
        [(o_a_p, o_a_s), (o_r_p, o_r_s)], [(w_branch_attn[0], 0), (w_branch_ret[0], 0)],
        [(0, 0), (1, 1)], D_MODEL, tm2, tn2, _epi_merge,
        [(gates, _tile_spec(tm2, tn2)), (gates, _tile_spec(tm2, tn2, D_MODEL // tn2))],
        [(full(D_MODEL, BF16), _tile_spec(tm2, tn2))], "merge", stream=True)

    tm3 = 512
    n_p3 = T_P // tm3
    row_vec = pl.BlockSpec((1, D_MODEL), lambda n, m: (0, 0))
    x1, h2 = _matmul_call(
        [merged], [(w_out[0], 0)], [(0, 0)], D_MODEL, tm3, D_MODEL,
        functools.partial(_epi_residual_norm, tm=tm3),
        [(xp, pl.BlockSpec((tm3, D_MODEL), lambda n, m: (jnp.minimum(m, n_p3 - 1), 0))),
         (xs, pl.BlockSpec((tm3, D_MODEL), lambda n, m: (jnp.maximum(m - n_p3, 0), 0))),
         (mod3, _mod_spec(tm3, D_MODEL, lambda n: 2, 2)),
         (norm_ffn, row_vec),
         (mod3, _mod_spec(tm3, D_MODEL, lambda n: 4, 2)),
         (mod3, _mod_spec(tm3, D_MODEL, lambda n: 3, 2))],
        [(full(D_MODEL, F32), _tile_spec(tm3, D_MODEL)),
         (full(D_MODEL, BF16), _tile_spec(tm3, D_MODEL))], "out_proj", stream=True)

    (act,) = _matmul_call(
        [h2], [(w_ffn_gate[0], 0), (w_ffn_up[0], 0)], [(0, 0), (0, 1)],
        D_FF, 2048, 512, _epi_swiglu, [],
        [(full(D_FF, BF16), _tile_spec(2048, 512))], "ffn_up", stream=True, kc=512)

    tm4, tn4 = 512, 1024
    (y_pre,) = _matmul_call(
        [act], [(w_ffn_down[0], 0)], [(0, 0)], D_MODEL, tm4, tn4, _epi_residual,
        [(x1, _tile_spec(tm4, tn4)),
         (mod3, _mod_spec(tm4, tn4, lambda n: 5 * (D_MODEL // tn4) + n, 2))],
        [(full(D_MODEL, F32), _tile_spec(tm4, tn4))], "ffn_down", stream=True, kc=512)

    y_p, y_s = _final_call(y_pre, final_norm[None, :])

    return (y_p.reshape(BATCH, SEQ, D_MODEL),
            y_s.reshape(DEC_BATCH, DEC_SEQ, D_MODEL),
            k_new[:T_P].reshape(BATCH, 1, SEQ, N_KV_HEADS, HEAD_DIM),
            v_new[:T_P].reshape(BATCH, 1, SEQ, N_KV_HEADS, HEAD_DIM),
            new_sf, new_sb)
```
